```python
import jax, jax.numpy as jnp
from jax import lax
import numpy as np

D_MODEL = 1024
BATCH = 16
SEQ = 2048
DEPTH = 1

CHUNK = 64
MIX_WIDTH = D_MODEL
HEAD_DIM = 64
SB_WIDTH = MIX_WIDTH // 2
SB_HEADS = SB_WIDTH // HEAD_DIM
RW_WIDTH = MIX_WIDTH - SB_WIDTH
RW_HEADS = RW_WIDTH // HEAD_DIM
DECAY_LORA = 32
AAA_LORA = 32
GATE_LORA = 96
Q_BLOCK = 128
SB_COLS = 3 * SB_WIDTH
RW_COLS = 3 * RW_WIDTH + DECAY_LORA + AAA_LORA + GATE_LORA
IN_COLS = SB_COLS + RW_COLS
N_GROUPS = 4
EXPERTS_PER_GROUP = 8
N_EXPERTS = N_GROUPS * EXPERTS_PER_GROUP
TOP_K = 2
D_EXPERT = 512
EXPERT_BLOCK = 128
NORM_EPS = 1e-6
LN_X_EPS = 64e-5

kernel_name = 'hybrid_stickbreak_rwkv7_hmoe_block'


def rms_norm(x, gain):
    xf = x.astype(jnp.float32)
    y = xf * lax.rsqrt(jnp.mean(xf * xf, axis=-1, keepdims=True) + NORM_EPS)
    return (y * gain.astype(jnp.float32)).astype(x.dtype)


def stick_breaking_attention(q, k, v):
    S, d = q.shape[2], q.shape[3]
    scale = d ** -0.5
    outs = []
    for blk in range(S // Q_BLOCK):
        t0 = blk * Q_BLOCK
        end = t0 + Q_BLOCK
        z = jnp.einsum('bhtd,bhsd->bhts', q[:, :, t0:end], k[:, :, :end]).astype(jnp.float32) * scale
        t_idx = t0 + jnp.arange(Q_BLOCK)[:, None]
        s_idx = jnp.arange(end)[None, :]
        causal = s_idx < t_idx
        log_keep = jnp.where(causal, jax.nn.log_sigmoid(-z), 0.0)
        suffix = lax.cumsum(log_keep, axis=3, reverse=True)
        after = jnp.concatenate([suffix[..., 1:], jnp.zeros_like(suffix[..., :1])], axis=-1)
        weights = jnp.where(causal, jnp.exp(jax.nn.log_sigmoid(z) + after), 0.0)
        outs.append(jnp.einsum('bhts,bhsd->bhtd', weights.astype(v.dtype), v[:, :, :end]))
    return jnp.concatenate(outs, axis=2)


def token_shift(p, mu):
    prev = jnp.pad(p, ((0, 0), (1, 0), (0, 0)))[:, :-1]
    return p + (prev - p) * mu


def rwkv7_time_mix(p, mu, w0, w_up, a0, a_up, g_up, k_k, k_a, r_k, ln_w, ln_b):
    B, S, _ = p.shape
    f32 = jnp.float32
    pf = token_shift(p.astype(f32), mu.astype(f32))
    c = RW_WIDTH
    r = pf[..., :c]
    k = pf[..., c:2 * c]
    v = pf[..., 2 * c:3 * c]
    wd = pf[..., 3 * c:3 * c + DECAY_LORA]
    ad = pf[..., 3 * c + DECAY_LORA:3 * c + DECAY_LORA + AAA_LORA]
    gd = pf[..., 3 * c + DECAY_LORA + AAA_LORA:]
    w = -jax.nn.softplus(-(w0.astype(f32) + jnp.tanh(wd) @ w_up.astype(f32))) - 0.5
    decay = jnp.exp(-jnp.exp(w))
    a = jax.nn.sigmoid(a0.astype(f32) + ad @ a_up.astype(f32))
    g = jax.nn.sigmoid(gd) @ g_up.astype(f32)
    heads = lambda t: t.reshape(B, S, RW_HEADS, HEAD_DIM)
    r, k, v, decay, a = heads(r), heads(k), heads(v), heads(decay), heads(a)
    kk = k * k_k.astype(f32).reshape(RW_HEADS, HEAD_DIM)
    kk = kk / jnp.maximum(jnp.sqrt(jnp.sum(kk * kk, axis=-1, keepdims=True)), 1e-12)
    k = k * (1.0 + (a - 1.0) * k_a.astype(f32).reshape(RW_HEADS, HEAD_DIM))
    xs = tuple(jnp.moveaxis(t, 1, 0) for t in (r, decay, k, v, -kk, kk * a))

    def step(state, inp):
        r_t, w_t, k_t, v_t, a_t, b_t = inp
        sa = jnp.einsum('bhvk,bhk->bhv', state, a_t)
        state = (state * w_t[:, :, None, :] + sa[..., None] * b_t[:, :, None, :]
                 + v_t[..., None] * k_t[:, :, None, :])
        return state, jnp.einsum('bhvk,bhk->bhv', state, r_t)

    state0 = jnp.zeros((B, RW_HEADS, HEAD_DIM, HEAD_DIM), f32)
    _, y = lax.scan(step, state0, xs)
    y = jnp.moveaxis(y, 0, 1)
    mean = jnp.mean(y, axis=-1, keepdims=True)
    var = jnp.mean(jnp.square(y - mean), axis=-1, keepdims=True)
    yn = ((y - mean) * lax.rsqrt(var + LN_X_EPS) * ln_w.astype(f32).reshape(RW_HEADS, HEAD_DIM)
          + ln_b.astype(f32).reshape(RW_HEADS, HEAD_DIM))
    bonus = jnp.sum(r * k * r_k.astype(f32), axis=-1, keepdims=True) * v
    out = (yn + bonus).reshape(B, S, c) * g
    return out.astype(p.dtype)


def hierarchical_moe(h, w_rg, w_re, w_g, w_u, w_d):
    B, S, D = h.shape
    hf = h.reshape(-1, D)
    n_tok = B * S
    group_logits = (hf @ w_rg).astype(jnp.float32)
    group_prob = jax.nn.softmax(group_logits, axis=-1)
    group_idx = jnp.argmax(group_logits, axis=-1).astype(jnp.int32)
    group_gate = jnp.take_along_axis(group_prob, group_idx[:, None], axis=-1)
    exp_logits = (hf @ w_re).astype(jnp.float32).reshape(n_tok, N_GROUPS, EXPERTS_PER_GROUP)
    in_group = jnp.take_along_axis(exp_logits, group_idx[:, None, None], axis=1)[:, 0]
    top_p, top_i = lax.top_k(jax.nn.softmax(in_group, axis=-1), TOP_K)
    gates = group_gate * top_p / jnp.sum(top_p, axis=-1, keepdims=True)
    eid = (group_idx[:, None] * EXPERTS_PER_GROUP + top_i.astype(jnp.int32)).reshape(-1)
    n_slot = n_tok * TOP_K
    order = jnp.argsort(eid).astype(jnp.int32)
    sorted_eid = eid[order]
    counts = jnp.zeros((N_EXPERTS,), jnp.int32).at[eid].add(1)
    padded = (counts + EXPERT_BLOCK - 1) // EXPERT_BLOCK * EXPERT_BLOCK
    pad_end = jnp.cumsum(padded).astype(jnp.int32)
    pad_start = pad_end - padded
    start = jnp.cumsum(counts).astype(jnp.int32) - counts
    dest_sorted = pad_start[sorted_eid] + jnp.arange(n_slot, dtype=jnp.int32) - start[sorted_eid]
    slot_dest = jnp.zeros((n_slot,), jnp.int32).at[order].set(dest_sorted)
    n_rows = (n_slot + EXPERT_BLOCK - 1) // EXPERT_BLOCK * EXPERT_BLOCK + N_EXPERTS * EXPERT_BLOCK
    n_blocks = n_rows // EXPERT_BLOCK
    token_of_slot = jnp.arange(n_slot, dtype=jnp.int32) // TOP_K
    buf = jnp.zeros((n_rows, D), h.dtype).at[slot_dest].set(hf[token_of_slot])
    block_start = jnp.arange(n_blocks, dtype=jnp.int32) * EXPERT_BLOCK
    block_eid = jnp.minimum(jnp.sum((pad_end[None, :] <= block_start[:, None]).astype(jnp.int32), axis=1),
                            N_EXPERTS - 1)

    def expert_block(args):
        xb, e = args
        return (jax.nn.silu(xb @ w_g[e]) * (xb @ w_u[e])) @ w_d[e]

    out = lax.map(expert_block, (buf.reshape(n_blocks, EXPERT_BLOCK, D), block_eid)).reshape(n_rows, D)
    y = jnp.sum(out[slot_dest].reshape(n_tok, TOP_K, D) * gates[..., None].astype(h.dtype), axis=1)
    return y.reshape(B, S, D)


def setup_inputs(seed: int = 0) -> dict:
    key = jax.random.key(seed)
    ks = jax.random.split(key, 24)
    L = DEPTH
    f32 = jnp.float32
    nrm = lambda k, shape: jax.random.normal(k, shape, f32)
    lin = lambda k, shape, fan_in: nrm(k, shape) * fan_in ** -0.5
    return {
        'x': nrm(ks[0], (BATCH, SEQ, D_MODEL)),
        'norm1_gain': 1.0 + 0.1 * nrm(ks[1], (L, D_MODEL)),
        'w_in': lin(ks[2], (L, D_MODEL, IN_COLS), D_MODEL),
        'sb_q_gain': 1.0 + 0.1 * nrm(ks[3], (L, HEAD_DIM)),
        'sb_k_gain': 1.0 + 0.1 * nrm(ks[4], (L, HEAD_DIM)),
        'rw_shift_mu': jax.random.uniform(ks[5], (L, RW_COLS), f32),
        'rw_w0': jax.random.uniform(ks[6], (L, RW_WIDTH), f32, minval=-6.5, maxval=-1.5),
        'rw_w_up': 0.1 * lin(ks[7], (L, DECAY_LORA, RW_WIDTH), DECAY_LORA),
        'rw_a0': 0.1 * nrm(ks[8], (L, RW_WIDTH)),
        'rw_a_up': lin(ks[9], (L, AAA_LORA, RW_WIDTH), AAA_LORA),
        'rw_g_up': lin(ks[10], (L, GATE_LORA, RW_WIDTH), GATE_LORA),
        'rw_k_k': 0.85 + 0.05 * nrm(ks[11], (L, RW_WIDTH)),
        'rw_k_a': 1.0 + 0.05 * nrm(ks[12], (L, RW_WIDTH)),
        'rw_r_k': 0.1 * nrm(ks[13], (L, RW_HEADS, HEAD_DIM)),
        'rw_ln_w': 1.0 + 0.1 * nrm(ks[14], (L, RW_WIDTH)),
        'rw_ln_b': 0.02 * nrm(ks[15], (L, RW_WIDTH)),
        'w_out': lin(ks[16], (L, MIX_WIDTH, D_MODEL), MIX_WIDTH),
        'norm2_gain': 1.0 + 0.1 * nrm(ks[17], (L, D_MODEL)),
        'w_router_group': lin(ks[18], (L, D_MODEL, N_GROUPS), D_MODEL),
        'w_router_expert': lin(ks[19], (L, D_MODEL, N_EXPERTS), D_MODEL),
        'w_exp_gate': lin(ks[20], (L, N_EXPERTS, D_MODEL, D_EXPERT), D_MODEL),
        'w_exp_up': lin(ks[21], (L, N_EXPERTS, D_MODEL, D_EXPERT), D_MODEL),
        'w_exp_down': lin(ks[22], (L, N_EXPERTS, D_EXPERT, D_MODEL), D_EXPERT),
    }


def reference(x, norm1_gain, w_in, sb_q_gain, sb_k_gain, rw_shift_mu, rw_w0, rw_w_up, rw_a0,
              rw_a_up, rw_g_up, rw_k_k, rw_k_a, rw_r_k, rw_ln_w, rw_ln_b, w_out, norm2_gain,
              w_router_group, w_router_expert, w_exp_gate, w_exp_up, w_exp_down):
    B, S, _ = x.shape
    for l in range(DEPTH):
        h = rms_norm(x, norm1_gain[l])
        proj = h @ w_in[l]
        sb = proj[..., :SB_COLS]
        q = rms_norm(sb[..., :SB_WIDTH].reshape(B, S, SB_HEADS, HEAD_DIM), sb_q_gain[l]).transpose(0, 2, 1, 3)
        k = rms_norm(sb[..., SB_WIDTH:2 * SB_WIDTH].reshape(B, S, SB_HEADS, HEAD_DIM), sb_k_gain[l]).transpose(0, 2, 1, 3)
        v = sb[..., 2 * SB_WIDTH:].reshape(B, S, SB_HEADS, HEAD_DIM).transpose(0, 2, 1, 3)
        sb_out = stick_breaking_attention(q, k, v).transpose(0, 2, 1, 3).reshape(B, S, SB_WIDTH)
        rw_out = rwkv7_time_mix(proj[..., SB_COLS:], rw_shift_mu[l], rw_w0[l], rw_w_up[l], rw_a0[l],
                                rw_a_up[l], rw_g_up[l], rw_k_k[l], rw_k_a[l], rw_r_k[l],
                                rw_ln_w[l], rw_ln_b[l])
        mix = jnp.concatenate([sb_out, rw_out], axis=-1)
        x = x + mix @ w_out[l]
        h2 = rms_norm(x, norm2_gain[l])
        x = x + hierarchical_moe(h2, w_router_group[l], w_router_expert[l],
                                 w_exp_gate[l], w_exp_up[l], w_exp_down[l])
    return x
```

```python
import functools

import jax
import jax.numpy as jnp
from jax import lax
from jax.experimental import pallas as pl
from jax.experimental.pallas import tpu as pltpu

F32 = jnp.float32
BF16 = jnp.bfloat16

HEAD_DIM = 64
NORM_EPS = 1e-6
LN_X_EPS = 64e-5
N_GROUPS = 4
EXPERTS_PER_GROUP = 8
N_EXPERTS = N_GROUPS * EXPERTS_PER_GROUP
TOP_K = 2

LANES = 128
MXU_DIM = 256
HEADS_PER_MXU = MXU_DIM // HEAD_DIM
RW_CHUNK = 64
EXPERT_ROWS = 256
VMEM_LIMIT = 56 * 1024 * 1024


def _nt(a, b):
    return lax.dot_general(a, b, (((1,), (1,)), ((), ())), preferred_element_type=F32)


def _tn(a, b):
    return lax.dot_general(a, b, (((0,), (0,)), ((), ())), preferred_element_type=F32)


def _mm(a, b):
    return jnp.dot(a, b, preferred_element_type=F32)


def _split2(x):
    hi = x.astype(BF16)
    lo = (x - hi.astype(F32)).astype(BF16)
    return hi, lo


def _split3(x):
    h1 = x.astype(BF16)
    r1 = x - h1.astype(F32)
    h2 = r1.astype(BF16)
    h3 = (r1 - h2.astype(F32)).astype(BF16)
    return h1, h2, h3


def _mm_exact_rhs(x, m):
    hi, lo = _split2(x)
    return _mm(hi, m) + _mm(lo, m)


def _head_ones(width):
    r = lax.broadcasted_iota(jnp.int32, (width, width), 0) // HEAD_DIM
    c = lax.broadcasted_iota(jnp.int32, (width, width), 1) // HEAD_DIM
    return jnp.where(r == c, 1.0, 0.0).astype(BF16)


def _softplus(z):
    return jnp.maximum(z, 0.0) + jnp.log(1.0 + jnp.exp(-jnp.abs(z)))


def _sigmoid(z):
    return 1.0 / (1.0 + jnp.exp(-z))


def _inproj_kernel(x_ref, g_ref, wsb_ref, wrw_ref, wlo_ref, sb_ref, rw_ref, lo_ref):
    x = x_ref[...]
    ms = jnp.mean(x * x, axis=-1, keepdims=True)
    h = (x * lax.rsqrt(ms + NORM_EPS) * g_ref[...]).astype(BF16)
    sb_ref[...] = _mm(h, wsb_ref[...])
    rw_ref[...] = _mm(h, wrw_ref[...])
    lo_ref[...] = _mm(h, wlo_ref[...])


def _inproj(x2, gain, w_sb, w_rw, w_lo, tm):
    n, d = x2.shape
    full = lambda a: pl.BlockSpec(a.shape, lambda i: (0,) * a.ndim)
    rows = lambda c: pl.BlockSpec((tm, c), lambda i: (i, 0))
    return pl.pallas_call(
        _inproj_kernel,
        grid=(n // tm,),
        in_specs=[rows(d), full(gain), full(w_sb), full(w_rw), full(w_lo)],
        out_specs=[rows(w_sb.shape[1]), rows(w_rw.shape[1]), rows(w_lo.shape[1])],
        out_shape=[jax.ShapeDtypeStruct((n, w.shape[1]), F32) for w in (w_sb, w_rw, w_lo)],
        compiler_params=pltpu.CompilerParams(dimension_semantics=("arbitrary",),
                                             vmem_limit_bytes=VMEM_LIMIT),
        name="inproj",
    )(x2, gain, w_sb, w_rw, w_lo)


def _pair_rms_norm(x, gain2, ones2):
    ms = _mm_exact_rhs(x * x, ones2) * (1.0 / HEAD_DIM)
    return x * lax.rsqrt(ms + NORM_EPS) * gain2


def _sbattn_kernel(q_ref, k_ref, v_ref, gq_ref, gk_ref, o_ref, kn_ref, vb_ref, *, blk):
    qb = pl.program_id(2)
    ones2 = _head_ones(LANES)

    @pl.when(qb == 0)
    def _():
        kn_ref[...] = _pair_rms_norm(k_ref[...], gk_ref[...], ones2).astype(BF16)
        vb_ref[...] = v_ref[...].astype(BF16)

    lane = lax.broadcasted_iota(jnp.int32, (1, LANES), 1)
    first = lane < HEAD_DIM
    qn = _pair_rms_norm(q_ref[...], gq_ref[...], ones2) * (HEAD_DIM ** -0.5)
    qq = jnp.concatenate([jnp.where(first, qn, 0.0), jnp.where(first, 0.0, qn)], axis=0).astype(BF16)

    jj = lax.broadcasted_iota(jnp.int32, (blk, 2 * blk), 0)
    ss = lax.broadcasted_iota(jnp.int32, (blk, 2 * blk), 1)
    uu = jnp.where((jj > ss) | (ss >= blk), 1.0, 0.0).astype(BF16)

    tt = lax.broadcasted_iota(jnp.int32, (2 * blk, blk), 0) % blk
    sk = lax.broadcasted_iota(jnp.int32, (2 * blk, blk), 1)
    causal = sk < tt

    def step(j, carry, diagonal):
        c, acc = carry
        start = pl.multiple_of(j * blk, blk)
        kb = kn_ref[pl.ds(start, blk), :]
        vb = vb_ref[pl.ds(start, blk), :]
        z = _nt(qq, kb)
        lk = -_softplus(z)
        if diagonal:
            lk = jnp.where(causal, lk, 0.0)
        sums = _mm_exact_rhs(lk, uu)
        w = jnp.exp(z + lk + sums[:, :blk] + c)
        if diagonal:
            w = jnp.where(causal, w, 0.0)
        acc = acc + _mm(w.astype(BF16), vb)
        return c + sums[:, blk:], acc

    zero = jnp.zeros((2 * blk, LANES), F32)
    carry = step(qb, (zero, zero), True)
    _, acc = lax.fori_loop(0, qb, lambda i, cr: step(qb - 1 - i, cr, False), carry)
    o_ref[...] = jnp.where(first, acc[:blk], acc[blk:])


def _sbattn(sb3, gq2, gk2, blk):
    b, s, w3 = sb3.shape
    width = w3 // 3
    pairs = width // LANES
    qspec = pl.BlockSpec((None, blk, LANES), lambda bi, hp, qb: (bi, qb, hp))
    kspec = pl.BlockSpec((None, s, LANES), lambda bi, hp, qb: (bi, 0, pairs + hp))
    vspec = pl.BlockSpec((None, s, LANES), lambda bi, hp, qb: (bi, 0, 2 * pairs + hp))
    gspec = pl.BlockSpec((1, LANES), lambda bi, hp, qb: (0, 0))
    return pl.pallas_call(
        functools.partial(_sbattn_kernel, blk=blk),
        grid=(b, pairs, s // blk),
        in_specs=[qspec, kspec, vspec, gspec, gspec],
        out_specs=pl.BlockSpec((None, blk, LANES), lambda bi, hp, qb: (bi, qb, hp)),
        out_shape=jax.ShapeDtypeStruct((b, s, width), F32),
        scratch_shapes=[pltpu.VMEM((s, LANES), BF16), pltpu.VMEM((s, LANES), BF16)],
        compiler_params=pltpu.CompilerParams(dimension_semantics=("arbitrary",) * 3,
                                             vmem_limit_bytes=VMEM_LIMIT),
        name="sbattn",
    )(sb3, sb3, sb3, gq2, gk2)


def _token_shift(p, prev_ref, mu, first_tile):
    rows = lax.broadcasted_iota(jnp.int32, (p.shape[0], 1), 0)
    last_prev = jnp.where(first_tile, 0.0, prev_ref[7:8, :])
    prev = jnp.where(rows == 0, last_prev, pltpu.roll(p, 1, axis=0))
    return p + (prev - p) * mu


def _rwprep_kernel(rw_ref, rwp_ref, lo_ref, lop_ref, mur_ref, mul_ref, w0_ref, a0_ref, kk_ref, ka_ref,
                   wup_ref, aup_ref, gup_ref,
                   r_ref, lw_ref, k_ref, v_ref, a_ref, b_ref, g_ref, *, width):
    first_tile = pl.program_id(1) == 0
    pf = _token_shift(rw_ref[...], rwp_ref, mur_ref[...], first_tile)
    lf = _token_shift(lo_ref[...], lop_ref, mul_ref[...], first_tile)
    r = pf[:, :width]
    k = pf[:, width:2 * width]
    v = pf[:, 2 * width:]
    w = -_softplus(-(w0_ref[...] + _mm(jnp.tanh(lf).astype(BF16), wup_ref[...]))) - 0.5
    lr = _sigmoid(a0_ref[...] + _mm(lf.astype(BF16), aup_ref[...]))
    kk = k * kk_ref[...]
    ss = _mm_exact_rhs(kk * kk, _head_ones(width))
    kk = kk / jnp.maximum(jnp.sqrt(ss), 1e-12)
    r_ref[...] = r
    lw_ref[...] = -jnp.exp(w)
    k_ref[...] = k * (1.0 + (lr - 1.0) * ka_ref[...])
    v_ref[...] = v
    a_ref[...] = -kk
    b_ref[...] = kk * lr
    g_ref[...] = _mm(_sigmoid(lf).astype(BF16), gup_ref[...])


def _rwprep(rw3, lo3, mu_rw, mu_lo, w0, a0, k_k, k_a, wup, aup, gup, tm):
    b, s, w3 = rw3.shape
    width = w3 // 3
    nlo = lo3.shape[2]
    cur = lambda c: pl.BlockSpec((None, tm, c), lambda bi, i: (bi, i, 0))
    prev = lambda c: pl.BlockSpec((None, 8, c), lambda bi, i: (bi, jnp.maximum(i * (tm // 8) - 1, 0), 0))
    full = lambda a: pl.BlockSpec(a.shape, lambda bi, i: (0,) * a.ndim)
    return pl.pallas_call(
        functools.partial(_rwprep_kernel, width=width),
        grid=(b, s // tm),
        in_specs=[cur(w3), prev(w3), cur(nlo), prev(nlo), full(mu_rw), full(mu_lo), full(w0), full(a0),
                  full(k_k), full(k_a), full(wup), full(aup), full(gup)],
        out_specs=[cur(width)] * 7,
        out_shape=[jax.ShapeDtypeStruct((b, s, width), F32)] * 7,
        compiler_params=pltpu.CompilerParams(dimension_semantics=("arbitrary",) * 2,
                                             vmem_limit_bytes=VMEM_LIMIT),
        name="rwprep",
    )(rw3, rw3, lo3, lo3, mu_rw, mu_lo, w0, a0, k_k, k_a, wup, aup, gup)


def _rwscan_kernel(r_ref, lw_ref, k_ref, v_ref, a_ref, b_ref, g_ref, rk_ref, lnw_ref, lnb_ref,
                   o_ref, st_ref, y_ref, *, tb, width):
    t = RW_CHUNK
    m = MXU_DIM
    groups = width // m

    @pl.when(pl.program_id(1) == 0)
    def _():
        st_ref[...] = jnp.zeros_like(st_ref)

    ri = lax.broadcasted_iota(jnp.int32, (m, m), 0)
    ci = lax.broadcasted_iota(jnp.int32, (m, m), 1)
    same = (ri // HEAD_DIM) == (ci // HEAD_DIM)
    bdmask = jnp.where(same, 1.0, 0.0)
    strict = same & ((ci % HEAD_DIM) < (ri % HEAD_DIM))
    incl = same & ((ci % HEAD_DIM) <= (ri % HEAD_DIM))
    eye = jnp.where(ri == ci, 1.0, 0.0)
    ti = lax.broadcasted_iota(jnp.int32, (t, t), 0)
    si = lax.broadcasted_iota(jnp.int32, (t, t), 1)
    tri = jnp.where(si <= ti, 1.0, 0.0).astype(BF16)

    def blockdiag(x):
        return (jnp.concatenate([x] * HEADS_PER_MXU, axis=0) * bdmask).astype(BF16)

    for grp in range(groups):
        cols = slice(grp * m, (grp + 1) * m)
        for ch in range(tb // t):
            rows = slice(ch * t, (ch + 1) * t)
            r = r_ref[rows, cols]
            lw = lw_ref[rows, cols]
            k = k_ref[rows, cols]
            v = v_ref[rows, cols]
            a = a_ref[rows, cols]
            b = b_ref[rows, cols]

            l1, l2, l3 = _split3(lw)
            cum = _mm(tri, l1) + _mm(tri, l2) + _mm(tri, l3)
            cum_end = cum[t - 1:t, :]
            p_inv = jnp.exp(-cum)
            p_rest = jnp.exp(cum_end - cum)
            r_bd = blockdiag(r * jnp.exp(cum))
            a_bd = blockdiag(a * jnp.exp(cum - lw))
            k_bd = blockdiag(k * p_inv)
            b_bd = blockdiag(b * p_inv)
            kd_bd = blockdiag(k * p_rest)
            bd_bd = blockdiag(b * p_rest)
            v_bd = blockdiag(v)

            aab = jnp.where(strict, _nt(a_bd, b_bd), 0.0)
            aak = jnp.where(strict, _nt(a_bd, k_bd), 0.0).astype(BF16)
            brb = jnp.where(incl, _nt(r_bd, b_bd), 0.0).astype(BF16)
            brk = jnp.where(incl, _nt(r_bd, k_bd), 0.0).astype(BF16)

            tinv = eye + aab
            apow = aab.astype(BF16)
            for _ in range((t - 1).bit_length() - 1):
                sq = _mm(apow, apow)
                apow = sq.astype(BF16)
                tinv = tinv + _mm(tinv.astype(BF16), apow)
            tinv = tinv.astype(BF16)

            w_mat = _mm(tinv, a_bd).astype(BF16)
            u0 = _mm(tinv, _mm(aak, v_bd).astype(BF16)).astype(BF16)
            q_mat = (r_bd.astype(F32) + _mm(brb, w_mat)).astype(BF16)
            y0 = _mm(brb, u0) + _mm(brk, v_bd)
            m_mat = (eye * jnp.exp(cum_end) + _tn(bd_bd, w_mat)).astype(BF16)
            c0 = _tn(bd_bd, u0) + _tn(kd_bd, v_bd)

            s0 = st_ref[grp]
            s_hi, s_lo = _split2(s0)
            y = _mm(q_mat, s_hi) + _mm(q_mat, s_lo) + y0
            st_ref[grp] = _mm(m_mat, s_hi) + _mm(m_mat, s_lo) + c0
            y_nat = y[0:t]
            for h in range(1, HEADS_PER_MXU):
                y_nat = y_nat + y[h * t:(h + 1) * t]
            y_ref[rows, cols] = y_nat

    ones = _head_ones(width)
    inv = 1.0 / HEAD_DIM
    y = y_ref[...]
    mean = _mm_exact_rhs(y, ones) * inv
    d = y - mean
    var = _mm_exact_rhs(d * d, ones) * inv
    yn = d * lax.rsqrt(var + LN_X_EPS) * lnw_ref[...] + lnb_ref[...]
    bonus = _mm_exact_rhs(r_ref[...] * k_ref[...] * rk_ref[...], ones) * v_ref[...]
    o_ref[...] = (yn + bonus) * g_ref[...]


def _rwscan(r, lw, k, v, a, b_, g, r_k, ln_w, ln_b, tb):
    bsz, s, width = r.shape
    cur = pl.BlockSpec((None, tb, width), lambda bi, i: (bi, i, 0))
    full = lambda x: pl.BlockSpec(x.shape, lambda bi, i: (0,) * x.ndim)
    return pl.pallas_call(
        functools.partial(_rwscan_kernel, tb=tb, width=width),
        grid=(bsz, s // tb),
        in_specs=[cur] * 7 + [full(r_k), full(ln_w), full(ln_b)],
        out_specs=cur,
        out_shape=jax.ShapeDtypeStruct((bsz, s, width), F32),
        scratch_shapes=[pltpu.VMEM((width // MXU_DIM, MXU_DIM, MXU_DIM), F32),
                        pltpu.VMEM((tb, width), F32)],
        compiler_params=pltpu.CompilerParams(dimension_semantics=("arbitrary",) * 2,
                                             vmem_limit_bytes=VMEM_LIMIT),
        name="rwscan",
    )(r, lw, k, v, a, b_, g, r_k, ln_w, ln_b)


def _outproj_kernel(sb_ref, rw_ref, x_ref, wsb_ref, wrw_ref, g_ref, wr_ref,
                    x1_ref, h2_ref, meta_ref, cnt_ref, run_ref, *, tm):
    @pl.when(pl.program_id(0) == 0)
    def _():
        run_ref[...] = jnp.zeros_like(run_ref)

    x1 = x_ref[...] + _mm(sb_ref[...].astype(BF16), wsb_ref[...]) + _mm(rw_ref[...].astype(BF16), wrw_ref[...])
    x1_ref[...] = x1
    ms = jnp.mean(x1 * x1, axis=-1, keepdims=True)
    h2 = x1 * lax.rsqrt(ms + NORM_EPS) * g_ref[...]
    h2_ref[...] = h2

    h_hi, h_lo = _split2(h2)
    w_hi, w_lo = _split2(wr_ref[...])
    lg = _mm(h_hi, w_hi) + _mm(h_hi, w_lo) + _mm(h_lo, w_hi)

    lane = lax.broadcasted_iota(jnp.int32, (tm, LANES), 1).astype(F32)
    neg = -jnp.inf
    big = float(LANES)
    is_group = lane < N_GROUPS
    gl = jnp.where(is_group, lg, neg)
    gmax = jnp.max(gl, axis=1, keepdims=True)
    gidx = jnp.min(jnp.where(gl == gmax, lane, big), axis=1, keepdims=True)
    group_gate = 1.0 / jnp.sum(jnp.where(is_group, jnp.exp(lg - gmax), 0.0), axis=1, keepdims=True)
    lo_lane = N_GROUPS + EXPERTS_PER_GROUP * gidx
    el = jnp.where((lane >= lo_lane) & (lane < lo_lane + EXPERTS_PER_GROUP), lg, neg)
    m1 = jnp.max(el, axis=1, keepdims=True)
    i1 = jnp.min(jnp.where(el == m1, lane, big), axis=1, keepdims=True)
    el2 = jnp.where(lane == i1, neg, el)
    m2 = jnp.max(el2, axis=1, keepdims=True)
    i2 = jnp.min(jnp.where(el2 == m2, lane, big), axis=1, keepdims=True)
    p2 = jnp.exp(m2 - m1)
    gate1 = group_gate / (1.0 + p2)
    gate2 = group_gate * p2 / (1.0 + p2)

    hit1 = lane == i1
    hit2 = lane == i2
    onehot = jnp.where(hit1 | hit2, 1.0, 0.0)
    rr = lax.broadcasted_iota(jnp.int32, (tm, tm), 0)
    cc = lax.broadcasted_iota(jnp.int32, (tm, tm), 1)
    below = jnp.where(cc < rr, 1.0, 0.0).astype(BF16)
    before = run_ref[...] + _mm(below, onehot.astype(BF16))
    rank1 = jnp.sum(jnp.where(hit1, before, 0.0), axis=1, keepdims=True)
    rank2 = jnp.sum(jnp.where(hit2, before, 0.0), axis=1, keepdims=True)
    run = run_ref[...] + jnp.sum(onehot, axis=0, keepdims=True)
    run_ref[...] = run
    cnt_ref[...] = run

    vals = (i1 - N_GROUPS, i2 - N_GROUPS, gate1, gate2, rank1, rank2)
    meta = jnp.zeros((tm, LANES), F32)
    for pos, val in enumerate(vals):
        meta = jnp.where(lane == pos, val.astype(F32), meta)
    meta_ref[...] = meta


def _outproj(sb_out, rw_out, x2, w_sb, w_rw, gain, w_router, tm):
    n, d = x2.shape
    rows = lambda c: pl.BlockSpec((tm, c), lambda i: (i, 0))
    full = lambda a: pl.BlockSpec(a.shape, lambda i: (0,) * a.ndim)
    return pl.pallas_call(
        functools.partial(_outproj_kernel, tm=tm),
        grid=(n // tm,),
        in_specs=[rows(sb_out.shape[1]), rows(rw_out.shape[1]), rows(d), full(w_sb), full(w_rw), full(gain),
                  full(w_router)],
        out_specs=[rows(d), rows(d), rows(LANES), pl.BlockSpec((1, LANES), lambda i: (0, 0))],
        out_shape=[jax.ShapeDtypeStruct((n, d), F32), jax.ShapeDtypeStruct((n, d), F32),
                   jax.ShapeDtypeStruct((n, LANES), F32), jax.ShapeDtypeStruct((1, LANES), F32)],
        scratch_shapes=[pltpu.VMEM((1, LANES), F32)],
        compiler_params=pltpu.CompilerParams(dimension_semantics=("arbitrary",),
                                             vmem_limit_bytes=VMEM_LIMIT),
        name="outproj",
    )(sb_out, rw_out, x2, w_sb, w_rw, gain, w_router)


def _row_copy(src_ref, src_row, dst_ref, dst_row, sem):
    return pltpu.make_async_copy(src_ref.at[pl.ds(src_row, 1)], dst_ref.at[pl.ds(dst_row, 1)], sem)


def _dispatch_kernel(dest_ref, h_ref, buf_in_ref, buf_ref, sem, *, tm):
    del buf_in_ref
    base = pl.program_id(0) * tm

    def issue(t, _):
        for kk in range(TOP_K):
            _row_copy(h_ref, base + t, buf_ref, dest_ref[0, TOP_K * t + kk], sem).start()
        return 0

    lax.fori_loop(0, tm, issue, 0)

    def drain(t, _):
        for kk in range(TOP_K):
            _row_copy(h_ref, 0, buf_ref, 0, sem).wait()
        return 0

    lax.fori_loop(0, tm, drain, 0)


def _dispatch(dest3, h2, buf0, tm):
    n = h2.shape[0]
    return pl.pallas_call(
        functools.partial(_dispatch_kernel, tm=tm),
        grid=(n // tm,),
        in_specs=[pl.BlockSpec((None, 1, TOP_K * tm), lambda i: (i, 0, 0), memory_space=pltpu.SMEM),
                  pl.BlockSpec(memory_space=pl.ANY), pl.BlockSpec(memory_space=pl.ANY)],
        out_specs=pl.BlockSpec(memory_space=pl.ANY),
        out_shape=jax.ShapeDtypeStruct(buf0.shape, buf0.dtype),
        scratch_shapes=[pltpu.SemaphoreType.DMA(())],
        input_output_aliases={2: 0},
        compiler_params=pltpu.CompilerParams(dimension_semantics=("arbitrary",)),
        name="dispatch",
    )(dest3, h2, buf0)


def _experts_kernel(eid_ref, used_ref, x_ref, wg_ref, wu_ref, wd_ref, o_ref, wgb_ref, wub_ref, wdb_ref):
    i = pl.program_id(0)
    prev = eid_ref[jnp.maximum(i - 1, 0)]

    @pl.when((i == 0) | (eid_ref[i] != prev))
    def _():
        wgb_ref[...] = wg_ref[...].astype(BF16)
        wub_ref[...] = wu_ref[...].astype(BF16)
        wdb_ref[...] = wd_ref[...].astype(BF16)

    @pl.when(i < used_ref[0])
    def _():
        xb = x_ref[...].astype(BF16)
        hg = _mm(xb, wgb_ref[...])
        hu = _mm(xb, wub_ref[...])
        act = hg * _sigmoid(hg) * hu
        o_ref[...] = _mm(act.astype(BF16), wdb_ref[...])

    @pl.when(i >= used_ref[0])
    def _():
        o_ref[...] = jnp.zeros_like(o_ref)


def _experts(blk_eid, n_used, buf, w_g, w_u, w_d):
    n_rows, d = buf.shape
    de = w_g.shape[2]
    n_blocks = n_rows // EXPERT_ROWS
    row_map = lambda i, eid, used: (jnp.minimum(i, used[0] - 1), 0)
    w_map = lambda i, eid, used: (eid[i], 0, 0)
    grid_spec = pltpu.PrefetchScalarGridSpec(
        num_scalar_prefetch=2,
        grid=(n_blocks,),
        in_specs=[pl.BlockSpec((EXPERT_ROWS, d), row_map),
                  pl.BlockSpec((None, d, de), w_map), pl.BlockSpec((None, d, de), w_map),
                  pl.BlockSpec((None, de, d), w_map)],
        out_specs=pl.BlockSpec((EXPERT_ROWS, d), lambda i, eid, used: (i, 0)),
        scratch_shapes=[pltpu.VMEM((d, de), BF16), pltpu.VMEM((d, de), BF16), pltpu.VMEM((de, d), BF16)],
    )
    return pl.pallas_call(
        _experts_kernel,
        grid_spec=grid_spec,
        out_shape=jax.ShapeDtypeStruct((n_rows, d), F32),
        compiler_params=pltpu.CompilerParams(dimension_semantics=("arbitrary",),
                                             vmem_limit_bytes=VMEM_LIMIT),
        name="experts",
    )(blk_eid, n_used, buf, w_g, w_u, w_d)


def _combine_kernel(dest_ref, rows_ref, x1_ref, meta_ref, o_ref, got_ref, sem, *, tm):
    def issue(t, _):
        for kk in range(TOP_K):
            _row_copy(rows_ref, dest_ref[0, TOP_K * t + kk], got_ref.at[kk], t, sem).start()
        return 0

    lax.fori_loop(0, tm, issue, 0)

    def drain(t, _):
        for kk in range(TOP_K):
            _row_copy(rows_ref, 0, got_ref.at[kk], 0, sem).wait()
        return 0

    lax.fori_loop(0, tm, drain, 0)
    meta = meta_ref[...]
    o_ref[...] = x1_ref[...] + got_ref[0] * meta[:, 2:3] + got_ref[1] * meta[:, 3:4]


def _combine(dest3, exp_out, x1, meta, tm):
    n, d = x1.shape
    rows = lambda c: pl.BlockSpec((tm, c), lambda i: (i, 0))
    return pl.pallas_call(
        functools.partial(_combine_kernel, tm=tm),
        grid=(n // tm,),
        in_specs=[pl.BlockSpec((None, 1, TOP_K * tm), lambda i: (i, 0, 0), memory_space=pltpu.SMEM),
                  pl.BlockSpec(memory_space=pl.ANY), rows(d), rows(LANES)],
        out_specs=rows(d),
        out_shape=jax.ShapeDtypeStruct((n, d), F32),
        scratch_shapes=[pltpu.VMEM((TOP_K, tm, d), F32), pltpu.SemaphoreType.DMA(())],
        compiler_params=pltpu.CompilerParams(dimension_semantics=("arbitrary",),
                                             vmem_limit_bytes=VMEM_LIMIT),
        name="combine",
    )(dest3, exp_out, x1, meta)


def _layer(x, norm1_gain, w_in, sb_q_gain, sb_k_gain, rw_shift_mu, rw_w0, rw_w_up, rw_a0, rw_a_up, rw_g_up,
           rw_k_k, rw_k_a, rw_r_k, rw_ln_w, rw_ln_b, w_out, norm2_gain, w_router_group, w_router_expert,
           w_exp_gate, w_exp_up, w_exp_down):
    bsz, s, d = x.shape
    n = bsz * s
    sbw = d // 2
    rww = d - sbw
    n_decay, n_aaa = rw_w_up.shape[0], rw_a_up.shape[0]
    n_lora = n_decay + n_aaa + rw_g_up.shape[0]
    tm = min(256, n)
    row = lambda a: a.reshape(1, -1).astype(F32)

    x2 = x.reshape(n, d)
    w_bf = w_in.astype(BF16)
    sb, rw, lo = _inproj(x2, row(norm1_gain), w_bf[:, :3 * sbw], w_bf[:, 3 * sbw:3 * sbw + 3 * rww],
                         w_bf[:, 3 * sbw + 3 * rww:], tm)
    pair = lambda gvec: jnp.tile(row(gvec), (1, LANES // HEAD_DIM))
    sb_out = _sbattn(sb.reshape(bsz, s, 3 * sbw), pair(sb_q_gain), pair(sb_k_gain), min(128, s))

    pad_rows = lambda w_up, start: jnp.zeros((n_lora, rww), BF16).at[start:start + w_up.shape[0]].set(
        w_up.astype(BF16))
    mu = row(rw_shift_mu)
    prep = _rwprep(rw.reshape(bsz, s, 3 * rww), lo.reshape(bsz, s, n_lora), mu[:, :3 * rww], mu[:, 3 * rww:],
                   row(rw_w0), row(rw_a0), row(rw_k_k), row(rw_k_a),
                   pad_rows(rw_w_up, 0), pad_rows(rw_a_up, n_decay), pad_rows(rw_g_up, n_decay + n_aaa),
                   min(256, s))
    rw_out = _rwscan(*prep, row(rw_r_k), row(rw_ln_w), row(rw_ln_b), min(256, s))

    w_out_bf = w_out.astype(BF16)
    w_router = jnp.zeros((d, LANES), F32).at[:, :N_GROUPS].set(w_router_group).at[
        :, N_GROUPS:N_GROUPS + N_EXPERTS].set(w_router_expert)
    x1, h2, meta, counts = _outproj(sb_out.reshape(n, sbw), rw_out.reshape(n, rww), x2, w_out_bf[:sbw],
                                    w_out_bf[sbw:], row(norm2_gain), w_router, tm)

    counts = counts[0, N_GROUPS:N_GROUPS + N_EXPERTS].astype(jnp.int32)
    padded = (counts + EXPERT_ROWS - 1) // EXPERT_ROWS * EXPERT_ROWS
    pad_end = jnp.cumsum(padded)
    pad_start = pad_end - padded
    eid = meta[:, 0:TOP_K].astype(jnp.int32)
    rank = meta[:, 4:4 + TOP_K].astype(jnp.int32)
    dest3 = (pad_start[eid] + rank).reshape(n // tm, 1, TOP_K * tm)
    n_rows = n * TOP_K + N_EXPERTS * EXPERT_ROWS
    n_blocks = n_rows // EXPERT_ROWS
    block_start = jnp.arange(n_blocks, dtype=jnp.int32) * EXPERT_ROWS
    blk_eid = jnp.minimum(jnp.sum((pad_end[None, :] <= block_start[:, None]).astype(jnp.int32), axis=1),
                          N_EXPERTS - 1)
    n_used = (pad_end[-1:] // EXPERT_ROWS).astype(jnp.int32)

    buf = _dispatch(dest3, h2, jnp.zeros((n_rows, d), F32), tm)
    exp_out = _experts(blk_eid, n_used, buf, w_exp_gate, w_exp_up, w_exp_down)
    return _combine(dest3, exp_out, x1, meta, tm).reshape(bsz, s, d)


def kernel(x, norm1_gain, w_in, sb_q_gain, sb_k_gain, rw_shift_mu, rw_w0, rw_w_up, rw_a0, rw_a_up, rw_g_up,
           rw_k_k, rw_k_a, rw_r_k, rw_ln_w, rw_ln_b, w_out, norm2_gain, w_router_group, w_router_expert,
           w_exp_gate, w_exp_up, w_exp_down):
    params = (norm1_gain, w_in, sb_q_gain, sb_k_gain, rw_shift_mu, rw_w0, rw_w_up, rw_a0, rw_a_up, rw_g_up,
              rw_k_k, rw_k_a, rw_r_k, rw_ln_w, rw_ln_b, w_out, norm2_gain, w_router_group, w_router_expert,
              w_exp_gate, w_exp_up, w_exp_down)
    for layer in range(norm1_gain.shape[0]):
        x = _layer(x, *(p[layer] for p in params))
    return x
```

```python
import functools

import jax
import jax.numpy as jnp
from jax import lax
from jax.experimental import pallas as pl
from jax.experimental.pallas import tpu as pltpu

F32 = jnp.float32
BF16 = jnp.bfloat16

HEAD_DIM = 64
NORM_EPS = 1e-6
LN_X_EPS = 64e-5
N_GROUPS = 4
EXPERTS_PER_GROUP = 8
N_EXPERTS = N_GROUPS * EXPERTS_PER_GROUP
TOP_K = 2

LANES = 128
MXU_DIM = 256
HEADS_PER_MXU = MXU_DIM // HEAD_DIM
RW_CHUNK = 64
EXPERT_ROWS = 256
VMEM_LIMIT = 56 * 1024 * 1024


def _nt(a, b):
    return lax.dot_general(a, b, (((1,), (1,)), ((), ())), preferred_element_type=F32)


def _tn(a, b):
    return lax.dot_general(a, b, (((0,), (0,)), ((), ())), preferred_element_type=F32)


def _mm(a, b):
    return jnp.dot(a, b, preferred_element_type=F32)


def _split2(x):
    hi = x.astype(BF16)
    lo = (x - hi.astype(F32)).astype(BF16)
    return hi, lo


def _split3(x):
    h1 = x.astype(BF16)
    r1 = x - h1.astype(F32)
    h2 = r1.astype(BF16)
    h3 = (r1 - h2.astype(F32)).astype(BF16)
    return h1, h2, h3


def _mm_exact_rhs(x, m):
    hi, lo = _split2(x)
    return _mm(hi, m) + _mm(lo, m)


def _head_ones(width):
    r = lax.broadcasted_iota(jnp.int32, (width, width), 0) // HEAD_DIM
    c = lax.broadcasted_iota(jnp.int32, (width, width), 1) // HEAD_DIM
    return jnp.where(r == c, 1.0, 0.0).astype(BF16)


def _softplus(z):
    return jnp.maximum(z, 0.0) + jnp.log(1.0 + jnp.exp(-jnp.abs(z)))


def _sigmoid(z):
    return 1.0 / (1.0 + jnp.exp(-z))


def _inproj_kernel(x_ref, g_ref, wsb_ref, wrw_ref, wlo_ref, sb_ref, rw_ref, lo_ref):
    x = x_ref[...]
    ms = jnp.mean(x * x, axis=-1, keepdims=True)
    h = (x * lax.rsqrt(ms + NORM_EPS) * g_ref[...]).astype(BF16)
    sb_ref[...] = _mm(h, wsb_ref[...])
    rw_ref[...] = _mm(h, wrw_ref[...])
    lo_ref[...] = _mm(h, wlo_ref[...])


def _inproj(x2, gain, w_sb, w_rw, w_lo, tm):
    n, d = x2.shape
    full = lambda a: pl.BlockSpec(a.shape, lambda i: (0,) * a.ndim)
    rows = lambda c: pl.BlockSpec((tm, c), lambda i: (i, 0))
    return pl.pallas_call(
        _inproj_kernel,
        grid=(n // tm,),
        in_specs=[rows(d), full(gain), full(w_sb), full(w_rw), full(w_lo)],
        out_specs=[rows(w_sb.shape[1]), rows(w_rw.shape[1]), rows(w_lo.shape[1])],
        out_shape=[jax.ShapeDtypeStruct((n, w.shape[1]), F32) for w in (w_sb, w_rw, w_lo)],
        compiler_params=pltpu.CompilerParams(dimension_semantics=("arbitrary",),
                                             vmem_limit_bytes=VMEM_LIMIT),
        name="inproj",
    )(x2, gain, w_sb, w_rw, w_lo)


def _head_rms_norm(x, gain, ones):
    ms = _mm_exact_rhs(x * x, ones) * (1.0 / HEAD_DIM)
    return x * lax.rsqrt(ms + NORM_EPS) * gain


def _sbattn_kernel(q_ref, k_ref, v_ref, gq_ref, gk_ref, o_ref, kn_ref, vb_ref, *, blk, pairs):
    qb = pl.program_id(1)
    ones = _head_ones(pairs * LANES)

    @pl.when(qb == 0)
    def _():
        kn_ref[...] = _head_rms_norm(k_ref[...], gk_ref[...], ones).astype(BF16)
        vb_ref[...] = v_ref[...].astype(BF16)

    lane = lax.broadcasted_iota(jnp.int32, (1, pairs * LANES), 1)
    first = (lane % LANES) < HEAD_DIM
    qn = _head_rms_norm(q_ref[...], gq_ref[...], ones) * (HEAD_DIM ** -0.5)
    q_first = jnp.where(first, qn, 0.0).astype(BF16)
    q_second = jnp.where(first, 0.0, qn).astype(BF16)
    lanes_of = lambda p: slice(p * LANES, (p + 1) * LANES)
    qq = [jnp.concatenate([q_first[:, lanes_of(p)], q_second[:, lanes_of(p)]], axis=0) for p in range(pairs)]

    jj = lax.broadcasted_iota(jnp.int32, (blk, blk + LANES), 0)
    ss = lax.broadcasted_iota(jnp.int32, (blk, blk + LANES), 1)
    uu = jnp.where((jj > ss) | (ss >= blk), 1.0, 0.0).astype(BF16)

    rows = 2 * blk
    tt = lax.broadcasted_iota(jnp.int32, (pairs * rows, blk), 0) % blk
    sk = lax.broadcasted_iota(jnp.int32, (pairs * rows, blk), 1)
    causal = sk < tt

    def step(j, carry, diagonal):
        c, acc = carry
        start = pl.multiple_of(j * blk, blk)
        z = jnp.concatenate([_nt(qq[p], kn_ref[pl.ds(start, blk), lanes_of(p)]) for p in range(pairs)], axis=0)
        lk = -_softplus(z)
        if diagonal:
            lk = jnp.where(causal, lk, 0.0)
        sums = _mm_exact_rhs(lk, uu)
        later = jnp.concatenate([c] * (blk // LANES), axis=1)
        w = jnp.exp(z + lk + sums[:, :blk] + later)
        if diagonal:
            w = jnp.where(causal, w, 0.0)
        w = w.astype(BF16)
        pv = jnp.concatenate([_mm(w[p * rows:(p + 1) * rows], vb_ref[pl.ds(start, blk), lanes_of(p)])
                              for p in range(pairs)], axis=0)
        return c + sums[:, blk:], acc + pv

    zero = jnp.zeros((pairs * rows, LANES), F32)
    carry = step(qb, (zero, zero), True)
    _, acc = lax.fori_loop(0, qb, lambda i, cr: step(qb - 1 - i, cr, False), carry)
    first_pair = first[:, :LANES]
    for p in range(pairs):
        o_ref[:, lanes_of(p)] = jnp.where(first_pair, acc[p * rows:p * rows + blk], acc[p * rows + blk:(p + 1) * rows])


def _sbattn(sb3, gq, gk, blk):
    b, s, w3 = sb3.shape
    width = w3 // 3
    gspec = pl.BlockSpec((1, width), lambda bi, qb: (0, 0))
    return pl.pallas_call(
        functools.partial(_sbattn_kernel, blk=blk, pairs=width // LANES),
        grid=(b, s // blk),
        in_specs=[pl.BlockSpec((None, blk, width), lambda bi, qb: (bi, qb, 0)),
                  pl.BlockSpec((None, s, width), lambda bi, qb: (bi, 0, 1)),
                  pl.BlockSpec((None, s, width), lambda bi, qb: (bi, 0, 2)), gspec, gspec],
        out_specs=pl.BlockSpec((None, blk, width), lambda bi, qb: (bi, qb, 0)),
        out_shape=jax.ShapeDtypeStruct((b, s, width), F32),
        scratch_shapes=[pltpu.VMEM((s, width), BF16), pltpu.VMEM((s, width), BF16)],
        compiler_params=pltpu.CompilerParams(dimension_semantics=("arbitrary",) * 2,
                                             vmem_limit_bytes=VMEM_LIMIT),
        name="sbattn",
    )(sb3, sb3, sb3, gq, gk)


def _token_shift(p, prev_ref, mu, first_tile):
    rows = lax.broadcasted_iota(jnp.int32, (p.shape[0], 1), 0)
    last_prev = jnp.where(first_tile, 0.0, prev_ref[7:8, :])
    prev = jnp.where(rows == 0, last_prev, pltpu.roll(p, 1, axis=0))
    return p + (prev - p) * mu


def _rwprep_kernel(rw_ref, rwp_ref, lo_ref, lop_ref, mur_ref, mul_ref, w0_ref, a0_ref, kk_ref, ka_ref,
                   wup_ref, aup_ref, gup_ref,
                   r_ref, lw_ref, k_ref, v_ref, a_ref, b_ref, g_ref, *, width):
    first_tile = pl.program_id(1) == 0
    pf = _token_shift(rw_ref[...], rwp_ref, mur_ref[...], first_tile)
    lf = _token_shift(lo_ref[...], lop_ref, mul_ref[...], first_tile)
    r = pf[:, :width]
    k = pf[:, width:2 * width]
    v = pf[:, 2 * width:]
    w = -_softplus(-(w0_ref[...] + _mm(jnp.tanh(lf).astype(BF16), wup_ref[...]))) - 0.5
    lr = _sigmoid(a0_ref[...] + _mm(lf.astype(BF16), aup_ref[...]))
    kk = k * kk_ref[...]
    ss = _mm_exact_rhs(kk * kk, _head_ones(width))
    kk = kk / jnp.maximum(jnp.sqrt(ss), 1e-12)
    r_ref[...] = r
    lw_ref[...] = -jnp.exp(w)
    k_ref[...] = k * (1.0 + (lr - 1.0) * ka_ref[...])
    v_ref[...] = v
    a_ref[...] = -kk
    b_ref[...] = kk * lr
    g_ref[...] = _mm(_sigmoid(lf).astype(BF16), gup_ref[...])


def _rwprep(rw3, lo3, mu_rw, mu_lo, w0, a0, k_k, k_a, wup, aup, gup, tm):
    b, s, w3 = rw3.shape
    width = w3 // 3
    nlo = lo3.shape[2]
    cur = lambda c: pl.BlockSpec((None, tm, c), lambda bi, i: (bi, i, 0))
    prev = lambda c: pl.BlockSpec((None, 8, c), lambda bi, i: (bi, jnp.maximum(i * (tm // 8) - 1, 0), 0))
    full = lambda a: pl.BlockSpec(a.shape, lambda bi, i: (0,) * a.ndim)
    return pl.pallas_call(
        functools.partial(_rwprep_kernel, width=width),
        grid=(b, s // tm),
        in_specs=[cur(w3), prev(w3), cur(nlo), prev(nlo), full(mu_rw), full(mu_lo), full(w0), full(a0),
                  full(k_k), full(k_a), full(wup), full(aup), full(gup)],
        out_specs=[cur(width)] * 7,
        out_shape=[jax.ShapeDtypeStruct((b, s, width), F32)] * 7,
        compiler_params=pltpu.CompilerParams(dimension_semantics=("arbitrary",) * 2,
                                             vmem_limit_bytes=VMEM_LIMIT),
        name="rwprep",
    )(rw3, rw3, lo3, lo3, mu_rw, mu_lo, w0, a0, k_k, k_a, wup, aup, gup)


def _rwscan_kernel(r_ref, lw_ref, k_ref, v_ref, a_ref, b_ref, g_ref, rk_ref, lnw_ref, lnb_ref,
                   o_ref, st_ref, y_ref, *, tb, width):
    t = RW_CHUNK
    m = MXU_DIM
    groups = width // m

    @pl.when(pl.program_id(1) == 0)
    def _():
        st_ref[...] = jnp.zeros_like(st_ref)

    ri = lax.broadcasted_iota(jnp.int32, (m, m), 0)
    ci = lax.broadcasted_iota(jnp.int32, (m, m), 1)
    same = (ri // HEAD_DIM) == (ci // HEAD_DIM)
    bdmask = jnp.where(same, 1.0, 0.0)
    strict = same & ((ci % HEAD_DIM) < (ri % HEAD_DIM))
    incl = same & ((ci % HEAD_DIM) <= (ri % HEAD_DIM))
    eye = jnp.where(ri == ci, 1.0, 0.0)
    ti = lax.broadcasted_iota(jnp.int32, (t, t), 0)
    si = lax.broadcasted_iota(jnp.int32, (t, t), 1)
    tri = jnp.where(si <= ti, 1.0, 0.0).astype(BF16)

    def blockdiag(x):
        return (jnp.concatenate([x] * HEADS_PER_MXU, axis=0) * bdmask).astype(BF16)

    for grp in range(groups):
        cols = slice(grp * m, (grp + 1) * m)
        for ch in range(tb // t):
            rows = slice(ch * t, (ch + 1) * t)
            r = r_ref[rows, cols]
            lw = lw_ref[rows, cols]
            k = k_ref[rows, cols]
            v = v_ref[rows, cols]
            a = a_ref[rows, cols]
            b = b_ref[rows, cols]

            l1, l2, l3 = _split3(lw)
            cum = _mm(tri, l1) + _mm(tri, l2) + _mm(tri, l3)
            cum_end = cum[t - 1:t, :]
            p_inv = jnp.exp(-cum)
            p_rest = jnp.exp(cum_end - cum)
            r_bd = blockdiag(r * jnp.exp(cum))
            a_bd = blockdiag(a * jnp.exp(cum - lw))
            k_bd = blockdiag(k * p_inv)
            b_bd = blockdiag(b * p_inv)
            kd_bd = blockdiag(k * p_rest)
            bd_bd = blockdiag(b * p_rest)
            v_bd = blockdiag(v)

            aab = jnp.where(strict, _nt(a_bd, b_bd), 0.0)
            aak = jnp.where(strict, _nt(a_bd, k_bd), 0.0).astype(BF16)
            brb = jnp.where(incl, _nt(r_bd, b_bd), 0.0).astype(BF16)
            brk = jnp.where(incl, _nt(r_bd, k_bd), 0.0).astype(BF16)

            tinv = eye + aab
            apow = aab.astype(BF16)
            for _ in range((t - 1).bit_length() - 1):
                sq = _mm(apow, apow)
                apow = sq.astype(BF16)
                tinv = tinv + _mm(tinv.astype(BF16), apow)
            tinv = tinv.astype(BF16)

            w_mat = _mm(tinv, a_bd).astype(BF16)
            u0 = _mm(tinv, _mm(aak, v_bd).astype(BF16)).astype(BF16)
            q_mat = (r_bd.astype(F32) + _mm(brb, w_mat)).astype(BF16)
            y0 = _mm(brb, u0) + _mm(brk, v_bd)
            m_mat = (eye * jnp.exp(cum_end) + _tn(bd_bd, w_mat)).astype(BF16)
            c0 = _tn(bd_bd, u0) + _tn(kd_bd, v_bd)

            s0 = st_ref[grp]
            s_hi, s_lo = _split2(s0)
            y = _mm(q_mat, s_hi) + _mm(q_mat, s_lo) + y0
            st_ref[grp] = _mm(m_mat, s_hi) + _mm(m_mat, s_lo) + c0
            y_nat = y[0:t]
            for h in range(1, HEADS_PER_MXU):
                y_nat = y_nat + y[h * t:(h + 1) * t]
            y_ref[rows, cols] = y_nat

    ones = _head_ones(width)
    inv = 1.0 / HEAD_DIM
    y = y_ref[...]
    mean = _mm_exact_rhs(y, ones) * inv
    d = y - mean
    var = _mm_exact_rhs(d * d, ones) * inv
    yn = d * lax.rsqrt(var + LN_X_EPS) * lnw_ref[...] + lnb_ref[...]
    bonus = _mm_exact_rhs(r_ref[...] * k_ref[...] * rk_ref[...], ones) * v_ref[...]
    o_ref[...] = (yn + bonus) * g_ref[...]


def _rwscan(r, lw, k, v, a, b_, g, r_k, ln_w, ln_b, tb):
    bsz, s, width = r.shape
    cur = pl.BlockSpec((None, tb, width), lambda bi, i: (bi, i, 0))
    full = lambda x: pl.BlockSpec(x.shape, lambda bi, i: (0,) * x.ndim)
    return pl.pallas_call(
        functools.partial(_rwscan_kernel, tb=tb, width=width),
        grid=(bsz, s // tb),
        in_specs=[cur] * 7 + [full(r_k), full(ln_w), full(ln_b)],
        out_specs=cur,
        out_shape=jax.ShapeDtypeStruct((bsz, s, width), F32),
        scratch_shapes=[pltpu.VMEM((width // MXU_DIM, MXU_DIM, MXU_DIM), F32),
                        pltpu.VMEM((tb, width), F32)],
        compiler_params=pltpu.CompilerParams(dimension_semantics=("arbitrary",) * 2,
                                             vmem_limit_bytes=VMEM_LIMIT),
        name="rwscan",
    )(r, lw, k, v, a, b_, g, r_k, ln_w, ln_b)


def _outproj_kernel(sb_ref, rw_ref, x_ref, wsb_ref, wrw_ref, g_ref, wr_ref,
                    x1_ref, h2_ref, meta_ref, cnt_ref, run_ref, *, tm):
    @pl.when(pl.program_id(0) == 0)
    def _():
        run_ref[...] = jnp.zeros_like(run_ref)

    x1 = x_ref[...] + _mm(sb_ref[...].astype(BF16), wsb_ref[...]) + _mm(rw_ref[...].astype(BF16), wrw_ref[...])
    x1_ref[...] = x1
    ms = jnp.mean(x1 * x1, axis=-1, keepdims=True)
    h2 = x1 * lax.rsqrt(ms + NORM_EPS) * g_ref[...]
    h2_ref[...] = h2

    h_hi, h_lo = _split2(h2)
    w_hi, w_lo = _split2(wr_ref[...])
    lg = _mm(h_hi, w_hi) + _mm(h_hi, w_lo) + _mm(h_lo, w_hi)

    lane = lax.broadcasted_iota(jnp.int32, (tm, LANES), 1).astype(F32)
    neg = -jnp.inf
    big = float(LANES)
    is_group = lane < N_GROUPS
    gl = jnp.where(is_group, lg, neg)
    gmax = jnp.max(gl, axis=1, keepdims=True)
    gidx = jnp.min(jnp.where(gl == gmax, lane, big), axis=1, keepdims=True)
    group_gate = 1.0 / jnp.sum(jnp.where(is_group, jnp.exp(lg - gmax), 0.0), axis=1, keepdims=True)
    lo_lane = N_GROUPS + EXPERTS_PER_GROUP * gidx
    el = jnp.where((lane >= lo_lane) & (lane < lo_lane + EXPERTS_PER_GROUP), lg, neg)
    m1 = jnp.max(el, axis=1, keepdims=True)
    i1 = jnp.min(jnp.where(el == m1, lane, big), axis=1, keepdims=True)
    el2 = jnp.where(lane == i1, neg, el)
    m2 = jnp.max(el2, axis=1, keepdims=True)
    i2 = jnp.min(jnp.where(el2 == m2, lane, big), axis=1, keepdims=True)
    p2 = jnp.exp(m2 - m1)
    gate1 = group_gate / (1.0 + p2)
    gate2 = group_gate * p2 / (1.0 + p2)

    hit1 = lane == i1
    hit2 = lane == i2
    onehot = jnp.where(hit1 | hit2, 1.0, 0.0)
    rr = lax.broadcasted_iota(jnp.int32, (tm, tm), 0)
    cc = lax.broadcasted_iota(jnp.int32, (tm, tm), 1)
    below = jnp.where(cc < rr, 1.0, 0.0).astype(BF16)
    before = run_ref[...] + _mm(below, onehot.astype(BF16))
    rank1 = jnp.sum(jnp.where(hit1, before, 0.0), axis=1, keepdims=True)
    rank2 = jnp.sum(jnp.where(hit2, before, 0.0), axis=1, keepdims=True)
    run = run_ref[...] + jnp.sum(onehot, axis=0, keepdims=True)
    run_ref[...] = run
    cnt_ref[...] = run

    vals = (i1 - N_GROUPS, i2 - N_GROUPS, gate1, gate2, rank1, rank2)
    meta = jnp.zeros((tm, LANES), F32)
    for pos, val in enumerate(vals):
        meta = jnp.where(lane == pos, val.astype(F32), meta)
    meta_ref[...] = meta


def _outproj(sb_out, rw_out, x2, w_sb, w_rw, gain, w_router, tm):
    n, d = x2.shape
    rows = lambda c: pl.BlockSpec((tm, c), lambda i: (i, 0))
    full = lambda a: pl.BlockSpec(a.shape, lambda i: (0,) * a.ndim)
    return pl.pallas_call(
        functools.partial(_outproj_kernel, tm=tm),
        grid=(n // tm,),
        in_specs=[rows(sb_out.shape[1]), rows(rw_out.shape[1]), rows(d), full(w_sb), full(w_rw), full(gain),
                  full(w_router)],
        out_specs=[rows(d), rows(d), rows(LANES), pl.BlockSpec((1, LANES), lambda i: (0, 0))],
        out_shape=[jax.ShapeDtypeStruct((n, d), F32), jax.ShapeDtypeStruct((n, d), F32),
                   jax.ShapeDtypeStruct((n, LANES), F32), jax.ShapeDtypeStruct((1, LANES), F32)],
        scratch_shapes=[pltpu.VMEM((1, LANES), F32)],
        compiler_params=pltpu.CompilerParams(dimension_semantics=("arbitrary",),
                                             vmem_limit_bytes=VMEM_LIMIT),
        name="outproj",
    )(sb_out, rw_out, x2, w_sb, w_rw, gain, w_router)


def _row_copy(src_ref, src_row, dst_ref, dst_row, sem):
    return pltpu.make_async_copy(src_ref.at[pl.ds(src_row, 1)], dst_ref.at[pl.ds(dst_row, 1)], sem)


def _start_row_gather(src_ref, idx_ref, dst_ref, sem, n_rows):
    def issue(r, _):
        _row_copy(src_ref, idx_ref[0, r], dst_ref, r, sem).start()
        return 0

    lax.fori_loop(0, n_rows, issue, 0, unroll=8)


def _wait_row_gather(src_ref, dst_ref, sem, n_rows):
    def drain(r, _):
        _row_copy(src_ref, 0, dst_ref, 0, sem).wait()
        return 0

    lax.fori_loop(0, n_rows, drain, 0, unroll=8)


def _experts_kernel(eid_ref, used_ref, cur_ref, nxt_ref, h_ref, wg_ref, wu_ref, wd_ref, o_ref,
                    x_ref, wgb_ref, wub_ref, wdb_ref, sems):
    i = pl.program_id(0)
    used = used_ref[0]
    slot = i % 2
    rows = x_ref.shape[1]

    @pl.when(i == 0)
    def _():
        _start_row_gather(h_ref, cur_ref, x_ref.at[0], sems.at[0], rows)

    @pl.when(i + 1 < used)
    def _():
        _start_row_gather(h_ref, nxt_ref, x_ref.at[1 - slot], sems.at[1 - slot], rows)

    @pl.when((i == 0) | (eid_ref[i] != eid_ref[jnp.maximum(i - 1, 0)]))
    def _():
        wgb_ref[...] = wg_ref[...].astype(BF16)
        wub_ref[...] = wu_ref[...].astype(BF16)
        wdb_ref[...] = wd_ref[...].astype(BF16)

    @pl.when(i < used)
    def _():
        _wait_row_gather(h_ref, x_ref.at[slot], sems.at[slot], rows)
        xb = x_ref[slot].astype(BF16)
        hg = _mm(xb, wgb_ref[...])
        hu = _mm(xb, wub_ref[...])
        act = hg * _sigmoid(hg) * hu
        o_ref[...] = _mm(act.astype(BF16), wdb_ref[...])

    @pl.when(i >= used)
    def _():
        o_ref[...] = jnp.zeros_like(o_ref)


def _experts(blk_eid, n_used, src3, h2, w_g, w_u, w_d):
    n_blocks = src3.shape[0]
    d = h2.shape[1]
    de = w_g.shape[2]
    w_map = lambda i, eid, used: (eid[i], 0, 0)
    idx_spec = lambda ahead: pl.BlockSpec(
        (None, 1, EXPERT_ROWS), lambda i, eid, used: (jnp.minimum(i + ahead, used[0] - 1), 0, 0),
        memory_space=pltpu.SMEM)
    grid_spec = pltpu.PrefetchScalarGridSpec(
        num_scalar_prefetch=2,
        grid=(n_blocks,),
        in_specs=[idx_spec(0), idx_spec(1), pl.BlockSpec(memory_space=pl.ANY),
                  pl.BlockSpec((None, d, de), w_map), pl.BlockSpec((None, d, de), w_map),
                  pl.BlockSpec((None, de, d), w_map)],
        out_specs=pl.BlockSpec((EXPERT_ROWS, d), lambda i, eid, used: (i, 0)),
        scratch_shapes=[pltpu.VMEM((2, EXPERT_ROWS, d), F32),
                        pltpu.VMEM((d, de), BF16), pltpu.VMEM((d, de), BF16), pltpu.VMEM((de, d), BF16),
                        pltpu.SemaphoreType.DMA((2,))],
    )
    return pl.pallas_call(
        _experts_kernel,
        grid_spec=grid_spec,
        out_shape=jax.ShapeDtypeStruct((n_blocks * EXPERT_ROWS, d), F32),
        compiler_params=pltpu.CompilerParams(dimension_semantics=("arbitrary",),
                                             vmem_limit_bytes=VMEM_LIMIT),
        name="experts",
    )(blk_eid, n_used, src3, src3, h2, w_g, w_u, w_d)


def _combine_kernel(dest_ref, rows_ref, x1_ref, meta_ref, o_ref, got_ref, sem, *, tm):
    def issue(t, _):
        for kk in range(TOP_K):
            _row_copy(rows_ref, dest_ref[0, TOP_K * t + kk], got_ref.at[kk], t, sem).start()
        return 0

    lax.fori_loop(0, tm, issue, 0, unroll=4)
    for kk in range(TOP_K):
        _wait_row_gather(rows_ref, got_ref.at[kk], sem, tm)
    meta = meta_ref[...]
    o_ref[...] = x1_ref[...] + got_ref[0] * meta[:, 2:3] + got_ref[1] * meta[:, 3:4]


def _combine(dest3, exp_out, x1, meta, tm):
    n, d = x1.shape
    rows = lambda c: pl.BlockSpec((tm, c), lambda i: (i, 0))
    return pl.pallas_call(
        functools.partial(_combine_kernel, tm=tm),
        grid=(n // tm,),
        in_specs=[pl.BlockSpec((None, 1, TOP_K * tm), lambda i: (i, 0, 0), memory_space=pltpu.SMEM),
                  pl.BlockSpec(memory_space=pl.ANY), rows(d), rows(LANES)],
        out_specs=rows(d),
        out_shape=jax.ShapeDtypeStruct((n, d), F32),
        scratch_shapes=[pltpu.VMEM((TOP_K, tm, d), F32), pltpu.SemaphoreType.DMA(())],
        compiler_params=pltpu.CompilerParams(dimension_semantics=("arbitrary",),
                                             vmem_limit_bytes=VMEM_LIMIT),
        name="combine",
    )(dest3, exp_out, x1, meta)


def _layer(x, norm1_gain, w_in, sb_q_gain, sb_k_gain, rw_shift_mu, rw_w0, rw_w_up, rw_a0, rw_a_up, rw_g_up,
           rw_k_k, rw_k_a, rw_r_k, rw_ln_w, rw_ln_b, w_out, norm2_gain, w_router_group, w_router_expert,
           w_exp_gate, w_exp_up, w_exp_down):
    bsz, s, d = x.shape
    n = bsz * s
    sbw = d // 2
    rww = d - sbw
    n_decay, n_aaa = rw_w_up.shape[0], rw_a_up.shape[0]
    n_lora = n_decay + n_aaa + rw_g_up.shape[0]
    tm = min(256, n)
    row = lambda a: a.reshape(1, -1).astype(F32)

    x2 = x.reshape(n, d)
    w_bf = w_in.astype(BF16)
    sb, rw, lo = _inproj(x2, row(norm1_gain), w_bf[:, :3 * sbw], w_bf[:, 3 * sbw:3 * sbw + 3 * rww],
                         w_bf[:, 3 * sbw + 3 * rww:], tm)
    heads = lambda gvec: jnp.tile(row(gvec), (1, sbw // HEAD_DIM))
    sb_out = _sbattn(sb.reshape(bsz, s, 3 * sbw), heads(sb_q_gain), heads(sb_k_gain), min(128, s))

    pad_rows = lambda w_up, start: jnp.zeros((n_lora, rww), BF16).at[start:start + w_up.shape[0]].set(
        w_up.astype(BF16))
    mu = row(rw_shift_mu)
    prep = _rwprep(rw.reshape(bsz, s, 3 * rww), lo.reshape(bsz, s, n_lora), mu[:, :3 * rww], mu[:, 3 * rww:],
                   row(rw_w0), row(rw_a0), row(rw_k_k), row(rw_k_a),
                   pad_rows(rw_w_up, 0), pad_rows(rw_a_up, n_decay), pad_rows(rw_g_up, n_decay + n_aaa),
                   min(256, s))
    rw_out = _rwscan(*prep, row(rw_r_k), row(rw_ln_w), row(rw_ln_b), min(256, s))

    w_out_bf = w_out.astype(BF16)
    w_router = jnp.zeros((d, LANES), F32).at[:, :N_GROUPS].set(w_router_group).at[
        :, N_GROUPS:N_GROUPS + N_EXPERTS].set(w_router_expert)
    x1, h2, meta, counts = _outproj(sb_out.reshape(n, sbw), rw_out.reshape(n, rww), x2, w_out_bf[:sbw],
                                    w_out_bf[sbw:], row(norm2_gain), w_router, tm)

    counts = counts[0, N_GROUPS:N_GROUPS + N_EXPERTS].astype(jnp.int32)
    padded = (counts + EXPERT_ROWS - 1) // EXPERT_ROWS * EXPERT_ROWS
    pad_end = jnp.cumsum(padded)
    pad_start = pad_end - padded
    eid = meta[:, 0:TOP_K].astype(jnp.int32)
    rank = meta[:, 4:4 + TOP_K].astype(jnp.int32)
    dest = pad_start[eid] + rank
    dest3 = dest.reshape(n // tm, 1, TOP_K * tm)
    n_rows = n * TOP_K + N_EXPERTS * EXPERT_ROWS
    n_blocks = n_rows // EXPERT_ROWS
    block_start = jnp.arange(n_blocks, dtype=jnp.int32) * EXPERT_ROWS
    blk_eid = jnp.minimum(jnp.sum((pad_end[None, :] <= block_start[:, None]).astype(jnp.int32), axis=1),
                          N_EXPERTS - 1)
    n_used = (pad_end[-1:] // EXPERT_ROWS).astype(jnp.int32)

    src3 = jnp.zeros((n_rows,), jnp.int32).at[dest.reshape(-1)].set(
        jnp.arange(n * TOP_K, dtype=jnp.int32) // TOP_K).reshape(n_blocks, 1, EXPERT_ROWS)

    exp_out = _experts(blk_eid, n_used, src3, h2, w_exp_gate, w_exp_up, w_exp_down)
    return _combine(dest3, exp_out, x1, meta, tm).reshape(bsz, s, d)


def kernel(x, norm1_gain, w_in, sb_q_gain, sb_k_gain, rw_shift_mu, rw_w0, rw_w_up, rw_a0, rw_a_up, rw_g_up,
           rw_k_k, rw_k_a, rw_r_k, rw_ln_w, rw_ln_b, w_out, norm2_gain, w_router_group, w_router_expert,
           w_exp_gate, w_exp_up, w_exp_down):
    params = (norm1_gain, w_in, sb_q_gain, sb_k_gain, rw_shift_mu, rw_w0, rw_w_up, rw_a0, rw_a_up, rw_g_up,
              rw_k_k, rw_k_a, rw_r_k, rw_ln_w, rw_ln_b, w_out, norm2_gain, w_router_group, w_router_expert,
              w_exp_gate, w_exp_up, w_exp_down)
    for layer in range(norm1_gain.shape[0]):
        x = _layer(x, *(p[layer] for p in params))
    return x
```

```python
import functools

import jax
import jax.numpy as jnp
from jax import lax
from jax.experimental import pallas as pl
from jax.experimental.pallas import tpu as pltpu

F32 = jnp.float32
BF16 = jnp.bfloat16

HEAD_DIM = 64
NORM_EPS = 1e-6
LN_X_EPS = 64e-5
N_GROUPS = 4
EXPERTS_PER_GROUP = 8
N_EXPERTS = N_GROUPS * EXPERTS_PER_GROUP
TOP_K = 2

LANES = 128
MXU_DIM = 256
HEADS_PER_MXU = MXU_DIM // HEAD_DIM
RW_CHUNK = 64
EXPERT_ROWS = 256
UNDERFLOW_LOG = -110.0
VMEM_LIMIT = 56 * 1024 * 1024


def _nt(a, b):
    return lax.dot_general(a, b, (((1,), (1,)), ((), ())), preferred_element_type=F32)


def _tn(a, b):
    return lax.dot_general(a, b, (((0,), (0,)), ((), ())), preferred_element_type=F32)


def _mm(a, b):
    return jnp.dot(a, b, preferred_element_type=F32)


def _split2(x):
    hi = x.astype(BF16)
    lo = (x - hi.astype(F32)).astype(BF16)
    return hi, lo


def _split3(x):
    h1 = x.astype(BF16)
    r1 = x - h1.astype(F32)
    h2 = r1.astype(BF16)
    h3 = (r1 - h2.astype(F32)).astype(BF16)
    return h1, h2, h3


def _mm_exact_rhs(x, m):
    hi, lo = _split2(x)
    if 2 * x.shape[1] <= MXU_DIM:
        return _mm(jnp.concatenate([hi, lo], axis=1), jnp.concatenate([m, m], axis=0))
    return _mm(hi, m) + _mm(lo, m)


def _head_ones(width):
    r = lax.broadcasted_iota(jnp.int32, (width, width), 0) // HEAD_DIM
    c = lax.broadcasted_iota(jnp.int32, (width, width), 1) // HEAD_DIM
    return jnp.where(r == c, 1.0, 0.0).astype(BF16)


def _softplus(z):
    return jnp.maximum(z, 0.0) + jnp.log(1.0 + jnp.exp(-jnp.abs(z)))


def _sigmoid(z):
    return 1.0 / (1.0 + jnp.exp(-z))


def _inproj_kernel(x_ref, g_ref, wsb_ref, wrw_ref, wlo_ref, sb_ref, rw_ref, lo_ref):
    x = x_ref[...]
    ms = jnp.mean(x * x, axis=-1, keepdims=True)
    h = (x * lax.rsqrt(ms + NORM_EPS) * g_ref[...]).astype(BF16)
    sb_ref[...] = _mm(h, wsb_ref[...])
    rw_ref[...] = _mm(h, wrw_ref[...])
    lo_ref[...] = _mm(h, wlo_ref[...])


def _inproj(x2, gain, w_sb, w_rw, w_lo, tm):
    n, d = x2.shape
    full = lambda a: pl.BlockSpec(a.shape, lambda i: (0,) * a.ndim)
    rows = lambda c: pl.BlockSpec((tm, c), lambda i: (i, 0))
    return pl.pallas_call(
        _inproj_kernel,
        grid=(n // tm,),
        in_specs=[rows(d), full(gain), full(w_sb), full(w_rw), full(w_lo)],
        out_specs=[rows(w_sb.shape[1]), rows(w_rw.shape[1]), rows(w_lo.shape[1])],
        out_shape=[jax.ShapeDtypeStruct((n, w.shape[1]), F32) for w in (w_sb, w_rw, w_lo)],
        compiler_params=pltpu.CompilerParams(dimension_semantics=("arbitrary",),
                                             vmem_limit_bytes=VMEM_LIMIT),
        name="inproj",
    )(x2, gain, w_sb, w_rw, w_lo)


def _head_rms_norm(x, gain, ones):
    ms = _mm_exact_rhs(x * x, ones) * (1.0 / HEAD_DIM)
    return x * lax.rsqrt(ms + NORM_EPS) * gain


def _sbattn_kernel(q_ref, k_ref, v_ref, gq_ref, gk_ref, o_ref, kn_ref, vb_ref, *, blk, pairs):
    qb = pl.program_id(1)
    ones = _head_ones(pairs * LANES)

    @pl.when(qb == 0)
    def _():
        kn_ref[...] = _head_rms_norm(k_ref[...], gk_ref[...], ones).astype(BF16)
        vb_ref[...] = v_ref[...].astype(BF16)

    lane = lax.broadcasted_iota(jnp.int32, (1, pairs * LANES), 1)
    first = (lane % LANES) < HEAD_DIM
    qn = _head_rms_norm(q_ref[...], gq_ref[...], ones) * (HEAD_DIM ** -0.5)
    q_first = jnp.where(first, qn, 0.0).astype(BF16)
    q_second = jnp.where(first, 0.0, qn).astype(BF16)
    lanes_of = lambda p: slice(p * LANES, (p + 1) * LANES)
    qq = [jnp.concatenate([q_first[:, lanes_of(p)], q_second[:, lanes_of(p)]], axis=0) for p in range(pairs)]

    jj = lax.broadcasted_iota(jnp.int32, (blk, blk + LANES), 0)
    ss = lax.broadcasted_iota(jnp.int32, (blk, blk + LANES), 1)
    uu = jnp.where((jj > ss) | (ss >= blk), 1.0, 0.0).astype(BF16)

    rows = 2 * blk
    tt = lax.broadcasted_iota(jnp.int32, (pairs * rows, blk), 0) % blk
    sk = lax.broadcasted_iota(jnp.int32, (pairs * rows, blk), 1)
    causal = sk < tt

    def step(j, c, acc, diagonal):
        start = pl.multiple_of(j * blk, blk)
        z = jnp.concatenate([_nt(qq[p], kn_ref[pl.ds(start, blk), lanes_of(p)]) for p in range(pairs)], axis=0)
        lk = -_softplus(z)
        if diagonal:
            lk = jnp.where(causal, lk, 0.0)
        sums = _mm_exact_rhs(lk, uu)
        later = jnp.concatenate([c] * (blk // LANES), axis=1)
        w = jnp.exp(z + lk + sums[:, :blk] + later)
        if diagonal:
            w = jnp.where(causal, w, 0.0)
        w = w.astype(BF16)
        pv = jnp.concatenate([_mm(w[p * rows:(p + 1) * rows], vb_ref[pl.ds(start, blk), lanes_of(p)])
                              for p in range(pairs)], axis=0)
        c = c + sums[:, blk:]
        return c, acc + pv, jnp.max(c) > UNDERFLOW_LOG

    zero = jnp.zeros((pairs * rows, LANES), F32)
    carry = (qb - 1,) + step(qb, zero, zero, True)
    _, _, acc, _ = lax.while_loop(lambda cr: (cr[0] >= 0) & cr[3],
                                  lambda cr: (cr[0] - 1,) + step(cr[0], cr[1], cr[2], False), carry)
    first_pair = first[:, :LANES]
    for p in range(pairs):
        o_ref[:, lanes_of(p)] = jnp.where(first_pair, acc[p * rows:p * rows + blk], acc[p * rows + blk:(p + 1) * rows])


def _sbattn(sb3, gq, gk, blk):
    b, s, w3 = sb3.shape
    width = w3 // 3
    gspec = pl.BlockSpec((1, width), lambda bi, qb: (0, 0))
    return pl.pallas_call(
        functools.partial(_sbattn_kernel, blk=blk, pairs=width // LANES),
        grid=(b, s // blk),
        in_specs=[pl.BlockSpec((None, blk, width), lambda bi, qb: (bi, qb, 0)),
                  pl.BlockSpec((None, s, width), lambda bi, qb: (bi, 0, 1)),
                  pl.BlockSpec((None, s, width), lambda bi, qb: (bi, 0, 2)), gspec, gspec],
        out_specs=pl.BlockSpec((None, blk, width), lambda bi, qb: (bi, qb, 0)),
        out_shape=jax.ShapeDtypeStruct((b, s, width), F32),
        scratch_shapes=[pltpu.VMEM((s, width), BF16), pltpu.VMEM((s, width), BF16)],
        compiler_params=pltpu.CompilerParams(dimension_semantics=("arbitrary",) * 2,
                                             vmem_limit_bytes=VMEM_LIMIT),
        name="sbattn",
    )(sb3, sb3, sb3, gq, gk)


def _token_shift(p, prev_ref, mu, first_tile):
    rows = lax.broadcasted_iota(jnp.int32, (p.shape[0], 1), 0)
    last_prev = jnp.where(first_tile, 0.0, prev_ref[7:8, :])
    prev = jnp.where(rows == 0, last_prev, pltpu.roll(p, 1, axis=0))
    return p + (prev - p) * mu


def _rwprep_kernel(rw_ref, rwp_ref, lo_ref, lop_ref, mur_ref, mul_ref, w0_ref, a0_ref, kk_ref, ka_ref,
                   wup_ref, aup_ref, gup_ref,
                   r_ref, lw_ref, k_ref, v_ref, a_ref, b_ref, g_ref, *, width):
    first_tile = pl.program_id(1) == 0
    pf = _token_shift(rw_ref[...], rwp_ref, mur_ref[...], first_tile)
    lf = _token_shift(lo_ref[...], lop_ref, mul_ref[...], first_tile)
    r = pf[:, :width]
    k = pf[:, width:2 * width]
    v = pf[:, 2 * width:]
    w = -_softplus(-(w0_ref[...] + _mm(jnp.tanh(lf).astype(BF16), wup_ref[...]))) - 0.5
    lr = _sigmoid(a0_ref[...] + _mm(lf.astype(BF16), aup_ref[...]))
    kk = k * kk_ref[...]
    ss = _mm_exact_rhs(kk * kk, _head_ones(width))
    kk = kk / jnp.maximum(jnp.sqrt(ss), 1e-12)
    r_ref[...] = r
    lw_ref[...] = -jnp.exp(w)
    k_ref[...] = k * (1.0 + (lr - 1.0) * ka_ref[...])
    v_ref[...] = v
    a_ref[...] = -kk
    b_ref[...] = kk * lr
    g_ref[...] = _mm(_sigmoid(lf).astype(BF16), gup_ref[...])


def _rwprep(rw3, lo3, mu_rw, mu_lo, w0, a0, k_k, k_a, wup, aup, gup, tm):
    b, s, w3 = rw3.shape
    width = w3 // 3
    nlo = lo3.shape[2]
    cur = lambda c: pl.BlockSpec((None, tm, c), lambda bi, i: (bi, i, 0))
    prev = lambda c: pl.BlockSpec((None, 8, c), lambda bi, i: (bi, jnp.maximum(i * (tm // 8) - 1, 0), 0))
    full = lambda a: pl.BlockSpec(a.shape, lambda bi, i: (0,) * a.ndim)
    return pl.pallas_call(
        functools.partial(_rwprep_kernel, width=width),
        grid=(b, s // tm),
        in_specs=[cur(w3), prev(w3), cur(nlo), prev(nlo), full(mu_rw), full(mu_lo), full(w0), full(a0),
                  full(k_k), full(k_a), full(wup), full(aup), full(gup)],
        out_specs=[cur(width)] * 7,
        out_shape=[jax.ShapeDtypeStruct((b, s, width), F32)] * 7,
        compiler_params=pltpu.CompilerParams(dimension_semantics=("arbitrary",) * 2,
                                             vmem_limit_bytes=VMEM_LIMIT),
        name="rwprep",
    )(rw3, rw3, lo3, lo3, mu_rw, mu_lo, w0, a0, k_k, k_a, wup, aup, gup)


def _rwscan_kernel(r_ref, lw_ref, k_ref, v_ref, a_ref, b_ref, g_ref, rk_ref, lnw_ref, lnb_ref,
                   o_ref, st_ref, y_ref, *, tb, width):
    t = RW_CHUNK
    m = MXU_DIM
    groups = width // m

    @pl.when(pl.program_id(1) == 0)
    def _():
        st_ref[...] = jnp.zeros_like(st_ref)

    ri = lax.broadcasted_iota(jnp.int32, (m, m), 0)
    ci = lax.broadcasted_iota(jnp.int32, (m, m), 1)
    same = (ri // HEAD_DIM) == (ci // HEAD_DIM)
    bdmask = jnp.where(same, 1.0, 0.0)
    strict = same & ((ci % HEAD_DIM) < (ri % HEAD_DIM))
    incl = same & ((ci % HEAD_DIM) <= (ri % HEAD_DIM))
    eye = jnp.where(ri == ci, 1.0, 0.0)
    ti = lax.broadcasted_iota(jnp.int32, (t, t), 0)
    si = lax.broadcasted_iota(jnp.int32, (t, t), 1)
    tri = jnp.where(si <= ti, 1.0, 0.0).astype(BF16)

    def blockdiag(x):
        return (jnp.concatenate([x] * HEADS_PER_MXU, axis=0) * bdmask).astype(BF16)

    for grp in range(groups):
        cols = slice(grp * m, (grp + 1) * m)
        for ch in range(tb // t):
            rows = slice(ch * t, (ch + 1) * t)
            r = r_ref[rows, cols]
            lw = lw_ref[rows, cols]
            k = k_ref[rows, cols]
            v = v_ref[rows, cols]
            a = a_ref[rows, cols]
            b = b_ref[rows, cols]

            l1, l2, l3 = _split3(lw)
            cum = _mm(tri, l1) + _mm(tri, l2) + _mm(tri, l3)
            cum_end = cum[t - 1:t, :]
            p_inv = jnp.exp(-cum)
            p_rest = jnp.exp(cum_end - cum)
            r_bd = blockdiag(r * jnp.exp(cum))
            a_bd = blockdiag(a * jnp.exp(cum - lw))
            k_bd = blockdiag(k * p_inv)
            b_bd = blockdiag(b * p_inv)
            kd_bd = blockdiag(k * p_rest)
            bd_bd = blockdiag(b * p_rest)
            v_bd = blockdiag(v)

            aab = jnp.where(strict, _nt(a_bd, b_bd), 0.0)
            aak = jnp.where(strict, _nt(a_bd, k_bd), 0.0).astype(BF16)
            brb = jnp.where(incl, _nt(r_bd, b_bd), 0.0).astype(BF16)
            brk = jnp.where(incl, _nt(r_bd, k_bd), 0.0).astype(BF16)

            tinv = eye + aab
            apow = aab.astype(BF16)
            for _ in range((t - 1).bit_length() - 1):
                sq = _mm(apow, apow)
                apow = sq.astype(BF16)
                tinv = tinv + _mm(tinv.astype(BF16), apow)
            tinv = tinv.astype(BF16)

            w_mat = _mm(tinv, a_bd).astype(BF16)
            u0 = _mm(tinv, _mm(aak, v_bd).astype(BF16)).astype(BF16)
            q_mat = (r_bd.astype(F32) + _mm(brb, w_mat)).astype(BF16)
            y0 = _mm(brb, u0) + _mm(brk, v_bd)
            m_mat = (eye * jnp.exp(cum_end) + _tn(bd_bd, w_mat)).astype(BF16)
            c0 = _tn(bd_bd, u0) + _tn(kd_bd, v_bd)

            s0 = st_ref[grp]
            s_hi, s_lo = _split2(s0)
            y = _mm(q_mat, s_hi) + _mm(q_mat, s_lo) + y0
            st_ref[grp] = _mm(m_mat, s_hi) + _mm(m_mat, s_lo) + c0
            y_nat = y[0:t]
            for h in range(1, HEADS_PER_MXU):
                y_nat = y_nat + y[h * t:(h + 1) * t]
            y_ref[rows, cols] = y_nat

    ones = _head_ones(m)
    inv = 1.0 / HEAD_DIM
    for grp in range(groups):
        cols = slice(grp * m, (grp + 1) * m)
        y = y_ref[:, cols]
        mean = _mm_exact_rhs(y, ones) * inv
        d = y - mean
        var = _mm_exact_rhs(d * d, ones) * inv
        yn = d * lax.rsqrt(var + LN_X_EPS) * lnw_ref[:, cols] + lnb_ref[:, cols]
        bonus = _mm_exact_rhs(r_ref[:, cols] * k_ref[:, cols] * rk_ref[:, cols], ones) * v_ref[:, cols]
        o_ref[:, cols] = (yn + bonus) * g_ref[:, cols]


def _rwscan(r, lw, k, v, a, b_, g, r_k, ln_w, ln_b, tb):
    bsz, s, width = r.shape
    cur = pl.BlockSpec((None, tb, width), lambda bi, i: (bi, i, 0))
    full = lambda x: pl.BlockSpec(x.shape, lambda bi, i: (0,) * x.ndim)
    return pl.pallas_call(
        functools.partial(_rwscan_kernel, tb=tb, width=width),
        grid=(bsz, s // tb),
        in_specs=[cur] * 7 + [full(r_k), full(ln_w), full(ln_b)],
        out_specs=cur,
        out_shape=jax.ShapeDtypeStruct((bsz, s, width), F32),
        scratch_shapes=[pltpu.VMEM((width // MXU_DIM, MXU_DIM, MXU_DIM), F32),
                        pltpu.VMEM((tb, width), F32)],
        compiler_params=pltpu.CompilerParams(dimension_semantics=("arbitrary",) * 2,
                                             vmem_limit_bytes=VMEM_LIMIT),
        name="rwscan",
    )(r, lw, k, v, a, b_, g, r_k, ln_w, ln_b)


def _outproj_kernel(sb_ref, rw_ref, x_ref, wsb_ref, wrw_ref, g_ref, wr_ref,
                    x1_ref, h2_ref, meta_ref, cnt_ref, run_ref, *, tm):
    @pl.when(pl.program_id(0) == 0)
    def _():
        run_ref[...] = jnp.zeros_like(run_ref)

    x1 = x_ref[...] + _mm(sb_ref[...].astype(BF16), wsb_ref[...]) + _mm(rw_ref[...].astype(BF16), wrw_ref[...])
    x1_ref[...] = x1
    ms = jnp.mean(x1 * x1, axis=-1, keepdims=True)
    h2 = x1 * lax.rsqrt(ms + NORM_EPS) * g_ref[...]
    h2_ref[...] = h2

    h_hi, h_lo = _split2(h2)
    w_hi, w_lo = _split2(wr_ref[...])
    lg = _mm(h_hi, w_hi) + _mm(h_hi, w_lo) + _mm(h_lo, w_hi)

    lane = lax.broadcasted_iota(jnp.int32, (tm, LANES), 1).astype(F32)
    neg = -jnp.inf
    big = float(LANES)
    is_group = lane < N_GROUPS
    gl = jnp.where(is_group, lg, neg)
    gmax = jnp.max(gl, axis=1, keepdims=True)
    gidx = jnp.min(jnp.where(gl == gmax, lane, big), axis=1, keepdims=True)
    group_gate = 1.0 / jnp.sum(jnp.where(is_group, jnp.exp(lg - gmax), 0.0), axis=1, keepdims=True)
    lo_lane = N_GROUPS + EXPERTS_PER_GROUP * gidx
    el = jnp.where((lane >= lo_lane) & (lane < lo_lane + EXPERTS_PER_GROUP), lg, neg)
    m1 = jnp.max(el, axis=1, keepdims=True)
    i1 = jnp.min(jnp.where(el == m1, lane, big), axis=1, keepdims=True)
    el2 = jnp.where(lane == i1, neg, el)
    m2 = jnp.max(el2, axis=1, keepdims=True)
    i2 = jnp.min(jnp.where(el2 == m2, lane, big), axis=1, keepdims=True)
    p2 = jnp.exp(m2 - m1)
    gate1 = group_gate / (1.0 + p2)
    gate2 = group_gate * p2 / (1.0 + p2)

    hit1 = lane == i1
    hit2 = lane == i2
    onehot = jnp.where(hit1 | hit2, 1.0, 0.0)
    rr = lax.broadcasted_iota(jnp.int32, (tm, tm), 0)
    cc = lax.broadcasted_iota(jnp.int32, (tm, tm), 1)
    below = jnp.where(cc < rr, 1.0, 0.0).astype(BF16)
    before = run_ref[...] + _mm(below, onehot.astype(BF16))
    rank1 = jnp.sum(jnp.where(hit1, before, 0.0), axis=1, keepdims=True)
    rank2 = jnp.sum(jnp.where(hit2, before, 0.0), axis=1, keepdims=True)
    run = run_ref[...] + jnp.sum(onehot, axis=0, keepdims=True)
    run_ref[...] = run
    cnt_ref[...] = run

    vals = (i1 - N_GROUPS, i2 - N_GROUPS, gate1, gate2, rank1, rank2)
    meta = jnp.zeros((tm, LANES), F32)
    for pos, val in enumerate(vals):
        meta = jnp.where(lane == pos, val.astype(F32), meta)
    meta_ref[...] = meta


def _outproj(sb_out, rw_out, x2, w_sb, w_rw, gain, w_router, tm):
    n, d = x2.shape
    rows = lambda c: pl.BlockSpec((tm, c), lambda i: (i, 0))
    full = lambda a: pl.BlockSpec(a.shape, lambda i: (0,) * a.ndim)
    return pl.pallas_call(
        functools.partial(_outproj_kernel, tm=tm),
        grid=(n // tm,),
        in_specs=[rows(sb_out.shape[1]), rows(rw_out.shape[1]), rows(d), full(w_sb), full(w_rw), full(gain),
                  full(w_router)],
        out_specs=[rows(d), rows(d), rows(LANES), pl.BlockSpec((1, LANES), lambda i: (0, 0))],
        out_shape=[jax.ShapeDtypeStruct((n, d), F32), jax.ShapeDtypeStruct((n, d), F32),
                   jax.ShapeDtypeStruct((n, LANES), F32), jax.ShapeDtypeStruct((1, LANES), F32)],
        scratch_shapes=[pltpu.VMEM((1, LANES), F32)],
        compiler_params=pltpu.CompilerParams(dimension_semantics=("arbitrary",),
                                             vmem_limit_bytes=VMEM_LIMIT),
        name="outproj",
    )(sb_out, rw_out, x2, w_sb, w_rw, gain, w_router)


def _row_copy(src_ref, src_row, dst_ref, dst_row, sem):
    return pltpu.make_async_copy(src_ref.at[pl.ds(src_row, 1)], dst_ref.at[pl.ds(dst_row, 1)], sem)


def _start_row_gather(src_ref, idx_ref, dst_ref, sem, n_rows):
    def issue(r, _):
        _row_copy(src_ref, idx_ref[0, r], dst_ref, r, sem).start()
        return 0

    lax.fori_loop(0, n_rows, issue, 0, unroll=8)


def _wait_row_gather(src_ref, dst_ref, sem, n_rows):
    def drain(r, _):
        _row_copy(src_ref, 0, dst_ref, 0, sem).wait()
        return 0

    lax.fori_loop(0, n_rows, drain, 0, unroll=8)


def _experts_kernel(eid_ref, used_ref, cur_ref, nxt_ref, h_ref, wg_ref, wu_ref, wd_ref, o_ref,
                    x_ref, wgb_ref, wub_ref, wdb_ref, sems):
    i = pl.program_id(0)
    used = used_ref[0]
    slot = i % 2
    rows = x_ref.shape[1]

    @pl.when(i == 0)
    def _():
        _start_row_gather(h_ref, cur_ref, x_ref.at[0], sems.at[0], rows)

    @pl.when(i + 1 < used)
    def _():
        _start_row_gather(h_ref, nxt_ref, x_ref.at[1 - slot], sems.at[1 - slot], rows)

    @pl.when((i == 0) | (eid_ref[i] != eid_ref[jnp.maximum(i - 1, 0)]))
    def _():
        wgb_ref[...] = wg_ref[...].astype(BF16)
        wub_ref[...] = wu_ref[...].astype(BF16)
        wdb_ref[...] = wd_ref[...].astype(BF16)

    @pl.when(i < used)
    def _():
        _wait_row_gather(h_ref, x_ref.at[slot], sems.at[slot], rows)
        xb = x_ref[slot].astype(BF16)
        hg = _mm(xb, wgb_ref[...])
        hu = _mm(xb, wub_ref[...])
        act = hg * _sigmoid(hg) * hu
        o_ref[...] = _mm(act.astype(BF16), wdb_ref[...])

    @pl.when(i >= used)
    def _():
        o_ref[...] = jnp.zeros_like(o_ref)


def _experts(blk_eid, n_used, src3, h2, w_g, w_u, w_d):
    n_blocks = src3.shape[0]
    d = h2.shape[1]
    de = w_g.shape[2]
    w_map = lambda i, eid, used: (eid[i], 0, 0)
    idx_spec = lambda ahead: pl.BlockSpec(
        (None, 1, EXPERT_ROWS), lambda i, eid, used: (jnp.minimum(i + ahead, used[0] - 1), 0, 0),
        memory_space=pltpu.SMEM)
    grid_spec = pltpu.PrefetchScalarGridSpec(
        num_scalar_prefetch=2,
        grid=(n_blocks,),
        in_specs=[idx_spec(0), idx_spec(1), pl.BlockSpec(memory_space=pl.ANY),
                  pl.BlockSpec((None, d, de), w_map), pl.BlockSpec((None, d, de), w_map),
                  pl.BlockSpec((None, de, d), w_map)],
        out_specs=pl.BlockSpec((EXPERT_ROWS, d), lambda i, eid, used: (i, 0)),
        scratch_shapes=[pltpu.VMEM((2, EXPERT_ROWS, d), F32),
                        pltpu.VMEM((d, de), BF16), pltpu.VMEM((d, de), BF16), pltpu.VMEM((de, d), BF16),
                        pltpu.SemaphoreType.DMA((2,))],
    )
    return pl.pallas_call(
        _experts_kernel,
        grid_spec=grid_spec,
        out_shape=jax.ShapeDtypeStruct((n_blocks * EXPERT_ROWS, d), F32),
        compiler_params=pltpu.CompilerParams(dimension_semantics=("arbitrary",),
                                             vmem_limit_bytes=VMEM_LIMIT),
        name="experts",
    )(blk_eid, n_used, src3, src3, h2, w_g, w_u, w_d)


def _combine_kernel(dest_ref, rows_ref, x1_ref, meta_ref, o_ref, got_ref, sem, *, tm):
    def issue(t, _):
        for kk in range(TOP_K):
            _row_copy(rows_ref, dest_ref[0, TOP_K * t + kk], got_ref.at[kk], t, sem).start(priority=kk)
        return 0

    lax.fori_loop(0, tm, issue, 0, unroll=4)
    for kk in range(TOP_K):
        _wait_row_gather(rows_ref, got_ref.at[kk], sem, tm)
    meta = meta_ref[...]
    o_ref[...] = x1_ref[...] + got_ref[0] * meta[:, 2:3] + got_ref[1] * meta[:, 3:4]


def _combine(dest3, exp_out, x1, meta, tm):
    n, d = x1.shape
    rows = lambda c: pl.BlockSpec((tm, c), lambda i: (i, 0))
    return pl.pallas_call(
        functools.partial(_combine_kernel, tm=tm),
        grid=(n // tm,),
        in_specs=[pl.BlockSpec((None, 1, TOP_K * tm), lambda i: (i, 0, 0), memory_space=pltpu.SMEM),
                  pl.BlockSpec(memory_space=pl.ANY), rows(d), rows(LANES)],
        out_specs=rows(d),
        out_shape=jax.ShapeDtypeStruct((n, d), F32),
        scratch_shapes=[pltpu.VMEM((TOP_K, tm, d), F32), pltpu.SemaphoreType.DMA(())],
        compiler_params=pltpu.CompilerParams(dimension_semantics=("arbitrary",),
                                             vmem_limit_bytes=VMEM_LIMIT),
        name="combine",
    )(dest3, exp_out, x1, meta)


def _layer(x, norm1_gain, w_in, sb_q_gain, sb_k_gain, rw_shift_mu, rw_w0, rw_w_up, rw_a0, rw_a_up, rw_g_up,
           rw_k_k, rw_k_a, rw_r_k, rw_ln_w, rw_ln_b, w_out, norm2_gain, w_router_group, w_router_expert,
           w_exp_gate, w_exp_up, w_exp_down):
    bsz, s, d = x.shape
    n = bsz * s
    sbw = d // 2
    rww = d - sbw
    n_decay, n_aaa = rw_w_up.shape[0], rw_a_up.shape[0]
    n_lora = n_decay + n_aaa + rw_g_up.shape[0]
    tm = min(256, n)
    row = lambda a: a.reshape(1, -1).astype(F32)

    x2 = x.reshape(n, d)
    w_bf = w_in.astype(BF16)
    sb, rw, lo = _inproj(x2, row(norm1_gain), w_bf[:, :3 * sbw], w_bf[:, 3 * sbw:3 * sbw + 3 * rww],
                         w_bf[:, 3 * sbw + 3 * rww:], tm)
    heads = lambda gvec: jnp.tile(row(gvec), (1, sbw // HEAD_DIM))
    sb_out = _sbattn(sb.reshape(bsz, s, 3 * sbw), heads(sb_q_gain), heads(sb_k_gain), min(128, s))

    pad_rows = lambda w_up, start: jnp.zeros((n_lora, rww), BF16).at[start:start + w_up.shape[0]].set(
        w_up.astype(BF16))
    mu = row(rw_shift_mu)
    prep = _rwprep(rw.reshape(bsz, s, 3 * rww), lo.reshape(bsz, s, n_lora), mu[:, :3 * rww], mu[:, 3 * rww:],
                   row(rw_w0), row(rw_a0), row(rw_k_k), row(rw_k_a),
                   pad_rows(rw_w_up, 0), pad_rows(rw_a_up, n_decay), pad_rows(rw_g_up, n_decay + n_aaa),
                   min(256, s))
    rw_out = _rwscan(*prep, row(rw_r_k), row(rw_ln_w), row(rw_ln_b), min(256, s))

    w_out_bf = w_out.astype(BF16)
    w_router = jnp.zeros((d, LANES), F32).at[:, :N_GROUPS].set(w_router_group).at[
        :, N_GROUPS:N_GROUPS + N_EXPERTS].set(w_router_expert)
    x1, h2, meta, counts = _outproj(sb_out.reshape(n, sbw), rw_out.reshape(n, rww), x2, w_out_bf[:sbw],
                                    w_out_bf[sbw:], row(norm2_gain), w_router, tm)

    counts = counts[0, N_GROUPS:N_GROUPS + N_EXPERTS].astype(jnp.int32)
    padded = (counts + EXPERT_ROWS - 1) // EXPERT_ROWS * EXPERT_ROWS
    pad_end = jnp.cumsum(padded)
    pad_start = pad_end - padded
    eid = meta[:, 0:TOP_K].astype(jnp.int32)
    rank = meta[:, 4:4 + TOP_K].astype(jnp.int32)
    dest = pad_start[eid] + rank
    dest3 = dest.reshape(n // tm, 1, TOP_K * tm)
    n_rows = n * TOP_K + N_EXPERTS * EXPERT_ROWS
    n_blocks = n_rows // EXPERT_ROWS
    block_start = jnp.arange(n_blocks, dtype=jnp.int32) * EXPERT_ROWS
    blk_eid = jnp.minimum(jnp.sum((pad_end[None, :] <= block_start[:, None]).astype(jnp.int32), axis=1),
                          N_EXPERTS - 1)
    n_used = (pad_end[-1:] // EXPERT_ROWS).astype(jnp.int32)

    src3 = jnp.zeros((n_rows,), jnp.int32).at[dest.reshape(-1)].set(
        jnp.arange(n * TOP_K, dtype=jnp.int32) // TOP_K).reshape(n_blocks, 1, EXPERT_ROWS)

    exp_out = _experts(blk_eid, n_used, src3, h2, w_exp_gate, w_exp_up, w_exp_down)
    return _combine(dest3, exp_out, x1, meta, tm).reshape(bsz, s, d)


def kernel(x, norm1_gain, w_in, sb_q_gain, sb_k_gain, rw_shift_mu, rw_w0, rw_w_up, rw_a0, rw_a_up, rw_g_up,
           rw_k_k, rw_k_a, rw_r_k, rw_ln_w, rw_ln_b, w_out, norm2_gain, w_router_group, w_router_expert,
           w_exp_gate, w_exp_up, w_exp_down):
    params = (norm1_gain, w_in, sb_q_gain, sb_k_gain, rw_shift_mu, rw_w0, rw_w_up, rw_a0, rw_a_up, rw_g_up,
              rw_k_k, rw_k_a, rw_r_k, rw_ln_w, rw_ln_b, w_out, norm2_gain, w_router_group, w_router_expert,
              w_exp_gate, w_exp_up, w_exp_down)
    for layer in range(norm1_gain.shape[0]):
        x = _layer(x, *(p[layer] for p in params))
    return x
```

```python
import functools

import jax
import jax.numpy as jnp
from jax import lax
from jax.experimental import pallas as pl
from jax.experimental.pallas import tpu as pltpu

F32 = jnp.float32
BF16 = jnp.bfloat16

HEAD_DIM = 64
NORM_EPS = 1e-6
LN_X_EPS = 64e-5
N_GROUPS = 4
EXPERTS_PER_GROUP = 8
N_EXPERTS = N_GROUPS * EXPERTS_PER_GROUP
TOP_K = 2

LANES = 128
MXU_DIM = 256
RW_CHUNK = 64
EXPERT_ROWS = 256
UNDERFLOW_LOG = -110.0
VMEM_LIMIT = 56 * 1024 * 1024


def _nt(a, b):
    return lax.dot_general(a, b, (((1,), (1,)), ((), ())), preferred_element_type=F32)


def _tn(a, b):
    return lax.dot_general(a, b, (((0,), (0,)), ((), ())), preferred_element_type=F32)


def _mm(a, b):
    return jnp.dot(a, b, preferred_element_type=F32)


def _split2(x):
    hi = x.astype(BF16)
    lo = (x - hi.astype(F32)).astype(BF16)
    return hi, lo


def _split3(x):
    h1 = x.astype(BF16)
    r1 = x - h1.astype(F32)
    h2 = r1.astype(BF16)
    h3 = (r1 - h2.astype(F32)).astype(BF16)
    return h1, h2, h3


def _mm_exact_rhs(x, m):
    hi, lo = _split2(x)
    if 2 * x.shape[1] <= MXU_DIM:
        return _mm(jnp.concatenate([hi, lo], axis=1), jnp.concatenate([m, m], axis=0))
    return _mm(hi, m) + _mm(lo, m)


def _head_ones(width):
    r = lax.broadcasted_iota(jnp.int32, (width, width), 0) // HEAD_DIM
    c = lax.broadcasted_iota(jnp.int32, (width, width), 1) // HEAD_DIM
    return jnp.where(r == c, 1.0, 0.0).astype(BF16)


def _softplus(z):
    return jnp.maximum(z, 0.0) + jnp.log(1.0 + jnp.exp(-jnp.abs(z)))


def _sigmoid(z):
    return 1.0 / (1.0 + jnp.exp(-z))


def _inproj_kernel(x_ref, g_ref, wsb_ref, wrw_ref, wlo_ref, sb_ref, rw_ref, lo_ref):
    x = x_ref[...]
    ms = jnp.mean(x * x, axis=-1, keepdims=True)
    h = (x * lax.rsqrt(ms + NORM_EPS) * g_ref[...]).astype(BF16)
    sb_ref[...] = _mm(h, wsb_ref[...])
    rw_ref[...] = _mm(h, wrw_ref[...])
    lo_ref[...] = _mm(h, wlo_ref[...])


def _inproj(x2, gain, w_sb, w_rw, w_lo, tm):
    n, d = x2.shape
    full = lambda a: pl.BlockSpec(a.shape, lambda i: (0,) * a.ndim)
    rows = lambda c: pl.BlockSpec((tm, c), lambda i: (i, 0))
    return pl.pallas_call(
        _inproj_kernel,
        grid=(n // tm,),
        in_specs=[rows(d), full(gain), full(w_sb), full(w_rw), full(w_lo)],
        out_specs=[rows(w_sb.shape[1]), rows(w_rw.shape[1]), rows(w_lo.shape[1])],
        out_shape=[jax.ShapeDtypeStruct((n, w.shape[1]), F32) for w in (w_sb, w_rw, w_lo)],
        compiler_params=pltpu.CompilerParams(dimension_semantics=("arbitrary",),
                                             vmem_limit_bytes=VMEM_LIMIT),
        name="inproj",
    )(x2, gain, w_sb, w_rw, w_lo)


def _head_rms_norm(x, gain, ones):
    ms = _mm_exact_rhs(x * x, ones) * (1.0 / HEAD_DIM)
    return x * lax.rsqrt(ms + NORM_EPS) * gain


def _sbattn_kernel(q_ref, k_ref, v_ref, gq_ref, gk_ref, o_ref, kn_ref, vb_ref, *, blk, pairs):
    qb = pl.program_id(1)
    ones = _head_ones(pairs * LANES)

    @pl.when(qb == 0)
    def _():
        kn_ref[...] = _head_rms_norm(k_ref[...], gk_ref[...], ones).astype(BF16)
        vb_ref[...] = v_ref[...].astype(BF16)

    lane = lax.broadcasted_iota(jnp.int32, (1, pairs * LANES), 1)
    first = (lane % LANES) < HEAD_DIM
    qn = _head_rms_norm(q_ref[...], gq_ref[...], ones) * (HEAD_DIM ** -0.5)
    q_first = jnp.where(first, qn, 0.0).astype(BF16)
    q_second = jnp.where(first, 0.0, qn).astype(BF16)
    lanes_of = lambda p: slice(p * LANES, (p + 1) * LANES)
    qq = [jnp.concatenate([q_first[:, lanes_of(p)], q_second[:, lanes_of(p)]], axis=0) for p in range(pairs)]

    jj = lax.broadcasted_iota(jnp.int32, (blk, blk + LANES), 0)
    ss = lax.broadcasted_iota(jnp.int32, (blk, blk + LANES), 1)
    uu = jnp.where((jj > ss) | (ss >= blk), 1.0, 0.0).astype(BF16)

    rows = 2 * blk
    tt = lax.broadcasted_iota(jnp.int32, (pairs * rows, blk), 0) % blk
    sk = lax.broadcasted_iota(jnp.int32, (pairs * rows, blk), 1)
    causal = sk < tt

    def step(j, c, acc, diagonal):
        start = pl.multiple_of(j * blk, blk)
        z = jnp.concatenate([_nt(qq[p], kn_ref[pl.ds(start, blk), lanes_of(p)]) for p in range(pairs)], axis=0)
        lk = -_softplus(z)
        if diagonal:
            lk = jnp.where(causal, lk, 0.0)
        sums = _mm_exact_rhs(lk, uu)
        later = jnp.concatenate([c] * (blk // LANES), axis=1)
        w = jnp.exp(z + lk + sums[:, :blk] + later)
        if diagonal:
            w = jnp.where(causal, w, 0.0)
        w = w.astype(BF16)
        pv = jnp.concatenate([_mm(w[p * rows:(p + 1) * rows], vb_ref[pl.ds(start, blk), lanes_of(p)])
                              for p in range(pairs)], axis=0)
        c = c + sums[:, blk:]
        return c, acc + pv, jnp.max(c) > UNDERFLOW_LOG

    zero = jnp.zeros((pairs * rows, LANES), F32)
    carry = (qb - 1,) + step(qb, zero, zero, True)
    _, _, acc, _ = lax.while_loop(lambda cr: (cr[0] >= 0) & cr[3],
                                  lambda cr: (cr[0] - 1,) + step(cr[0], cr[1], cr[2], False), carry)
    first_pair = first[:, :LANES]
    for p in range(pairs):
        o_ref[:, lanes_of(p)] = jnp.where(first_pair, acc[p * rows:p * rows + blk], acc[p * rows + blk:(p + 1) * rows])


def _sbattn(sb3, gq, gk, blk):
    b, s, w3 = sb3.shape
    width = w3 // 3
    gspec = pl.BlockSpec((1, width), lambda bi, qb: (0, 0))
    return pl.pallas_call(
        functools.partial(_sbattn_kernel, blk=blk, pairs=width // LANES),
        grid=(b, s // blk),
        in_specs=[pl.BlockSpec((None, blk, width), lambda bi, qb: (bi, qb, 0)),
                  pl.BlockSpec((None, s, width), lambda bi, qb: (bi, 0, 1)),
                  pl.BlockSpec((None, s, width), lambda bi, qb: (bi, 0, 2)), gspec, gspec],
        out_specs=pl.BlockSpec((None, blk, width), lambda bi, qb: (bi, qb, 0)),
        out_shape=jax.ShapeDtypeStruct((b, s, width), F32),
        scratch_shapes=[pltpu.VMEM((s, width), BF16), pltpu.VMEM((s, width), BF16)],
        compiler_params=pltpu.CompilerParams(dimension_semantics=("arbitrary",) * 2,
                                             vmem_limit_bytes=VMEM_LIMIT),
        name="sbattn",
    )(sb3, sb3, sb3, gq, gk)


def _token_shift(p, prev_ref, mu, first_tile):
    rows = lax.broadcasted_iota(jnp.int32, (p.shape[0], 1), 0)
    last_prev = jnp.where(first_tile, 0.0, prev_ref[7:8, :])
    prev = jnp.where(rows == 0, last_prev, pltpu.roll(p, 1, axis=0))
    return p + (prev - p) * mu


def _rwprep_kernel(rw_ref, rwp_ref, lo_ref, lop_ref, mur_ref, mul_ref, w0_ref, a0_ref, kk_ref, ka_ref,
                   wup_ref, aup_ref, gup_ref,
                   r_ref, lw_ref, k_ref, v_ref, a_ref, b_ref, g_ref, *, width):
    first_tile = pl.program_id(1) == 0
    pf = _token_shift(rw_ref[...], rwp_ref, mur_ref[...], first_tile)
    lf = _token_shift(lo_ref[...], lop_ref, mul_ref[...], first_tile)
    r = pf[:, :width]
    k = pf[:, width:2 * width]
    v = pf[:, 2 * width:]
    w = -_softplus(-(w0_ref[...] + _mm(jnp.tanh(lf).astype(BF16), wup_ref[...]))) - 0.5
    lr = _sigmoid(a0_ref[...] + _mm(lf.astype(BF16), aup_ref[...]))
    kk = k * kk_ref[...]
    ss = _mm_exact_rhs(kk * kk, _head_ones(width))
    kk = kk / jnp.maximum(jnp.sqrt(ss), 1e-12)
    r_ref[...] = r
    lw_ref[...] = -jnp.exp(w)
    k_ref[...] = k * (1.0 + (lr - 1.0) * ka_ref[...])
    v_ref[...] = v
    a_ref[...] = -kk
    b_ref[...] = kk * lr
    g_ref[...] = _mm(_sigmoid(lf).astype(BF16), gup_ref[...])


def _rwprep(rw3, lo3, mu_rw, mu_lo, w0, a0, k_k, k_a, wup, aup, gup, tm):
    b, s, w3 = rw3.shape
    width = w3 // 3
    nlo = lo3.shape[2]
    cur = lambda c: pl.BlockSpec((None, tm, c), lambda bi, i: (bi, i, 0))
    prev = lambda c: pl.BlockSpec((None, 8, c), lambda bi, i: (bi, jnp.maximum(i * (tm // 8) - 1, 0), 0))
    full = lambda a: pl.BlockSpec(a.shape, lambda bi, i: (0,) * a.ndim)
    return pl.pallas_call(
        functools.partial(_rwprep_kernel, width=width),
        grid=(b, s // tm),
        in_specs=[cur(w3), prev(w3), cur(nlo), prev(nlo), full(mu_rw), full(mu_lo), full(w0), full(a0),
                  full(k_k), full(k_a), full(wup), full(aup), full(gup)],
        out_specs=[cur(width)] * 7,
        out_shape=[jax.ShapeDtypeStruct((b, s, width), F32)] * 7,
        compiler_params=pltpu.CompilerParams(dimension_semantics=("arbitrary",) * 2,
                                             vmem_limit_bytes=VMEM_LIMIT),
        name="rwprep",
    )(rw3, rw3, lo3, lo3, mu_rw, mu_lo, w0, a0, k_k, k_a, wup, aup, gup)


def _rwscan_kernel(r_ref, lw_ref, k_ref, v_ref, a_ref, b_ref, g_ref, rk_ref, lnw_ref, lnb_ref,
                   o_ref, st_ref, y_ref, *, tb, width):
    t = RW_CHUNK
    n = LANES
    m = MXU_DIM
    groups = width // m
    pairs = width // n

    @pl.when(pl.program_id(1) == 0)
    def _():
        st_ref[...] = jnp.zeros_like(st_ref)

    ri = lax.broadcasted_iota(jnp.int32, (2 * n, 2 * n), 0)
    ci = lax.broadcasted_iota(jnp.int32, (2 * n, 2 * n), 1)
    same = ((ri % n) // HEAD_DIM) == ((ci % n) // HEAD_DIM)
    gmask = same & ((ci % HEAD_DIM) < (ri % HEAD_DIM) + jnp.where(ri >= n, 1, 0))
    rn = lax.broadcasted_iota(jnp.int32, (n, n), 0)
    cn = lax.broadcasted_iota(jnp.int32, (n, n), 1)
    bdmask = jnp.where((rn // HEAD_DIM) == (cn // HEAD_DIM), 1.0, 0.0)
    eye = jnp.where(rn == cn, 1.0, 0.0)
    ti = lax.broadcasted_iota(jnp.int32, (t, 4 * t), 0)
    si = lax.broadcasted_iota(jnp.int32, (t, 4 * t), 1)
    tri3 = jnp.where((si % t <= ti) & (si < 3 * t), 1.0, 0.0).astype(BF16)
    right_half = jnp.where(lax.broadcasted_iota(jnp.int32, (n, 2 * n), 1) >= n, 1.0, 0.0)
    cat = jnp.concatenate

    def blockdiag32(x):
        return cat([x, x], axis=0) * bdmask

    def prepare(rows, cols):
        r = r_ref[rows, cols]
        lw = lw_ref[rows, cols]
        k = k_ref[rows, cols]
        v = v_ref[rows, cols]
        a = a_ref[rows, cols]
        b = b_ref[rows, cols]

        l1, l2, l3 = _split3(lw)
        cum = _mm(tri3, cat([l1, l2, l3, l1], axis=0))
        yield
        cum_end = cum[t - 1:t, :]
        p_inv = jnp.exp(-cum)
        p_rest = jnp.exp(cum_end - cum)
        r32 = blockdiag32(r * jnp.exp(cum))
        r_bd = r32.astype(BF16)
        a_bd = blockdiag32(a * jnp.exp(cum - lw)).astype(BF16)
        k_bd = blockdiag32(k * p_inv).astype(BF16)
        b_bd = blockdiag32(b * p_inv).astype(BF16)
        kd_bd = blockdiag32(k * p_rest).astype(BF16)
        bd_bd = blockdiag32(b * p_rest).astype(BF16)
        v32 = blockdiag32(v)
        v_bd = v32.astype(BF16)
        zero_v = (cat([v32, v32], axis=1) * right_half).astype(BF16)

        g = jnp.where(gmask, _nt(cat([a_bd, r_bd], axis=0), cat([b_bd, k_bd], axis=0)), 0.0)
        yield
        aab = g[:n, :n]
        aak = g[:n, n:].astype(BF16)
        brb_brk = g[n:, :].astype(BF16)

        x = eye + aab
        p = aab.astype(BF16)
        p = _mm(p, p).astype(BF16)
        av = _mm(aak, v_bd).astype(BF16)
        yield
        steps = (t - 1).bit_length() - 1
        for i in range(steps):
            if i + 1 < steps:
                px_pp = _mm(p, cat([x.astype(BF16), p], axis=1))
                x = x + px_pp[:, :n]
                p = px_pp[:, n:].astype(BF16)
            else:
                x = x + _mm(p, x.astype(BF16))
            yield
        tinv = x.astype(BF16)

        w_u0 = _mm(tinv, cat([a_bd, av], axis=1)).astype(BF16)
        yield
        stack = cat([w_u0, zero_v], axis=0)
        q_y0 = _mm(brb_brk, stack)
        m_c0 = _tn(cat([bd_bd, kd_bd], axis=0), stack)
        yield
        q_mat = (r32 + q_y0[:, :n]).astype(BF16)
        m_mat = (eye * jnp.exp(cum_end) + m_c0[:, :n]).astype(BF16)
        lhs = cat([cat([q_mat, q_mat], axis=1), cat([m_mat, m_mat], axis=1)], axis=0)
        return lhs, q_y0[:, n:], m_c0[:, n:]

    def lockstep(generators):
        results = [None] * len(generators)
        live = list(enumerate(generators))
        while live:
            still = []
            for idx, gen in live:
                try:
                    next(gen)
                    still.append((idx, gen))
                except StopIteration as done:
                    results[idx] = done.value
            live = still
        return results

    lane_cols = [slice(pr * n, (pr + 1) * n) for pr in range(pairs)]
    time_rows = [slice(ch * t, (ch + 1) * t) for ch in range(tb // t)]
    prepared = lockstep([prepare(rows, cols) for rows in time_rows for cols in lane_cols])
    states = [st_ref[pr] for pr in range(pairs)]
    for ch, rows in enumerate(time_rows):
        for pr, cols in enumerate(lane_cols):
            lhs, y0, c0 = prepared[ch * pairs + pr]
            y_s = _mm(lhs, cat(_split2(states[pr]), axis=0))
            y = y_s[:n] + y0
            y_ref[rows, cols] = y[:t] + y[t:]
            states[pr] = y_s[n:] + c0
    for pr in range(pairs):
        st_ref[pr] = states[pr]

    ones = _head_ones(m)
    inv = 1.0 / HEAD_DIM
    for grp in range(groups):
        cols = slice(grp * m, (grp + 1) * m)
        y = y_ref[:, cols]
        mean = _mm_exact_rhs(y, ones) * inv
        d = y - mean
        var = _mm_exact_rhs(d * d, ones) * inv
        yn = d * lax.rsqrt(var + LN_X_EPS) * lnw_ref[:, cols] + lnb_ref[:, cols]
        bonus = _mm_exact_rhs(r_ref[:, cols] * k_ref[:, cols] * rk_ref[:, cols], ones) * v_ref[:, cols]
        o_ref[:, cols] = (yn + bonus) * g_ref[:, cols]


def _rwscan(r, lw, k, v, a, b_, g, r_k, ln_w, ln_b, tb):
    bsz, s, width = r.shape
    cur = pl.BlockSpec((None, tb, width), lambda bi, i: (bi, i, 0))
    full = lambda x: pl.BlockSpec(x.shape, lambda bi, i: (0,) * x.ndim)
    return pl.pallas_call(
        functools.partial(_rwscan_kernel, tb=tb, width=width),
        grid=(bsz, s // tb),
        in_specs=[cur] * 7 + [full(r_k), full(ln_w), full(ln_b)],
        out_specs=cur,
        out_shape=jax.ShapeDtypeStruct((bsz, s, width), F32),
        scratch_shapes=[pltpu.VMEM((width // LANES, LANES, LANES), F32),
                        pltpu.VMEM((tb, width), F32)],
        compiler_params=pltpu.CompilerParams(dimension_semantics=("arbitrary",) * 2,
                                             vmem_limit_bytes=VMEM_LIMIT),
        name="rwscan",
    )(r, lw, k, v, a, b_, g, r_k, ln_w, ln_b)


def _outproj_kernel(sb_ref, rw_ref, x_ref, wsb_ref, wrw_ref, g_ref, wr_ref,
                    x1_ref, h2_ref, meta_ref, cnt_ref, run_ref, *, tm):
    @pl.when(pl.program_id(0) == 0)
    def _():
        run_ref[...] = jnp.zeros_like(run_ref)

    x1 = x_ref[...] + _mm(sb_ref[...].astype(BF16), wsb_ref[...]) + _mm(rw_ref[...].astype(BF16), wrw_ref[...])
    x1_ref[...] = x1
    ms = jnp.mean(x1 * x1, axis=-1, keepdims=True)
    h2 = x1 * lax.rsqrt(ms + NORM_EPS) * g_ref[...]
    h2_ref[...] = h2

    h_hi, h_lo = _split2(h2)
    w_hi, w_lo = _split2(wr_ref[...])
    lg = _mm(h_hi, w_hi) + _mm(h_hi, w_lo) + _mm(h_lo, w_hi)

    lane = lax.broadcasted_iota(jnp.int32, (tm, LANES), 1).astype(F32)
    neg = -jnp.inf
    big = float(LANES)
    is_group = lane < N_GROUPS
    gl = jnp.where(is_group, lg, neg)
    gmax = jnp.max(gl, axis=1, keepdims=True)
    gidx = jnp.min(jnp.where(gl == gmax, lane, big), axis=1, keepdims=True)
    group_gate = 1.0 / jnp.sum(jnp.where(is_group, jnp.exp(lg - gmax), 0.0), axis=1, keepdims=True)
    lo_lane = N_GROUPS + EXPERTS_PER_GROUP * gidx
    el = jnp.where((lane >= lo_lane) & (lane < lo_lane + EXPERTS_PER_GROUP), lg, neg)
    m1 = jnp.max(el, axis=1, keepdims=True)
    i1 = jnp.min(jnp.where(el == m1, lane, big), axis=1, keepdims=True)
    el2 = jnp.where(lane == i1, neg, el)
    m2 = jnp.max(el2, axis=1, keepdims=True)
    i2 = jnp.min(jnp.where(el2 == m2, lane, big), axis=1, keepdims=True)
    p2 = jnp.exp(m2 - m1)
    gate1 = group_gate / (1.0 + p2)
    gate2 = group_gate * p2 / (1.0 + p2)

    hit1 = lane == i1
    hit2 = lane == i2
    onehot = jnp.where(hit1 | hit2, 1.0, 0.0)
    rr = lax.broadcasted_iota(jnp.int32, (tm, tm), 0)
    cc = lax.broadcasted_iota(jnp.int32, (tm, tm), 1)
    below = jnp.where(cc < rr, 1.0, 0.0).astype(BF16)
    before = run_ref[...] + _mm(below, onehot.astype(BF16))
    rank1 = jnp.sum(jnp.where(hit1, before, 0.0), axis=1, keepdims=True)
    rank2 = jnp.sum(jnp.where(hit2, before, 0.0), axis=1, keepdims=True)
    run = run_ref[...] + jnp.sum(onehot, axis=0, keepdims=True)
    run_ref[...] = run
    cnt_ref[...] = run

    vals = (i1 - N_GROUPS, i2 - N_GROUPS, gate1, gate2, rank1, rank2)
    meta = jnp.zeros((tm, LANES), F32)
    for pos, val in enumerate(vals):
        meta = jnp.where(lane == pos, val.astype(F32), meta)
    meta_ref[...] = meta


def _outproj(sb_out, rw_out, x2, w_sb, w_rw, gain, w_router, tm):
    n, d = x2.shape
    rows = lambda c: pl.BlockSpec((tm, c), lambda i: (i, 0))
    full = lambda a: pl.BlockSpec(a.shape, lambda i: (0,) * a.ndim)
    return pl.pallas_call(
        functools.partial(_outproj_kernel, tm=tm),
        grid=(n // tm,),
        in_specs=[rows(sb_out.shape[1]), rows(rw_out.shape[1]), rows(d), full(w_sb), full(w_rw), full(gain),
                  full(w_router)],
        out_specs=[rows(d), rows(d), rows(LANES), pl.BlockSpec((1, LANES), lambda i: (0, 0))],
        out_shape=[jax.ShapeDtypeStruct((n, d), F32), jax.ShapeDtypeStruct((n, d), F32),
                   jax.ShapeDtypeStruct((n, LANES), F32), jax.ShapeDtypeStruct((1, LANES), F32)],
        scratch_shapes=[pltpu.VMEM((1, LANES), F32)],
        compiler_params=pltpu.CompilerParams(dimension_semantics=("arbitrary",),
                                             vmem_limit_bytes=VMEM_LIMIT),
        name="outproj",
    )(sb_out, rw_out, x2, w_sb, w_rw, gain, w_router)


def _row_copy(src_ref, src_row, dst_ref, dst_row, sem):
    return pltpu.make_async_copy(src_ref.at[pl.ds(src_row, 1)], dst_ref.at[pl.ds(dst_row, 1)], sem)


def _start_row_gather(src_ref, idx_ref, dst_ref, sem, n_rows):
    def issue(r, _):
        _row_copy(src_ref, idx_ref[0, r], dst_ref, r, sem).start()
        return 0

    lax.fori_loop(0, n_rows, issue, 0, unroll=8)


def _wait_row_gather(src_ref, dst_ref, sem, n_rows):
    def drain(r, _):
        _row_copy(src_ref, 0, dst_ref, 0, sem).wait()
        return 0

    lax.fori_loop(0, n_rows, drain, 0, unroll=8)


def _experts_kernel(eid_ref, used_ref, cur_ref, nxt_ref, h_ref, wg_ref, wu_ref, wd_ref, o_ref,
                    x_ref, wgb_ref, wub_ref, wdb_ref, sems):
    i = pl.program_id(0)
    used = used_ref[0]
    slot = i % 2
    rows = x_ref.shape[1]

    @pl.when(i == 0)
    def _():
        _start_row_gather(h_ref, cur_ref, x_ref.at[0], sems.at[0], rows)

    @pl.when(i + 1 < used)
    def _():
        _start_row_gather(h_ref, nxt_ref, x_ref.at[1 - slot], sems.at[1 - slot], rows)

    @pl.when((i == 0) | (eid_ref[i] != eid_ref[jnp.maximum(i - 1, 0)]))
    def _():
        wgb_ref[...] = wg_ref[...].astype(BF16)
        wub_ref[...] = wu_ref[...].astype(BF16)
        wdb_ref[...] = wd_ref[...].astype(BF16)

    @pl.when(i < used)
    def _():
        _wait_row_gather(h_ref, x_ref.at[slot], sems.at[slot], rows)
        xb = x_ref[slot].astype(BF16)
        hg = _mm(xb, wgb_ref[...])
        hu = _mm(xb, wub_ref[...])
        act = hg * _sigmoid(hg) * hu
        o_ref[...] = _mm(act.astype(BF16), wdb_ref[...])

    @pl.when(i >= used)
    def _():
        o_ref[...] = jnp.zeros_like(o_ref)


def _experts(blk_eid, n_used, src3, h2, w_g, w_u, w_d):
    n_blocks = src3.shape[0]
    d = h2.shape[1]
    de = w_g.shape[2]
    w_map = lambda i, eid, used: (eid[i], 0, 0)
    idx_spec = lambda ahead: pl.BlockSpec(
        (None, 1, EXPERT_ROWS), lambda i, eid, used: (jnp.minimum(i + ahead, used[0] - 1), 0, 0),
        memory_space=pltpu.SMEM)
    grid_spec = pltpu.PrefetchScalarGridSpec(
        num_scalar_prefetch=2,
        grid=(n_blocks,),
        in_specs=[idx_spec(0), idx_spec(1), pl.BlockSpec(memory_space=pl.ANY),
                  pl.BlockSpec((None, d, de), w_map), pl.BlockSpec((None, d, de), w_map),
                  pl.BlockSpec((None, de, d), w_map)],
        out_specs=pl.BlockSpec((EXPERT_ROWS, d), lambda i, eid, used: (i, 0)),
        scratch_shapes=[pltpu.VMEM((2, EXPERT_ROWS, d), F32),
                        pltpu.VMEM((d, de), BF16), pltpu.VMEM((d, de), BF16), pltpu.VMEM((de, d), BF16),
                        pltpu.SemaphoreType.DMA((2,))],
    )
    return pl.pallas_call(
        _experts_kernel,
        grid_spec=grid_spec,
        out_shape=jax.ShapeDtypeStruct((n_blocks * EXPERT_ROWS, d), F32),
        compiler_params=pltpu.CompilerParams(dimension_semantics=("arbitrary",),
                                             vmem_limit_bytes=VMEM_LIMIT),
        name="experts",
    )(blk_eid, n_used, src3, src3, h2, w_g, w_u, w_d)


def _combine_kernel(dest_ref, rows_ref, x1_ref, meta_ref, o_ref, got_ref, sem, *, tm):
    def issue(t, _):
        for kk in range(TOP_K):
            _row_copy(rows_ref, dest_ref[0, TOP_K * t + kk], got_ref.at[kk], t, sem).start(priority=kk)
        return 0

    lax.fori_loop(0, tm, issue, 0, unroll=4)
    for kk in range(TOP_K):
        _wait_row_gather(rows_ref, got_ref.at[kk], sem, tm)
    meta = meta_ref[...]
    o_ref[...] = x1_ref[...] + got_ref[0] * meta[:, 2:3] + got_ref[1] * meta[:, 3:4]


def _combine(dest3, exp_out, x1, meta, tm):
    n, d = x1.shape
    rows = lambda c: pl.BlockSpec((tm, c), lambda i: (i, 0))
    return pl.pallas_call(
        functools.partial(_combine_kernel, tm=tm),
        grid=(n // tm,),
        in_specs=[pl.BlockSpec((None, 1, TOP_K * tm), lambda i: (i, 0, 0), memory_space=pltpu.SMEM),
                  pl.BlockSpec(memory_space=pl.ANY), rows(d), rows(LANES)],
        out_specs=rows(d),
        out_shape=jax.ShapeDtypeStruct((n, d), F32),
        scratch_shapes=[pltpu.VMEM((TOP_K, tm, d), F32), pltpu.SemaphoreType.DMA(())],
        compiler_params=pltpu.CompilerParams(dimension_semantics=("arbitrary",),
                                             vmem_limit_bytes=VMEM_LIMIT),
        name="combine",
    )(dest3, exp_out, x1, meta)


def _layer(x, norm1_gain, w_in, sb_q_gain, sb_k_gain, rw_shift_mu, rw_w0, rw_w_up, rw_a0, rw_a_up, rw_g_up,
           rw_k_k, rw_k_a, rw_r_k, rw_ln_w, rw_ln_b, w_out, norm2_gain, w_router_group, w_router_expert,
           w_exp_gate, w_exp_up, w_exp_down):
    bsz, s, d = x.shape
    n = bsz * s
    sbw = d // 2
    rww = d - sbw
    n_decay, n_aaa = rw_w_up.shape[0], rw_a_up.shape[0]
    n_lora = n_decay + n_aaa + rw_g_up.shape[0]
    tm = min(256, n)
    row = lambda a: a.reshape(1, -1).astype(F32)

    x2 = x.reshape(n, d)
    w_bf = w_in.astype(BF16)
    sb, rw, lo = _inproj(x2, row(norm1_gain), w_bf[:, :3 * sbw], w_bf[:, 3 * sbw:3 * sbw + 3 * rww],
                         w_bf[:, 3 * sbw + 3 * rww:], tm)
    heads = lambda gvec: jnp.tile(row(gvec), (1, sbw // HEAD_DIM))
    sb_out = _sbattn(sb.reshape(bsz, s, 3 * sbw), heads(sb_q_gain), heads(sb_k_gain), min(128, s))

    pad_rows = lambda w_up, start: jnp.zeros((n_lora, rww), BF16).at[start:start + w_up.shape[0]].set(
        w_up.astype(BF16))
    mu = row(rw_shift_mu)
    prep = _rwprep(rw.reshape(bsz, s, 3 * rww), lo.reshape(bsz, s, n_lora), mu[:, :3 * rww], mu[:, 3 * rww:],
                   row(rw_w0), row(rw_a0), row(rw_k_k), row(rw_k_a),
                   pad_rows(rw_w_up, 0), pad_rows(rw_a_up, n_decay), pad_rows(rw_g_up, n_decay + n_aaa),
                   min(256, s))
    rw_out = _rwscan(*prep, row(rw_r_k), row(rw_ln_w), row(rw_ln_b), min(256, s))

    w_out_bf = w_out.astype(BF16)
    w_router = jnp.zeros((d, LANES), F32).at[:, :N_GROUPS].set(w_router_group).at[
        :, N_GROUPS:N_GROUPS + N_EXPERTS].set(w_router_expert)
    x1, h2, meta, counts = _outproj(sb_out.reshape(n, sbw), rw_out.reshape(n, rww), x2, w_out_bf[:sbw],
                                    w_out_bf[sbw:], row(norm2_gain), w_router, tm)

    counts = counts[0, N_GROUPS:N_GROUPS + N_EXPERTS].astype(jnp.int32)
    padded = (counts + EXPERT_ROWS - 1) // EXPERT_ROWS * EXPERT_ROWS
    pad_end = jnp.cumsum(padded)
    pad_start = pad_end - padded
    eid = meta[:, 0:TOP_K].astype(jnp.int32)
    rank = meta[:, 4:4 + TOP_K].astype(jnp.int32)
    dest = pad_start[eid] + rank
    dest3 = dest.reshape(n // tm, 1, TOP_K * tm)
    n_rows = n * TOP_K + N_EXPERTS * EXPERT_ROWS
    n_blocks = n_rows // EXPERT_ROWS
    block_start = jnp.arange(n_blocks, dtype=jnp.int32) * EXPERT_ROWS
    blk_eid = jnp.minimum(jnp.sum((pad_end[None, :] <= block_start[:, None]).astype(jnp.int32), axis=1),
                          N_EXPERTS - 1)
    n_used = (pad_end[-1:] // EXPERT_ROWS).astype(jnp.int32)

    src3 = jnp.zeros((n_rows,), jnp.int32).at[dest.reshape(-1)].set(
        jnp.arange(n * TOP_K, dtype=jnp.int32) // TOP_K).reshape(n_blocks, 1, EXPERT_ROWS)

    exp_out = _experts(blk_eid, n_used, src3, h2, w_exp_gate, w_exp_up, w_exp_down)
    return _combine(dest3, exp_out, x1, meta, tm).reshape(bsz, s, d)


def kernel(x, norm1_gain, w_in, sb_q_gain, sb_k_gain, rw_shift_mu, rw_w0, rw_w_up, rw_a0, rw_a_up, rw_g_up,
           rw_k_k, rw_k_a, rw_r_k, rw_ln_w, rw_ln_b, w_out, norm2_gain, w_router_group, w_router_expert,
           w_exp_gate, w_exp_up, w_exp_down):
    params = (norm1_gain, w_in, sb_q_gain, sb_k_gain, rw_shift_mu, rw_w0, rw_w_up, rw_a0, rw_a_up, rw_g_up,
              rw_k_k, rw_k_a, rw_r_k, rw_ln_w, rw_ln_b, w_out, norm2_gain, w_router_group, w_router_expert,
              w_exp_gate, w_exp_up, w_exp_down)
    for layer in range(norm1_gain.shape[0]):
        x = _layer(x, *(p[layer] for p in params))
    return x
```

```python
import functools

import jax
import jax.numpy as jnp
from jax import lax
from jax.experimental import pallas as pl
from jax.experimental.pallas import tpu as pltpu

F32 = jnp.float32
BF16 = jnp.bfloat16

HEAD_DIM = 64
NORM_EPS = 1e-6
LN_X_EPS = 64e-5
N_GROUPS = 4
EXPERTS_PER_GROUP = 8
N_EXPERTS = N_GROUPS * EXPERTS_PER_GROUP
TOP_K = 2

LANES = 128
SUBLANES = 8
MXU_DIM = 256
RW_CHUNK = 64
EXPERT_ROWS = 256
UNDERFLOW_LOG = -110.0
VMEM_LIMIT = 56 * 1024 * 1024


def _nt(a, b):
    return lax.dot_general(a, b, (((1,), (1,)), ((), ())), preferred_element_type=F32)


def _tn(a, b):
    return lax.dot_general(a, b, (((0,), (0,)), ((), ())), preferred_element_type=F32)


def _mm(a, b):
    return jnp.dot(a, b, preferred_element_type=F32)


def _split2(x):
    hi = x.astype(BF16)
    lo = (x - hi.astype(F32)).astype(BF16)
    return hi, lo


def _split3(x):
    h1 = x.astype(BF16)
    r1 = x - h1.astype(F32)
    h2 = r1.astype(BF16)
    h3 = (r1 - h2.astype(F32)).astype(BF16)
    return h1, h2, h3


def _mm_exact_rhs(x, m):
    hi, lo = _split2(x)
    if 2 * x.shape[1] <= MXU_DIM:
        return _mm(jnp.concatenate([hi, lo], axis=1), jnp.concatenate([m, m], axis=0))
    return _mm(hi, m) + _mm(lo, m)


def _head_ones(width):
    r = lax.broadcasted_iota(jnp.int32, (width, width), 0) // HEAD_DIM
    c = lax.broadcasted_iota(jnp.int32, (width, width), 1) // HEAD_DIM
    return jnp.where(r == c, 1.0, 0.0).astype(BF16)


def _softplus(z):
    return jnp.maximum(z, 0.0) + jnp.log(1.0 + jnp.exp(-jnp.abs(z)))


def _sigmoid(z):
    return 1.0 / (1.0 + jnp.exp(-z))


def _store_row_tiles(ref, base, x):
    pieces = x.shape[1] // LANES
    for s in range(pieces):
        ref[pl.ds(base + s, x.shape[0], stride=pieces), :] = x[:, s * LANES:(s + 1) * LANES]


def _load_row_tiles(ref, base, rows, pieces):
    return jnp.concatenate([ref[pl.ds(base + s, rows, stride=pieces), :] for s in range(pieces)], axis=1)


def _tile_copy(src_ref, src_row, dst_ref, dst_row, sem):
    src = src_ref.at[pl.ds(pl.multiple_of(src_row * SUBLANES, SUBLANES), SUBLANES)]
    dst = dst_ref.at[pl.ds(pl.multiple_of(dst_row * SUBLANES, SUBLANES), SUBLANES)]
    return pltpu.make_async_copy(src, dst, sem)


def _inproj_kernel(x_ref, g_ref, wsb_ref, wrw_ref, wlo_ref, sb_ref, rw_ref, lo_ref):
    x = x_ref[...]
    ms = jnp.mean(x * x, axis=-1, keepdims=True)
    h = (x * lax.rsqrt(ms + NORM_EPS) * g_ref[...]).astype(BF16)
    sb_ref[...] = _mm(h, wsb_ref[...])
    rw_ref[...] = _mm(h, wrw_ref[...])
    lo_ref[...] = _mm(h, wlo_ref[...])


def _inproj(x2, gain, w_sb, w_rw, w_lo, tm):
    n, d = x2.shape
    full = lambda a: pl.BlockSpec(a.shape, lambda i: (0,) * a.ndim)
    rows = lambda c: pl.BlockSpec((tm, c), lambda i: (i, 0))
    return pl.pallas_call(
        _inproj_kernel,
        grid=(n // tm,),
        in_specs=[rows(d), full(gain), full(w_sb), full(w_rw), full(w_lo)],
        out_specs=[rows(w_sb.shape[1]), rows(w_rw.shape[1]), rows(w_lo.shape[1])],
        out_shape=[jax.ShapeDtypeStruct((n, w.shape[1]), F32) for w in (w_sb, w_rw, w_lo)],
        compiler_params=pltpu.CompilerParams(dimension_semantics=("arbitrary",),
                                             vmem_limit_bytes=VMEM_LIMIT),
        name="inproj",
    )(x2, gain, w_sb, w_rw, w_lo)


def _head_rms_norm(x, gain, ones):
    ms = _mm_exact_rhs(x * x, ones) * (1.0 / HEAD_DIM)
    return x * lax.rsqrt(ms + NORM_EPS) * gain


def _sbattn_kernel(q_ref, k_ref, v_ref, gq_ref, gk_ref, o_ref, kn_ref, vb_ref, *, blk, pairs):
    qb = pl.program_id(1)
    ones = _head_ones(pairs * LANES)

    @pl.when(qb == 0)
    def _():
        kn_ref[...] = _head_rms_norm(k_ref[...], gk_ref[...], ones).astype(BF16)
        vb_ref[...] = v_ref[...].astype(BF16)

    lane = lax.broadcasted_iota(jnp.int32, (1, pairs * LANES), 1)
    first = (lane % LANES) < HEAD_DIM
    qn = _head_rms_norm(q_ref[...], gq_ref[...], ones) * (HEAD_DIM ** -0.5)
    q_first = jnp.where(first, qn, 0.0).astype(BF16)
    q_second = jnp.where(first, 0.0, qn).astype(BF16)
    lanes_of = lambda p: slice(p * LANES, (p + 1) * LANES)
    qq = [jnp.concatenate([q_first[:, lanes_of(p)], q_second[:, lanes_of(p)]], axis=0) for p in range(pairs)]

    jj = lax.broadcasted_iota(jnp.int32, (blk, blk + LANES), 0)
    ss = lax.broadcasted_iota(jnp.int32, (blk, blk + LANES), 1)
    uu = jnp.where((jj > ss) | (ss >= blk), 1.0, 0.0).astype(BF16)

    rows = 2 * blk
    tt = lax.broadcasted_iota(jnp.int32, (pairs * rows, blk), 0) % blk
    sk = lax.broadcasted_iota(jnp.int32, (pairs * rows, blk), 1)
    causal = sk < tt

    def step(j, c, acc, diagonal):
        start = pl.multiple_of(j * blk, blk)
        z = jnp.concatenate([_nt(qq[p], kn_ref[pl.ds(start, blk), lanes_of(p)]) for p in range(pairs)], axis=0)
        lk = -_softplus(z)
        if diagonal:
            lk = jnp.where(causal, lk, 0.0)
        sums = _mm_exact_rhs(lk, uu)
        later = jnp.concatenate([c] * (blk // LANES), axis=1)
        w = jnp.exp(z + lk + sums[:, :blk] + later)
        if diagonal:
            w = jnp.where(causal, w, 0.0)
        w = w.astype(BF16)
        pv = jnp.concatenate([_mm(w[p * rows:(p + 1) * rows], vb_ref[pl.ds(start, blk), lanes_of(p)])
                              for p in range(pairs)], axis=0)
        c = c + sums[:, blk:]
        return c, acc + pv, jnp.max(c) > UNDERFLOW_LOG

    zero = jnp.zeros((pairs * rows, LANES), F32)
    carry = (qb - 1,) + step(qb, zero, zero, True)
    _, _, acc, _ = lax.while_loop(lambda cr: (cr[0] >= 0) & cr[3],
                                  lambda cr: (cr[0] - 1,) + step(cr[0], cr[1], cr[2], False), carry)
    first_pair = first[:, :LANES]
    for p in range(pairs):
        o_ref[:, lanes_of(p)] = jnp.where(first_pair, acc[p * rows:p * rows + blk], acc[p * rows + blk:(p + 1) * rows])


def _sbattn(sb3, gq, gk, blk):
    b, s, w3 = sb3.shape
    width = w3 // 3
    gspec = pl.BlockSpec((1, width), lambda bi, qb: (0, 0))
    return pl.pallas_call(
        functools.partial(_sbattn_kernel, blk=blk, pairs=width // LANES),
        grid=(b, s // blk),
        in_specs=[pl.BlockSpec((None, blk, width), lambda bi, qb: (bi, qb, 0)),
                  pl.BlockSpec((None, s, width), lambda bi, qb: (bi, 0, 1)),
                  pl.BlockSpec((None, s, width), lambda bi, qb: (bi, 0, 2)), gspec, gspec],
        out_specs=pl.BlockSpec((None, blk, width), lambda bi, qb: (bi, qb, 0)),
        out_shape=jax.ShapeDtypeStruct((b, s, width), F32),
        scratch_shapes=[pltpu.VMEM((s, width), BF16), pltpu.VMEM((s, width), BF16)],
        compiler_params=pltpu.CompilerParams(dimension_semantics=("arbitrary",) * 2,
                                             vmem_limit_bytes=VMEM_LIMIT),
        name="sbattn",
    )(sb3, sb3, sb3, gq, gk)


def _token_shift(p, prev_ref, mu, first_tile):
    rows = lax.broadcasted_iota(jnp.int32, (p.shape[0], 1), 0)
    last_prev = jnp.where(first_tile, 0.0, prev_ref[7:8, :])
    prev = jnp.where(rows == 0, last_prev, pltpu.roll(p, 1, axis=0))
    return p + (prev - p) * mu


def _rwprep_stage(rw_ref, rwp_ref, lo_ref, lop_ref, mur_ref, mul_ref, w0_ref, a0_ref, kk_ref, ka_ref,
                  wup_ref, aup_ref, gup_ref,
                  r_ref, lw_ref, k_ref, v_ref, a_ref, b_ref, g_ref, *, width):
    first_tile = pl.program_id(1) == 0
    pf = _token_shift(rw_ref[...], rwp_ref, mur_ref[...], first_tile)
    lf = _token_shift(lo_ref[...], lop_ref, mul_ref[...], first_tile)
    r = pf[:, :width]
    k = pf[:, width:2 * width]
    v = pf[:, 2 * width:]
    w = -_softplus(-(w0_ref[...] + _mm(jnp.tanh(lf).astype(BF16), wup_ref[...]))) - 0.5
    lr = _sigmoid(a0_ref[...] + _mm(lf.astype(BF16), aup_ref[...]))
    kk = k * kk_ref[...]
    ss = _mm_exact_rhs(kk * kk, _head_ones(width))
    kk = kk / jnp.maximum(jnp.sqrt(ss), 1e-12)
    r_ref[...] = r
    lw_ref[...] = -jnp.exp(w)
    k_ref[...] = k * (1.0 + (lr - 1.0) * ka_ref[...])
    v_ref[...] = v
    a_ref[...] = -kk
    b_ref[...] = kk * lr
    g_ref[...] = _mm(_sigmoid(lf).astype(BF16), gup_ref[...])


def _rwscan_kernel(rw_ref, rwp_ref, lo_ref, lop_ref, mur_ref, mul_ref, w0_ref, a0_ref, kk_ref, ka_ref,
                   wup_ref, aup_ref, gup_ref, rk_ref, lnw_ref, lnb_ref,
                   o_ref, st_ref, y_ref, r_ref, lw_ref, k_ref, v_ref, a_ref, b_ref, g_ref, *, tb, width):
    _rwprep_stage(rw_ref, rwp_ref, lo_ref, lop_ref, mur_ref, mul_ref, w0_ref, a0_ref, kk_ref, ka_ref,
                  wup_ref, aup_ref, gup_ref, r_ref, lw_ref, k_ref, v_ref, a_ref, b_ref, g_ref, width=width)

    t = RW_CHUNK
    n = LANES
    m = MXU_DIM
    groups = width // m
    pairs = width // n

    @pl.when(pl.program_id(1) == 0)
    def _():
        st_ref[...] = jnp.zeros_like(st_ref)

    ri = lax.broadcasted_iota(jnp.int32, (2 * n, 2 * n), 0)
    ci = lax.broadcasted_iota(jnp.int32, (2 * n, 2 * n), 1)
    same = ((ri % n) // HEAD_DIM) == ((ci % n) // HEAD_DIM)
    gmask = same & ((ci % HEAD_DIM) < (ri % HEAD_DIM) + jnp.where(ri >= n, 1, 0))
    rn = lax.broadcasted_iota(jnp.int32, (n, n), 0)
    cn = lax.broadcasted_iota(jnp.int32, (n, n), 1)
    bdmask = jnp.where((rn // HEAD_DIM) == (cn // HEAD_DIM), 1.0, 0.0)
    eye = jnp.where(rn == cn, 1.0, 0.0)
    ti = lax.broadcasted_iota(jnp.int32, (t, 4 * t), 0)
    si = lax.broadcasted_iota(jnp.int32, (t, 4 * t), 1)
    tri3 = jnp.where((si % t <= ti) & (si < 3 * t), 1.0, 0.0).astype(BF16)
    right_half = jnp.where(lax.broadcasted_iota(jnp.int32, (n, 2 * n), 1) >= n, 1.0, 0.0)
    cat = jnp.concatenate

    def blockdiag32(x):
        return cat([x, x], axis=0) * bdmask

    def prepare(rows, cols):
        r = r_ref[rows, cols]
        lw = lw_ref[rows, cols]
        k = k_ref[rows, cols]
        v = v_ref[rows, cols]
        a = a_ref[rows, cols]
        b = b_ref[rows, cols]

        l1, l2, l3 = _split3(lw)
        cum = _mm(tri3, cat([l1, l2, l3, l1], axis=0))
        yield
        cum_end = cum[t - 1:t, :]
        p_inv = jnp.exp(-cum)
        p_rest = jnp.exp(cum_end - cum)
        r32 = blockdiag32(r * jnp.exp(cum))
        r_bd = r32.astype(BF16)
        a_bd = blockdiag32(a * jnp.exp(cum - lw)).astype(BF16)
        k_bd = blockdiag32(k * p_inv).astype(BF16)
        b_bd = blockdiag32(b * p_inv).astype(BF16)
        kd_bd = blockdiag32(k * p_rest).astype(BF16)
        bd_bd = blockdiag32(b * p_rest).astype(BF16)
        v32 = blockdiag32(v)
        v_bd = v32.astype(BF16)
        zero_v = (cat([v32, v32], axis=1) * right_half).astype(BF16)

        g = jnp.where(gmask, _nt(cat([a_bd, r_bd], axis=0), cat([b_bd, k_bd], axis=0)), 0.0)
        yield
        aab = g[:n, :n]
        aak = g[:n, n:].astype(BF16)
        brb_brk = g[n:, :].astype(BF16)

        x = eye + aab
        p = aab.astype(BF16)
        p = _mm(p, p).astype(BF16)
        av = _mm(aak, v_bd).astype(BF16)
        yield
        steps = (t - 1).bit_length() - 1
        for i in range(steps):
            if i + 1 < steps:
                px_pp = _mm(p, cat([x.astype(BF16), p], axis=1))
                x = x + px_pp[:, :n]
                p = px_pp[:, n:].astype(BF16)
            else:
                x = x + _mm(p, x.astype(BF16))
            yield
        tinv = x.astype(BF16)

        w_u0 = _mm(tinv, cat([a_bd, av], axis=1)).astype(BF16)
        yield
        stack = cat([w_u0, zero_v], axis=0)
        q_y0 = _mm(brb_brk, stack)
        m_c0 = _tn(cat([bd_bd, kd_bd], axis=0), stack)
        yield
        q_mat = (r32 + q_y0[:, :n]).astype(BF16)
        m_mat = (eye * jnp.exp(cum_end) + m_c0[:, :n]).astype(BF16)
        lhs = cat([cat([q_mat, q_mat], axis=1), cat([m_mat, m_mat], axis=1)], axis=0)
        return lhs, q_y0[:, n:], m_c0[:, n:]

    def lockstep(generators):
        results = [None] * len(generators)
        live = list(enumerate(generators))
        while live:
            still = []
            for idx, gen in live:
                try:
                    next(gen)
                    still.append((idx, gen))
                except StopIteration as done:
                    results[idx] = done.value
            live = still
        return results

    lane_cols = [slice(pr * n, (pr + 1) * n) for pr in range(pairs)]
    time_rows = [slice(ch * t, (ch + 1) * t) for ch in range(tb // t)]
    prepared = lockstep([prepare(rows, cols) for rows in time_rows for cols in lane_cols])
    states = [st_ref[pr] for pr in range(pairs)]
    for ch, rows in enumerate(time_rows):
        for pr, cols in enumerate(lane_cols):
            lhs, y0, c0 = prepared[ch * pairs + pr]
            y_s = _mm(lhs, cat(_split2(states[pr]), axis=0))
            y = y_s[:n] + y0
            y_ref[rows, cols] = y[:t] + y[t:]
            states[pr] = y_s[n:] + c0
    for pr in range(pairs):
        st_ref[pr] = states[pr]

    ones = _head_ones(m)
    inv = 1.0 / HEAD_DIM
    for grp in range(groups):
        cols = slice(grp * m, (grp + 1) * m)
        y = y_ref[:, cols]
        mean = _mm_exact_rhs(y, ones) * inv
        d = y - mean
        var = _mm_exact_rhs(d * d, ones) * inv
        yn = d * lax.rsqrt(var + LN_X_EPS) * lnw_ref[:, cols] + lnb_ref[:, cols]
        bonus = _mm_exact_rhs(r_ref[:, cols] * k_ref[:, cols] * rk_ref[:, cols], ones) * v_ref[:, cols]
        o_ref[:, cols] = (yn + bonus) * g_ref[:, cols]


def _rwscan(rw3, lo3, mu_rw, mu_lo, w0, a0, k_k, k_a, wup, aup, gup, r_k, ln_w, ln_b, tb):
    bsz, s, w3 = rw3.shape
    width = w3 // 3
    nlo = lo3.shape[2]
    cur = lambda c: pl.BlockSpec((None, tb, c), lambda bi, i: (bi, i, 0))
    prev = lambda c: pl.BlockSpec((None, SUBLANES, c),
                                  lambda bi, i: (bi, jnp.maximum(i * (tb // SUBLANES) - 1, 0), 0))
    full = lambda x: pl.BlockSpec(x.shape, lambda bi, i: (0,) * x.ndim)
    params = (mu_rw, mu_lo, w0, a0, k_k, k_a, wup, aup, gup, r_k, ln_w, ln_b)
    return pl.pallas_call(
        functools.partial(_rwscan_kernel, tb=tb, width=width),
        grid=(bsz, s // tb),
        in_specs=[cur(w3), prev(w3), cur(nlo), prev(nlo)] + [full(p) for p in params],
        out_specs=cur(width),
        out_shape=jax.ShapeDtypeStruct((bsz, s, width), F32),
        scratch_shapes=[pltpu.VMEM((width // LANES, LANES, LANES), F32)] + [pltpu.VMEM((tb, width), F32)] * 8,
        compiler_params=pltpu.CompilerParams(dimension_semantics=("arbitrary",) * 2,
                                             vmem_limit_bytes=VMEM_LIMIT),
        name="rwscan",
    )(rw3, rw3, lo3, lo3, *params)


def _outproj_kernel(sb_ref, rw_ref, x_ref, wsb_ref, wrw_ref, g_ref, wr_ref,
                    x1_ref, h2_ref, meta_ref, cnt_ref, run_ref, *, tm):
    @pl.when(pl.program_id(0) == 0)
    def _():
        run_ref[...] = jnp.zeros_like(run_ref)

    x1 = x_ref[...] + _mm(sb_ref[...].astype(BF16), wsb_ref[...]) + _mm(rw_ref[...].astype(BF16), wrw_ref[...])
    x1_ref[...] = x1
    ms = jnp.mean(x1 * x1, axis=-1, keepdims=True)
    h2 = x1 * lax.rsqrt(ms + NORM_EPS) * g_ref[...]
    _store_row_tiles(h2_ref, 0, h2)

    h_hi, h_lo = _split2(h2)
    w_hi, w_lo = _split2(wr_ref[...])
    lg = _mm(h_hi, w_hi) + _mm(h_hi, w_lo) + _mm(h_lo, w_hi)

    lane = lax.broadcasted_iota(jnp.int32, (tm, LANES), 1).astype(F32)
    neg = -jnp.inf
    big = float(LANES)
    is_group = lane < N_GROUPS
    gl = jnp.where(is_group, lg, neg)
    gmax = jnp.max(gl, axis=1, keepdims=True)
    gidx = jnp.min(jnp.where(gl == gmax, lane, big), axis=1, keepdims=True)
    group_gate = 1.0 / jnp.sum(jnp.where(is_group, jnp.exp(lg - gmax), 0.0), axis=1, keepdims=True)
    lo_lane = N_GROUPS + EXPERTS_PER_GROUP * gidx
    el = jnp.where((lane >= lo_lane) & (lane < lo_lane + EXPERTS_PER_GROUP), lg, neg)
    m1 = jnp.max(el, axis=1, keepdims=True)
    i1 = jnp.min(jnp.where(el == m1, lane, big), axis=1, keepdims=True)
    el2 = jnp.where(lane == i1, neg, el)
    m2 = jnp.max(el2, axis=1, keepdims=True)
    i2 = jnp.min(jnp.where(el2 == m2, lane, big), axis=1, keepdims=True)
    p2 = jnp.exp(m2 - m1)
    gate1 = group_gate / (1.0 + p2)
    gate2 = group_gate * p2 / (1.0 + p2)

    hit1 = lane == i1
    hit2 = lane == i2
    onehot = jnp.where(hit1 | hit2, 1.0, 0.0)
    rr = lax.broadcasted_iota(jnp.int32, (tm, tm), 0)
    cc = lax.broadcasted_iota(jnp.int32, (tm, tm), 1)
    below = jnp.where(cc < rr, 1.0, 0.0).astype(BF16)
    before = run_ref[...] + _mm(below, onehot.astype(BF16))
    rank1 = jnp.sum(jnp.where(hit1, before, 0.0), axis=1, keepdims=True)
    rank2 = jnp.sum(jnp.where(hit2, before, 0.0), axis=1, keepdims=True)
    run = run_ref[...] + jnp.sum(onehot, axis=0, keepdims=True)
    run_ref[...] = run
    cnt_ref[...] = run

    vals = (i1 - N_GROUPS, i2 - N_GROUPS, gate1, gate2, rank1, rank2)
    meta = jnp.zeros((tm, LANES), F32)
    for pos, val in enumerate(vals):
        meta = jnp.where(lane == pos, val.astype(F32), meta)
    meta_ref[...] = meta


def _outproj(sb_out, rw_out, x2, w_sb, w_rw, gain, w_router, tm):
    n, d = x2.shape
    rows = lambda c: pl.BlockSpec((tm, c), lambda i: (i, 0))
    full = lambda a: pl.BlockSpec(a.shape, lambda i: (0,) * a.ndim)
    return pl.pallas_call(
        functools.partial(_outproj_kernel, tm=tm),
        grid=(n // tm,),
        in_specs=[rows(sb_out.shape[1]), rows(rw_out.shape[1]), rows(d), full(w_sb), full(w_rw), full(gain),
                  full(w_router)],
        out_specs=[rows(d), pl.BlockSpec((tm * (d // LANES), LANES), lambda i: (i, 0)), rows(LANES),
                   pl.BlockSpec((1, LANES), lambda i: (0, 0))],
        out_shape=[jax.ShapeDtypeStruct((n, d), F32), jax.ShapeDtypeStruct((n * (d // LANES), LANES), F32),
                   jax.ShapeDtypeStruct((n, LANES), F32), jax.ShapeDtypeStruct((1, LANES), F32)],
        scratch_shapes=[pltpu.VMEM((1, LANES), F32)],
        compiler_params=pltpu.CompilerParams(dimension_semantics=("arbitrary",),
                                             vmem_limit_bytes=VMEM_LIMIT),
        name="outproj",
    )(sb_out, rw_out, x2, w_sb, w_rw, gain, w_router)


def _start_tile_gather(src_ref, idx_ref, dst_ref, dst_base, sem, n_rows):
    def issue(r, _):
        _tile_copy(src_ref, idx_ref[0, r], dst_ref, dst_base + r, sem).start()
        return 0

    lax.fori_loop(0, n_rows, issue, 0, unroll=8)


def _wait_tile_gather(src_ref, dst_ref, sem, n_rows):
    def drain(r, _):
        _tile_copy(src_ref, 0, dst_ref, 0, sem).wait()
        return 0

    lax.fori_loop(0, n_rows, drain, 0, unroll=8)


def _srcmap_kernel(dest_ref, seg_ref, src_ref):
    def put(tok, _):
        for kk in range(TOP_K):
            src_ref[dest_ref[TOP_K * tok + kk]] = tok
        return 0

    lax.fori_loop(0, dest_ref.shape[0] // TOP_K, put, 0, unroll=8)

    def pad(e, _):
        first = seg_ref[0, e]

        def clear(r, _):
            src_ref[first + r] = 0
            return 0

        lax.fori_loop(seg_ref[1, e], seg_ref[2, e], clear, 0)
        return 0

    lax.fori_loop(0, seg_ref.shape[1], pad, 0)


def _srcmap(dest_flat, segments, n_rows):
    return pl.pallas_call(
        _srcmap_kernel,
        in_specs=[pl.BlockSpec(memory_space=pltpu.SMEM), pl.BlockSpec(memory_space=pltpu.SMEM)],
        out_specs=pl.BlockSpec(memory_space=pltpu.SMEM),
        out_shape=jax.ShapeDtypeStruct((n_rows,), jnp.int32),
        name="srcmap",
    )(dest_flat, segments)


def _experts_kernel(eid_ref, used_ref, cur_ref, nxt_ref, h_ref, wg_ref, wu_ref, wd_ref, o_ref,
                    x_ref, wgb_ref, wub_ref, wdb_ref, sems):
    i = pl.program_id(0)
    used = used_ref[0]
    slot = i % 2
    rows = EXPERT_ROWS

    @pl.when(i == 0)
    def _():
        _start_tile_gather(h_ref, cur_ref, x_ref, 0, sems.at[0], rows)

    @pl.when(i + 1 < used)
    def _():
        _start_tile_gather(h_ref, nxt_ref, x_ref, (1 - slot) * rows, sems.at[1 - slot], rows)

    @pl.when((i == 0) | (eid_ref[i] != eid_ref[jnp.maximum(i - 1, 0)]))
    def _():
        wgb_ref[...] = wg_ref[...].astype(BF16)
        wub_ref[...] = wu_ref[...].astype(BF16)
        wdb_ref[...] = wd_ref[...].astype(BF16)

    @pl.when(i < used)
    def _():
        _wait_tile_gather(h_ref, x_ref, sems.at[slot], rows)
        xb = _load_row_tiles(x_ref, slot * (rows * SUBLANES), rows, SUBLANES).astype(BF16)
        hg = _mm(xb, wgb_ref[...])
        hu = _mm(xb, wub_ref[...])
        act = hg * _sigmoid(hg) * hu
        _store_row_tiles(o_ref, 0, _mm(act.astype(BF16), wdb_ref[...]))

    @pl.when(i >= used)
    def _():
        o_ref[...] = jnp.zeros_like(o_ref)


def _experts(blk_eid, n_used, src3, h2_tiles, w_g, w_u, w_d):
    n_blocks = src3.shape[0]
    d, de = w_g.shape[1], w_g.shape[2]
    assert d == SUBLANES * LANES
    w_map = lambda i, eid, used: (eid[i], 0, 0)
    idx_spec = lambda ahead: pl.BlockSpec(
        (None, 1, EXPERT_ROWS), lambda i, eid, used: (jnp.minimum(i + ahead, used[0] - 1), 0, 0),
        memory_space=pltpu.SMEM)
    grid_spec = pltpu.PrefetchScalarGridSpec(
        num_scalar_prefetch=2,
        grid=(n_blocks,),
        in_specs=[idx_spec(0), idx_spec(1), pl.BlockSpec(memory_space=pl.ANY),
                  pl.BlockSpec((None, d, de), w_map), pl.BlockSpec((None, d, de), w_map),
                  pl.BlockSpec((None, de, d), w_map)],
        out_specs=pl.BlockSpec((EXPERT_ROWS * SUBLANES, LANES), lambda i, eid, used: (i, 0)),
        scratch_shapes=[pltpu.VMEM((2 * EXPERT_ROWS * SUBLANES, LANES), F32),
                        pltpu.VMEM((d, de), BF16), pltpu.VMEM((d, de), BF16), pltpu.VMEM((de, d), BF16),
                        pltpu.SemaphoreType.DMA((2,))],
    )
    return pl.pallas_call(
        _experts_kernel,
        grid_spec=grid_spec,
        out_shape=jax.ShapeDtypeStruct((n_blocks * EXPERT_ROWS * SUBLANES, LANES), F32),
        compiler_params=pltpu.CompilerParams(dimension_semantics=("arbitrary",),
                                             vmem_limit_bytes=VMEM_LIMIT),
        name="experts",
    )(blk_eid, n_used, src3, src3, h2_tiles, w_g, w_u, w_d)


def _combine_kernel(dest_ref, rows_ref, x1_ref, meta_ref, o_ref, got_ref, sem, *, tm):
    def issue(t, _):
        for kk in range(TOP_K):
            _tile_copy(rows_ref, dest_ref[0, TOP_K * t + kk], got_ref, kk * tm + t, sem).start(priority=kk)
        return 0

    lax.fori_loop(0, tm, issue, 0, unroll=4)
    _wait_tile_gather(rows_ref, got_ref, sem, TOP_K * tm)
    meta = meta_ref[...]
    y = x1_ref[...]
    for kk in range(TOP_K):
        y = y + _load_row_tiles(got_ref, kk * tm * SUBLANES, tm, SUBLANES) * meta[:, 2 + kk:3 + kk]
    o_ref[...] = y


def _combine(dest3, exp_out_tiles, x1, meta, tm):
    n, d = x1.shape
    rows = lambda c: pl.BlockSpec((tm, c), lambda i: (i, 0))
    return pl.pallas_call(
        functools.partial(_combine_kernel, tm=tm),
        grid=(n // tm,),
        in_specs=[pl.BlockSpec((None, 1, TOP_K * tm), lambda i: (i, 0, 0), memory_space=pltpu.SMEM),
                  pl.BlockSpec(memory_space=pl.ANY), rows(d), rows(LANES)],
        out_specs=rows(d),
        out_shape=jax.ShapeDtypeStruct((n, d), F32),
        scratch_shapes=[pltpu.VMEM((TOP_K * tm * SUBLANES, LANES), F32), pltpu.SemaphoreType.DMA(())],
        compiler_params=pltpu.CompilerParams(dimension_semantics=("arbitrary",),
                                             vmem_limit_bytes=VMEM_LIMIT),
        name="combine",
    )(dest3, exp_out_tiles, x1, meta)


def _layer(x, norm1_gain, w_in, sb_q_gain, sb_k_gain, rw_shift_mu, rw_w0, rw_w_up, rw_a0, rw_a_up, rw_g_up,
           rw_k_k, rw_k_a, rw_r_k, rw_ln_w, rw_ln_b, w_out, norm2_gain, w_router_group, w_router_expert,
           w_exp_gate, w_exp_up, w_exp_down):
    bsz, s, d = x.shape
    n = bsz * s
    sbw = d // 2
    rww = d - sbw
    n_decay, n_aaa = rw_w_up.shape[0], rw_a_up.shape[0]
    n_lora = n_decay + n_aaa + rw_g_up.shape[0]
    tm = min(256, n)
    row = lambda a: a.reshape(1, -1).astype(F32)

    x2 = x.reshape(n, d)
    w_bf = w_in.astype(BF16)
    sb, rw, lo = _inproj(x2, row(norm1_gain), w_bf[:, :3 * sbw], w_bf[:, 3 * sbw:3 * sbw + 3 * rww],
                         w_bf[:, 3 * sbw + 3 * rww:], tm)
    heads = lambda gvec: jnp.tile(row(gvec), (1, sbw // HEAD_DIM))
    sb_out = _sbattn(sb.reshape(bsz, s, 3 * sbw), heads(sb_q_gain), heads(sb_k_gain), min(128, s))

    pad_rows = lambda w_up, start: jnp.zeros((n_lora, rww), BF16).at[start:start + w_up.shape[0]].set(
        w_up.astype(BF16))
    mu = row(rw_shift_mu)
    rw_out = _rwscan(rw.reshape(bsz, s, 3 * rww), lo.reshape(bsz, s, n_lora), mu[:, :3 * rww], mu[:, 3 * rww:],
                     row(rw_w0), row(rw_a0), row(rw_k_k), row(rw_k_a),
                     pad_rows(rw_w_up, 0), pad_rows(rw_a_up, n_decay), pad_rows(rw_g_up, n_decay + n_aaa),
                     row(rw_r_k), row(rw_ln_w), row(rw_ln_b), min(256, s))

    w_out_bf = w_out.astype(BF16)
    w_router = jnp.zeros((d, LANES), F32).at[:, :N_GROUPS].set(w_router_group).at[
        :, N_GROUPS:N_GROUPS + N_EXPERTS].set(w_router_expert)
    x1, h2, meta, counts = _outproj(sb_out.reshape(n, sbw), rw_out.reshape(n, rww), x2, w_out_bf[:sbw],
                                    w_out_bf[sbw:], row(norm2_gain), w_router, tm)

    counts = counts[0, N_GROUPS:N_GROUPS + N_EXPERTS].astype(jnp.int32)
    padded = (counts + EXPERT_ROWS - 1) // EXPERT_ROWS * EXPERT_ROWS
    pad_end = jnp.cumsum(padded)
    pad_start = pad_end - padded
    eid = meta[:, 0:TOP_K].astype(jnp.int32)
    rank = meta[:, 4:4 + TOP_K].astype(jnp.int32)
    dest = pad_start[eid] + rank
    dest3 = dest.reshape(n // tm, 1, TOP_K * tm)
    n_rows = n * TOP_K + N_EXPERTS * EXPERT_ROWS
    n_blocks = n_rows // EXPERT_ROWS
    block_start = jnp.arange(n_blocks, dtype=jnp.int32) * EXPERT_ROWS
    blk_eid = jnp.minimum(jnp.sum((pad_end[None, :] <= block_start[:, None]).astype(jnp.int32), axis=1),
                          N_EXPERTS - 1)
    n_used = (pad_end[-1:] // EXPERT_ROWS).astype(jnp.int32)

    tail = jnp.zeros((1,), jnp.int32)
    segments = jnp.stack([jnp.concatenate([pad_start, pad_end[-1:]]), jnp.concatenate([counts, tail]),
                          jnp.concatenate([padded, n_rows - pad_end[-1:]])]).astype(jnp.int32)
    src3 = _srcmap(dest.reshape(-1), segments, n_rows).reshape(n_blocks, 1, EXPERT_ROWS)

    exp_out = _experts(blk_eid, n_used, src3, h2, w_exp_gate, w_exp_up, w_exp_down)
    return _combine(dest3, exp_out, x1, meta, tm).reshape(bsz, s, d)


def kernel(x, norm1_gain, w_in, sb_q_gain, sb_k_gain, rw_shift_mu, rw_w0, rw_w_up, rw_a0, rw_a_up, rw_g_up,
           rw_k_k, rw_k_a, rw_r_k, rw_ln_w, rw_ln_b, w_out, norm2_gain, w_router_group, w_router_expert,
           w_exp_gate, w_exp_up, w_exp_down):
    params = (norm1_gain, w_in, sb_q_gain, sb_k_gain, rw_shift_mu, rw_w0, rw_w_up, rw_a0, rw_a_up, rw_g_up,
              rw_k_k, rw_k_a, rw_r_k, rw_ln_w, rw_ln_b, w_out, norm2_gain, w_router_group, w_router_expert,
              w_exp_gate, w_exp_up, w_exp_down)
    for layer in range(norm1_gain.shape[0]):
        x = _layer(x, *(p[layer] for p in params))
    return x
```

```python
import functools

import jax
import jax.numpy as jnp
from jax import lax
from jax.experimental import pallas as pl
from jax.experimental.pallas import tpu as pltpu

F32 = jnp.float32
BF16 = jnp.bfloat16

HEAD_DIM = 64
NORM_EPS = 1e-6
LN_X_EPS = 64e-5
N_GROUPS = 4
EXPERTS_PER_GROUP = 8
N_EXPERTS = N_GROUPS * EXPERTS_PER_GROUP
TOP_K = 2

LANES = 128
SUBLANES = 8
MXU_DIM = 256
RW_CHUNK = 64
SB_BLOCK = 256
EXPERT_ROWS = 256
UNDERFLOW_LOG = -110.0
VMEM_LIMIT = 56 * 1024 * 1024


def _nt(a, b):
    return lax.dot_general(a, b, (((1,), (1,)), ((), ())), preferred_element_type=F32)


def _tn(a, b):
    return lax.dot_general(a, b, (((0,), (0,)), ((), ())), preferred_element_type=F32)


def _mm(a, b):
    return jnp.dot(a, b, preferred_element_type=F32)


def _split2(x):
    hi = x.astype(BF16)
    lo = (x - hi.astype(F32)).astype(BF16)
    return hi, lo


def _split3(x):
    h1 = x.astype(BF16)
    r1 = x - h1.astype(F32)
    h2 = r1.astype(BF16)
    h3 = (r1 - h2.astype(F32)).astype(BF16)
    return h1, h2, h3


def _mm_exact_rhs(x, m):
    hi, lo = _split2(x)
    if 2 * x.shape[1] <= MXU_DIM:
        return _mm(jnp.concatenate([hi, lo], axis=1), jnp.concatenate([m, m], axis=0))
    return _mm(hi, m) + _mm(lo, m)


def _head_ones(width):
    r = lax.broadcasted_iota(jnp.int32, (width, width), 0) // HEAD_DIM
    c = lax.broadcasted_iota(jnp.int32, (width, width), 1) // HEAD_DIM
    return jnp.where(r == c, 1.0, 0.0).astype(BF16)


def _softplus(z):
    return jnp.maximum(z, 0.0) + jnp.log(1.0 + jnp.exp(-jnp.abs(z)))


def _sigmoid(z):
    return 1.0 / (1.0 + jnp.exp(-z))


def _store_row_tiles(ref, base, x):
    pieces = x.shape[1] // LANES
    for s in range(pieces):
        ref[pl.ds(base + s, x.shape[0], stride=pieces), :] = x[:, s * LANES:(s + 1) * LANES]


def _load_row_tiles(ref, base, rows, pieces):
    return jnp.concatenate([ref[pl.ds(base + s, rows, stride=pieces), :] for s in range(pieces)], axis=1)


def _tile_copy(src_ref, src_row, dst_ref, dst_row, sem):
    src = src_ref.at[pl.ds(pl.multiple_of(src_row * SUBLANES, SUBLANES), SUBLANES)]
    dst = dst_ref.at[pl.ds(pl.multiple_of(dst_row * SUBLANES, SUBLANES), SUBLANES)]
    return pltpu.make_async_copy(src, dst, sem)


def _inproj_kernel(x_ref, g_ref, wsb_ref, wrw_ref, wlo_ref, sb_ref, rw_ref, lo_ref):
    x = x_ref[...]
    ms = jnp.mean(x * x, axis=-1, keepdims=True)
    h = (x * lax.rsqrt(ms + NORM_EPS) * g_ref[...]).astype(BF16)
    sb_ref[...] = _mm(h, wsb_ref[...])
    rw_ref[...] = _mm(h, wrw_ref[...])
    lo_ref[...] = _mm(h, wlo_ref[...])


def _inproj(x2, gain, w_sb, w_rw, w_lo, tm):
    n, d = x2.shape
    full = lambda a: pl.BlockSpec(a.shape, lambda i: (0,) * a.ndim)
    rows = lambda c: pl.BlockSpec((tm, c), lambda i: (i, 0))
    return pl.pallas_call(
        _inproj_kernel,
        grid=(n // tm,),
        in_specs=[rows(d), full(gain), full(w_sb), full(w_rw), full(w_lo)],
        out_specs=[rows(w_sb.shape[1]), rows(w_rw.shape[1]), rows(w_lo.shape[1])],
        out_shape=[jax.ShapeDtypeStruct((n, w.shape[1]), F32) for w in (w_sb, w_rw, w_lo)],
        compiler_params=pltpu.CompilerParams(dimension_semantics=("arbitrary",),
                                             vmem_limit_bytes=VMEM_LIMIT),
        name="inproj",
    )(x2, gain, w_sb, w_rw, w_lo)


def _head_rms_norm(x, gain, ones):
    ms = _mm_exact_rhs(x * x, ones) * (1.0 / HEAD_DIM)
    return x * lax.rsqrt(ms + NORM_EPS) * gain


def _sbattn_kernel(q_ref, k_ref, v_ref, gq_ref, gk_ref, o_ref, kn_ref, vb_ref, *, blk, pairs):
    qb = pl.program_id(1)
    ones = _head_ones(pairs * LANES)

    @pl.when(qb == 0)
    def _():
        kn_ref[...] = _head_rms_norm(k_ref[...], gk_ref[...], ones).astype(BF16)
        vb_ref[...] = v_ref[...].astype(BF16)

    lane = lax.broadcasted_iota(jnp.int32, (1, pairs * LANES), 1)
    first = (lane % LANES) < HEAD_DIM
    qn = _head_rms_norm(q_ref[...], gq_ref[...], ones) * (HEAD_DIM ** -0.5)
    q_first = jnp.where(first, qn, 0.0).astype(BF16)
    q_second = jnp.where(first, 0.0, qn).astype(BF16)
    lanes_of = lambda p: slice(p * LANES, (p + 1) * LANES)
    qq = [jnp.concatenate([q_first[:, lanes_of(p)], q_second[:, lanes_of(p)]], axis=0) for p in range(pairs)]

    jj = lax.broadcasted_iota(jnp.int32, (blk, blk), 0)
    ss = lax.broadcasted_iota(jnp.int32, (blk, blk), 1)
    uu = jnp.where(jj >= ss, 1.0, 0.0).astype(BF16)

    rows = 2 * blk
    tt = lax.broadcasted_iota(jnp.int32, (pairs * rows, blk), 0) % blk
    sk = lax.broadcasted_iota(jnp.int32, (pairs * rows, blk), 1)
    causal = sk < tt

    def step(j, c, acc, diagonal):
        start = pl.multiple_of(j * blk, blk)
        z = jnp.concatenate([_nt(qq[p], kn_ref[pl.ds(start, blk), lanes_of(p)]) for p in range(pairs)], axis=0)
        lk = -_softplus(z)
        if diagonal:
            lk = jnp.where(causal, lk, 0.0)
        sums = _mm_exact_rhs(lk, uu)
        w = jnp.exp(z + sums + c)
        if diagonal:
            w = jnp.where(causal, w, 0.0)
        w = w.astype(BF16)
        pv = jnp.concatenate([_mm(w[p * rows:(p + 1) * rows], vb_ref[pl.ds(start, blk), lanes_of(p)])
                              for p in range(pairs)], axis=0)
        c = c + sums[:, 0:1]
        return c, acc + pv, jnp.max(c) > UNDERFLOW_LOG

    carry = (qb - 1,) + step(qb, jnp.zeros((pairs * rows, 1), F32), jnp.zeros((pairs * rows, LANES), F32), True)
    _, _, acc, _ = lax.while_loop(lambda cr: (cr[0] >= 0) & cr[3],
                                  lambda cr: (cr[0] - 1,) + step(cr[0], cr[1], cr[2], False), carry)
    first_pair = first[:, :LANES]
    for p in range(pairs):
        o_ref[:, lanes_of(p)] = jnp.where(first_pair, acc[p * rows:p * rows + blk], acc[p * rows + blk:(p + 1) * rows])


def _sbattn(sb3, gq, gk, blk):
    b, s, w3 = sb3.shape
    width = w3 // 3
    gspec = pl.BlockSpec((1, width), lambda bi, qb: (0, 0))
    return pl.pallas_call(
        functools.partial(_sbattn_kernel, blk=blk, pairs=width // LANES),
        grid=(b, s // blk),
        in_specs=[pl.BlockSpec((None, blk, width), lambda bi, qb: (bi, qb, 0)),
                  pl.BlockSpec((None, s, width), lambda bi, qb: (bi, 0, 1)),
                  pl.BlockSpec((None, s, width), lambda bi, qb: (bi, 0, 2)), gspec, gspec],
        out_specs=pl.BlockSpec((None, blk, width), lambda bi, qb: (bi, qb, 0)),
        out_shape=jax.ShapeDtypeStruct((b, s, width), F32),
        scratch_shapes=[pltpu.VMEM((s, width), BF16), pltpu.VMEM((s, width), BF16)],
        compiler_params=pltpu.CompilerParams(dimension_semantics=("arbitrary",) * 2,
                                             vmem_limit_bytes=VMEM_LIMIT),
        name="sbattn",
    )(sb3, sb3, sb3, gq, gk)


def _token_shift(p, prev_ref, mu, first_tile):
    rows = lax.broadcasted_iota(jnp.int32, (p.shape[0], 1), 0)
    last_prev = jnp.where(first_tile, 0.0, prev_ref[7:8, :])
    prev = jnp.where(rows == 0, last_prev, pltpu.roll(p, 1, axis=0))
    return p + (prev - p) * mu


def _rwprep_stage(rw_ref, rwp_ref, lo_ref, lop_ref, mur_ref, mul_ref, w0_ref, a0_ref, kk_ref, ka_ref,
                  wup_ref, aup_ref, gup_ref, *, width):
    first_tile = pl.program_id(1) == 0
    pf = _token_shift(rw_ref[...], rwp_ref, mur_ref[...], first_tile)
    lf = _token_shift(lo_ref[...], lop_ref, mul_ref[...], first_tile)
    r = pf[:, :width]
    k = pf[:, width:2 * width]
    v = pf[:, 2 * width:]
    w = -_softplus(-(w0_ref[...] + _mm(jnp.tanh(lf).astype(BF16), wup_ref[...]))) - 0.5
    lr = _sigmoid(a0_ref[...] + _mm(lf.astype(BF16), aup_ref[...]))
    kk = k * kk_ref[...]
    ss = _mm_exact_rhs(kk * kk, _head_ones(width))
    kk = kk / jnp.maximum(jnp.sqrt(ss), 1e-12)
    log_decay = -jnp.exp(w)
    gate = _mm(_sigmoid(lf).astype(BF16), gup_ref[...])
    return r, log_decay, k * (1.0 + (lr - 1.0) * ka_ref[...]), v, -kk, kk * lr, gate


def _rwscan_kernel(rw_ref, rwp_ref, lo_ref, lop_ref, mur_ref, mul_ref, w0_ref, a0_ref, kk_ref, ka_ref,
                   wup_ref, aup_ref, gup_ref, rk_ref, lnw_ref, lnb_ref,
                   o_ref, st_ref, y_ref, *, tb, width):
    r_all, lw_all, k_all, v_all, a_all, b_all, g_all = _rwprep_stage(
        rw_ref, rwp_ref, lo_ref, lop_ref, mur_ref, mul_ref, w0_ref, a0_ref, kk_ref, ka_ref,
        wup_ref, aup_ref, gup_ref, width=width)

    t = RW_CHUNK
    n = LANES
    m = MXU_DIM
    groups = width // m
    pairs = width // n

    @pl.when(pl.program_id(1) == 0)
    def _():
        st_ref[...] = jnp.zeros_like(st_ref)

    ri = lax.broadcasted_iota(jnp.int32, (2 * n, 2 * n), 0)
    ci = lax.broadcasted_iota(jnp.int32, (2 * n, 2 * n), 1)
    same = ((ri % n) // HEAD_DIM) == ((ci % n) // HEAD_DIM)
    gmask = same & ((ci % HEAD_DIM) < (ri % HEAD_DIM) + jnp.where(ri >= n, 1, 0))
    rn = lax.broadcasted_iota(jnp.int32, (n, n), 0)
    cn = lax.broadcasted_iota(jnp.int32, (n, n), 1)
    bdmask = jnp.where((rn // HEAD_DIM) == (cn // HEAD_DIM), 1.0, 0.0)
    eye = jnp.where(rn == cn, 1.0, 0.0)
    ti = lax.broadcasted_iota(jnp.int32, (t, 4 * t), 0)
    si = lax.broadcasted_iota(jnp.int32, (t, 4 * t), 1)
    tri3 = jnp.where((si % t <= ti) & (si < 3 * t), 1.0, 0.0).astype(BF16)
    right_half = jnp.where(lax.broadcasted_iota(jnp.int32, (n, 2 * n), 1) >= n, 1.0, 0.0)
    cat = jnp.concatenate

    def blockdiag32(x):
        return cat([x, x], axis=0) * bdmask

    def prepare(rows, cols):
        r = r_all[rows, cols]
        lw = lw_all[rows, cols]
        k = k_all[rows, cols]
        v = v_all[rows, cols]
        a = a_all[rows, cols]
        b = b_all[rows, cols]

        l1, l2, l3 = _split3(lw)
        cum = _mm(tri3, cat([l1, l2, l3, l1], axis=0))
        yield
        cum_end = cum[t - 1:t, :]
        p_inv = jnp.exp(-cum)
        p_rest = jnp.exp(cum_end - cum)
        r32 = blockdiag32(r * jnp.exp(cum))
        r_bd = r32.astype(BF16)
        a_bd = blockdiag32(a * jnp.exp(cum - lw)).astype(BF16)
        k_bd = blockdiag32(k * p_inv).astype(BF16)
        b_bd = blockdiag32(b * p_inv).astype(BF16)
        kd_bd = blockdiag32(k * p_rest).astype(BF16)
        bd_bd = blockdiag32(b * p_rest).astype(BF16)
        v32 = blockdiag32(v)
        v_bd = v32.astype(BF16)
        zero_v = (cat([v32, v32], axis=1) * right_half).astype(BF16)

        g = jnp.where(gmask, _nt(cat([a_bd, r_bd], axis=0), cat([b_bd, k_bd], axis=0)), 0.0)
        yield
        aab = g[:n, :n]
        aak = g[:n, n:].astype(BF16)
        brb_brk = g[n:, :].astype(BF16)

        x = eye + aab
        p = aab.astype(BF16)
        p = _mm(p, p).astype(BF16)
        av = _mm(aak, v_bd).astype(BF16)
        yield
        steps = (t - 1).bit_length() - 1
        for i in range(steps):
            if i + 1 < steps:
                px_pp = _mm(p, cat([x.astype(BF16), p], axis=1))
                x = x + px_pp[:, :n]
                p = px_pp[:, n:].astype(BF16)
            else:
                x = x + _mm(p, x.astype(BF16))
            yield
        tinv = x.astype(BF16)

        w_u0 = _mm(tinv, cat([a_bd, av], axis=1)).astype(BF16)
        yield
        stack = cat([w_u0, zero_v], axis=0)
        q_y0 = _mm(brb_brk, stack)
        m_c0 = _tn(cat([bd_bd, kd_bd], axis=0), stack)
        yield
        q_mat = (r32 + q_y0[:, :n]).astype(BF16)
        m_mat = (eye * jnp.exp(cum_end) + m_c0[:, :n]).astype(BF16)
        lhs = cat([cat([q_mat, q_mat], axis=1), cat([m_mat, m_mat], axis=1)], axis=0)
        return lhs, q_y0[:, n:], m_c0[:, n:]

    def lockstep(generators):
        results = [None] * len(generators)
        live = list(enumerate(generators))
        while live:
            still = []
            for idx, gen in live:
                try:
                    next(gen)
                    still.append((idx, gen))
                except StopIteration as done:
                    results[idx] = done.value
            live = still
        return results

    lane_cols = [slice(pr * n, (pr + 1) * n) for pr in range(pairs)]
    time_rows = [slice(ch * t, (ch + 1) * t) for ch in range(tb // t)]
    prepared = lockstep([prepare(rows, cols) for rows in time_rows for cols in lane_cols])
    states = [st_ref[pr] for pr in range(pairs)]
    for ch, rows in enumerate(time_rows):
        for pr, cols in enumerate(lane_cols):
            lhs, y0, c0 = prepared[ch * pairs + pr]
            y_s = _mm(lhs, cat(_split2(states[pr]), axis=0))
            y = y_s[:n] + y0
            y_ref[rows, cols] = y[:t] + y[t:]
            states[pr] = y_s[n:] + c0
    for pr in range(pairs):
        st_ref[pr] = states[pr]

    ones = _head_ones(m)
    inv = 1.0 / HEAD_DIM
    for grp in range(groups):
        cols = slice(grp * m, (grp + 1) * m)
        y = y_ref[:, cols]
        mean = _mm_exact_rhs(y, ones) * inv
        d = y - mean
        var = _mm_exact_rhs(d * d, ones) * inv
        yn = d * lax.rsqrt(var + LN_X_EPS) * lnw_ref[:, cols] + lnb_ref[:, cols]
        bonus = _mm_exact_rhs(r_all[:, cols] * k_all[:, cols] * rk_ref[:, cols], ones) * v_all[:, cols]
        o_ref[:, cols] = (yn + bonus) * g_all[:, cols]


def _rwscan(rw3, lo3, mu_rw, mu_lo, w0, a0, k_k, k_a, wup, aup, gup, r_k, ln_w, ln_b, tb):
    bsz, s, w3 = rw3.shape
    width = w3 // 3
    nlo = lo3.shape[2]
    cur = lambda c: pl.BlockSpec((None, tb, c), lambda bi, i: (bi, i, 0))
    prev = lambda c: pl.BlockSpec((None, SUBLANES, c),
                                  lambda bi, i: (bi, jnp.maximum(i * (tb // SUBLANES) - 1, 0), 0))
    full = lambda x: pl.BlockSpec(x.shape, lambda bi, i: (0,) * x.ndim)
    params = (mu_rw, mu_lo, w0, a0, k_k, k_a, wup, aup, gup, r_k, ln_w, ln_b)
    return pl.pallas_call(
        functools.partial(_rwscan_kernel, tb=tb, width=width),
        grid=(bsz, s // tb),
        in_specs=[cur(w3), prev(w3), cur(nlo), prev(nlo)] + [full(p) for p in params],
        out_specs=cur(width),
        out_shape=jax.ShapeDtypeStruct((bsz, s, width), F32),
        scratch_shapes=[pltpu.VMEM((width // LANES, LANES, LANES), F32)] + [pltpu.VMEM((tb, width), F32)],
        compiler_params=pltpu.CompilerParams(dimension_semantics=("arbitrary",) * 2,
                                             vmem_limit_bytes=VMEM_LIMIT),
        name="rwscan",
    )(rw3, rw3, lo3, lo3, *params)


def _outproj_kernel(sb_ref, rw_ref, x_ref, wsb_ref, wrw_ref, g_ref, wr_ref,
                    x1_ref, h2_ref, meta_ref, cnt_ref, run_ref, *, tm):
    @pl.when(pl.program_id(0) == 0)
    def _():
        run_ref[...] = jnp.zeros_like(run_ref)

    x1 = x_ref[...] + _mm(sb_ref[...].astype(BF16), wsb_ref[...]) + _mm(rw_ref[...].astype(BF16), wrw_ref[...])
    x1_ref[...] = x1
    ms = jnp.mean(x1 * x1, axis=-1, keepdims=True)
    h2 = x1 * lax.rsqrt(ms + NORM_EPS) * g_ref[...]
    _store_row_tiles(h2_ref, 0, h2)

    h_hi, h_lo = _split2(h2)
    w_hi, w_lo = _split2(wr_ref[...])
    lg = _mm(h_hi, w_hi) + _mm(h_hi, w_lo) + _mm(h_lo, w_hi)

    lane = lax.broadcasted_iota(jnp.int32, (tm, LANES), 1).astype(F32)
    neg = -jnp.inf
    big = float(LANES)
    is_group = lane < N_GROUPS
    gl = jnp.where(is_group, lg, neg)
    gmax = jnp.max(gl, axis=1, keepdims=True)
    gidx = jnp.min(jnp.where(gl == gmax, lane, big), axis=1, keepdims=True)
    group_gate = 1.0 / jnp.sum(jnp.where(is_group, jnp.exp(lg - gmax), 0.0), axis=1, keepdims=True)
    lo_lane = N_GROUPS + EXPERTS_PER_GROUP * gidx
    el = jnp.where((lane >= lo_lane) & (lane < lo_lane + EXPERTS_PER_GROUP), lg, neg)
    m1 = jnp.max(el, axis=1, keepdims=True)
    i1 = jnp.min(jnp.where(el == m1, lane, big), axis=1, keepdims=True)
    el2 = jnp.where(lane == i1, neg, el)
    m2 = jnp.max(el2, axis=1, keepdims=True)
    i2 = jnp.min(jnp.where(el2 == m2, lane, big), axis=1, keepdims=True)
    p2 = jnp.exp(m2 - m1)
    gate1 = group_gate / (1.0 + p2)
    gate2 = group_gate * p2 / (1.0 + p2)

    hit1 = lane == i1
    hit2 = lane == i2
    onehot = jnp.where(hit1 | hit2, 1.0, 0.0)
    rr = lax.broadcasted_iota(jnp.int32, (tm, tm), 0)
    cc = lax.broadcasted_iota(jnp.int32, (tm, tm), 1)
    below = jnp.where(cc < rr, 1.0, 0.0).astype(BF16)
    before = run_ref[...] + _mm(below, onehot.astype(BF16))
    rank1 = jnp.sum(jnp.where(hit1, before, 0.0), axis=1, keepdims=True)
    rank2 = jnp.sum(jnp.where(hit2, before, 0.0), axis=1, keepdims=True)
    run = run_ref[...] + jnp.sum(onehot, axis=0, keepdims=True)
    run_ref[...] = run
    cnt_ref[...] = run

    vals = (i1 - N_GROUPS, i2 - N_GROUPS, gate1, gate2, rank1, rank2)
    meta = jnp.zeros((tm, LANES), F32)
    for pos, val in enumerate(vals):
        meta = jnp.where(lane == pos, val.astype(F32), meta)
    meta_ref[...] = meta


def _outproj(sb_out, rw_out, x2, w_sb, w_rw, gain, w_router, tm):
    n, d = x2.shape
    rows = lambda c: pl.BlockSpec((tm, c), lambda i: (i, 0))
    full = lambda a: pl.BlockSpec(a.shape, lambda i: (0,) * a.ndim)
    return pl.pallas_call(
        functools.partial(_outproj_kernel, tm=tm),
        grid=(n // tm,),
        in_specs=[rows(sb_out.shape[1]), rows(rw_out.shape[1]), rows(d), full(w_sb), full(w_rw), full(gain),
                  full(w_router)],
        out_specs=[rows(d), pl.BlockSpec((tm * (d // LANES), LANES), lambda i: (i, 0)), rows(LANES),
                   pl.BlockSpec((1, LANES), lambda i: (0, 0))],
        out_shape=[jax.ShapeDtypeStruct((n, d), F32), jax.ShapeDtypeStruct((n * (d // LANES), LANES), F32),
                   jax.ShapeDtypeStruct((n, LANES), F32), jax.ShapeDtypeStruct((1, LANES), F32)],
        scratch_shapes=[pltpu.VMEM((1, LANES), F32)],
        compiler_params=pltpu.CompilerParams(dimension_semantics=("arbitrary",),
                                             vmem_limit_bytes=VMEM_LIMIT),
        name="outproj",
    )(sb_out, rw_out, x2, w_sb, w_rw, gain, w_router)


def _start_tile_gather(src_ref, idx_ref, dst_ref, dst_base, sem, n_rows):
    def issue(r, _):
        _tile_copy(src_ref, idx_ref[0, r], dst_ref, dst_base + r, sem).start()
        return 0

    lax.fori_loop(0, n_rows, issue, 0, unroll=8)


def _wait_tile_gather(src_ref, dst_ref, sem, n_rows):
    def drain(r, _):
        _tile_copy(src_ref, 0, dst_ref, 0, sem).wait()
        return 0

    lax.fori_loop(0, n_rows, drain, 0, unroll=8)


def _srcmap_kernel(dest_ref, seg_ref, src_ref):
    def put(tok, _):
        for kk in range(TOP_K):
            src_ref[dest_ref[TOP_K * tok + kk]] = tok
        return 0

    lax.fori_loop(0, dest_ref.shape[0] // TOP_K, put, 0, unroll=8)

    def pad(e, _):
        first = seg_ref[0, e]

        def clear(r, _):
            src_ref[first + r] = 0
            return 0

        lax.fori_loop(seg_ref[1, e], seg_ref[2, e], clear, 0)
        return 0

    lax.fori_loop(0, seg_ref.shape[1], pad, 0)


def _srcmap(dest_flat, segments, n_rows):
    return pl.pallas_call(
        _srcmap_kernel,
        in_specs=[pl.BlockSpec(memory_space=pltpu.SMEM), pl.BlockSpec(memory_space=pltpu.SMEM)],
        out_specs=pl.BlockSpec(memory_space=pltpu.SMEM),
        out_shape=jax.ShapeDtypeStruct((n_rows,), jnp.int32),
        name="srcmap",
    )(dest_flat, segments)


def _experts_kernel(eid_ref, used_ref, cur_ref, nxt_ref, h_ref, wg_ref, wu_ref, wd_ref, o_ref,
                    x_ref, wgb_ref, wub_ref, wdb_ref, sems):
    i = pl.program_id(0)
    used = used_ref[0]
    slot = i % 2
    rows = EXPERT_ROWS

    @pl.when(i == 0)
    def _():
        _start_tile_gather(h_ref, cur_ref, x_ref, 0, sems.at[0], rows)

    @pl.when(i + 1 < used)
    def _():
        _start_tile_gather(h_ref, nxt_ref, x_ref, (1 - slot) * rows, sems.at[1 - slot], rows)

    @pl.when((i == 0) | (eid_ref[i] != eid_ref[jnp.maximum(i - 1, 0)]))
    def _():
        wgb_ref[...] = wg_ref[...].astype(BF16)
        wub_ref[...] = wu_ref[...].astype(BF16)
        wdb_ref[...] = wd_ref[...].astype(BF16)

    @pl.when(i < used)
    def _():
        _wait_tile_gather(h_ref, x_ref, sems.at[slot], rows)
        xb = _load_row_tiles(x_ref, slot * (rows * SUBLANES), rows, SUBLANES).astype(BF16)
        hg = _mm(xb, wgb_ref[...])
        hu = _mm(xb, wub_ref[...])
        act = hg * _sigmoid(hg) * hu
        _store_row_tiles(o_ref, 0, _mm(act.astype(BF16), wdb_ref[...]))

    @pl.when(i >= used)
    def _():
        o_ref[...] = jnp.zeros_like(o_ref)


def _experts(blk_eid, n_used, src3, h2_tiles, w_g, w_u, w_d):
    n_blocks = src3.shape[0]
    d, de = w_g.shape[1], w_g.shape[2]
    assert d == SUBLANES * LANES
    w_map = lambda i, eid, used: (eid[i], 0, 0)
    idx_spec = lambda ahead: pl.BlockSpec(
        (None, 1, EXPERT_ROWS), lambda i, eid, used: (jnp.minimum(i + ahead, used[0] - 1), 0, 0),
        memory_space=pltpu.SMEM)
    grid_spec = pltpu.PrefetchScalarGridSpec(
        num_scalar_prefetch=2,
        grid=(n_blocks,),
        in_specs=[idx_spec(0), idx_spec(1), pl.BlockSpec(memory_space=pl.ANY),
                  pl.BlockSpec((None, d, de), w_map), pl.BlockSpec((None, d, de), w_map),
                  pl.BlockSpec((None, de, d), w_map)],
        out_specs=pl.BlockSpec((EXPERT_ROWS * SUBLANES, LANES), lambda i, eid, used: (i, 0)),
        scratch_shapes=[pltpu.VMEM((2 * EXPERT_ROWS * SUBLANES, LANES), F32),
                        pltpu.VMEM((d, de), BF16), pltpu.VMEM((d, de), BF16), pltpu.VMEM((de, d), BF16),
                        pltpu.SemaphoreType.DMA((2,))],
    )
    return pl.pallas_call(
        _experts_kernel,
        grid_spec=grid_spec,
        out_shape=jax.ShapeDtypeStruct((n_blocks * EXPERT_ROWS * SUBLANES, LANES), F32),
        compiler_params=pltpu.CompilerParams(dimension_semantics=("arbitrary",),
                                             vmem_limit_bytes=VMEM_LIMIT),
        name="experts",
    )(blk_eid, n_used, src3, src3, h2_tiles, w_g, w_u, w_d)


def _combine_kernel(dest_ref, rows_ref, x1_ref, meta_ref, o_ref, got_ref, sem, *, tm):
    def issue(t, _):
        for kk in range(TOP_K):
            _tile_copy(rows_ref, dest_ref[0, TOP_K * t + kk], got_ref, kk * tm + t, sem).start(priority=kk)
        return 0

    lax.fori_loop(0, tm, issue, 0, unroll=4)
    _wait_tile_gather(rows_ref, got_ref, sem, TOP_K * tm)
    meta = meta_ref[...]
    y = x1_ref[...]
    for kk in range(TOP_K):
        y = y + _load_row_tiles(got_ref, kk * tm * SUBLANES, tm, SUBLANES) * meta[:, 2 + kk:3 + kk]
    o_ref[...] = y


def _combine(dest3, exp_out_tiles, x1, meta, tm):
    n, d = x1.shape
    rows = lambda c: pl.BlockSpec((tm, c), lambda i: (i, 0))
    return pl.pallas_call(
        functools.partial(_combine_kernel, tm=tm),
        grid=(n // tm,),
        in_specs=[pl.BlockSpec((None, 1, TOP_K * tm), lambda i: (i, 0, 0), memory_space=pltpu.SMEM),
                  pl.BlockSpec(memory_space=pl.ANY), rows(d), rows(LANES)],
        out_specs=rows(d),
        out_shape=jax.ShapeDtypeStruct((n, d), F32),
        scratch_shapes=[pltpu.VMEM((TOP_K * tm * SUBLANES, LANES), F32), pltpu.SemaphoreType.DMA(())],
        compiler_params=pltpu.CompilerParams(dimension_semantics=("arbitrary",),
                                             vmem_limit_bytes=VMEM_LIMIT),
        name="combine",
    )(dest3, exp_out_tiles, x1, meta)


def _layer(x, norm1_gain, w_in, sb_q_gain, sb_k_gain, rw_shift_mu, rw_w0, rw_w_up, rw_a0, rw_a_up, rw_g_up,
           rw_k_k, rw_k_a, rw_r_k, rw_ln_w, rw_ln_b, w_out, norm2_gain, w_router_group, w_router_expert,
           w_exp_gate, w_exp_up, w_exp_down):
    bsz, s, d = x.shape
    n = bsz * s
    sbw = d // 2
    rww = d - sbw
    n_decay, n_aaa = rw_w_up.shape[0], rw_a_up.shape[0]
    n_lora = n_decay + n_aaa + rw_g_up.shape[0]
    tm = min(256, n)
    row = lambda a: a.reshape(1, -1).astype(F32)

    x2 = x.reshape(n, d)
    w_bf = w_in.astype(BF16)
    sb, rw, lo = _inproj(x2, row(norm1_gain), w_bf[:, :3 * sbw], w_bf[:, 3 * sbw:3 * sbw + 3 * rww],
                         w_bf[:, 3 * sbw + 3 * rww:], tm)
    heads = lambda gvec: jnp.tile(row(gvec), (1, sbw // HEAD_DIM))
    sb_out = _sbattn(sb.reshape(bsz, s, 3 * sbw), heads(sb_q_gain), heads(sb_k_gain), min(SB_BLOCK, s))

    pad_rows = lambda w_up, start: jnp.zeros((n_lora, rww), BF16).at[start:start + w_up.shape[0]].set(
        w_up.astype(BF16))
    mu = row(rw_shift_mu)
    rw_out = _rwscan(rw.reshape(bsz, s, 3 * rww), lo.reshape(bsz, s, n_lora), mu[:, :3 * rww], mu[:, 3 * rww:],
                     row(rw_w0), row(rw_a0), row(rw_k_k), row(rw_k_a),
                     pad_rows(rw_w_up, 0), pad_rows(rw_a_up, n_decay), pad_rows(rw_g_up, n_decay + n_aaa),
                     row(rw_r_k), row(rw_ln_w), row(rw_ln_b), min(256, s))

    w_out_bf = w_out.astype(BF16)
    w_router = jnp.zeros((d, LANES), F32).at[:, :N_GROUPS].set(w_router_group).at[
        :, N_GROUPS:N_GROUPS + N_EXPERTS].set(w_router_expert)
    x1, h2, meta, counts = _outproj(sb_out.reshape(n, sbw), rw_out.reshape(n, rww), x2, w_out_bf[:sbw],
                                    w_out_bf[sbw:], row(norm2_gain), w_router, tm)

    counts = counts[0, N_GROUPS:N_GROUPS + N_EXPERTS].astype(jnp.int32)
    padded = (counts + EXPERT_ROWS - 1) // EXPERT_ROWS * EXPERT_ROWS
    pad_end = jnp.cumsum(padded)
    pad_start = pad_end - padded
    eid = meta[:, 0:TOP_K].astype(jnp.int32)
    rank = meta[:, 4:4 + TOP_K].astype(jnp.int32)
    dest = pad_start[eid] + rank
    dest3 = dest.reshape(n // tm, 1, TOP_K * tm)
    n_rows = n * TOP_K + N_EXPERTS * EXPERT_ROWS
    n_blocks = n_rows // EXPERT_ROWS
    block_start = jnp.arange(n_blocks, dtype=jnp.int32) * EXPERT_ROWS
    blk_eid = jnp.minimum(jnp.sum((pad_end[None, :] <= block_start[:, None]).astype(jnp.int32), axis=1),
                          N_EXPERTS - 1)
    n_used = (pad_end[-1:] // EXPERT_ROWS).astype(jnp.int32)

    tail = jnp.zeros((1,), jnp.int32)
    segments = jnp.stack([jnp.concatenate([pad_start, pad_end[-1:]]), jnp.concatenate([counts, tail]),
                          jnp.concatenate([padded, n_rows - pad_end[-1:]])]).astype(jnp.int32)
    src3 = _srcmap(dest.reshape(-1), segments, n_rows).reshape(n_blocks, 1, EXPERT_ROWS)

    exp_out = _experts(blk_eid, n_used, src3, h2, w_exp_gate, w_exp_up, w_exp_down)
    return _combine(dest3, exp_out, x1, meta, tm).reshape(bsz, s, d)


def kernel(x, norm1_gain, w_in, sb_q_gain, sb_k_gain, rw_shift_mu, rw_w0, rw_w_up, rw_a0, rw_a_up, rw_g_up,
           rw_k_k, rw_k_a, rw_r_k, rw_ln_w, rw_ln_b, w_out, norm2_gain, w_router_group, w_router_expert,
           w_exp_gate, w_exp_up, w_exp_down):
    params = (norm1_gain, w_in, sb_q_gain, sb_k_gain, rw_shift_mu, rw_w0, rw_w_up, rw_a0, rw_a_up, rw_g_up,
              rw_k_k, rw_k_a, rw_r_k, rw_ln_w, rw_ln_b, w_out, norm2_gain, w_router_group, w_router_expert,
              w_exp_gate, w_exp_up, w_exp_down)
    for layer in range(norm1_gain.shape[0]):
        x = _layer(x, *(p[layer] for p in params))
    return x
```

```python
import functools

import jax
import jax.numpy as jnp
from jax import lax
from jax.experimental import pallas as pl
from jax.experimental.pallas import tpu as pltpu

F32 = jnp.float32
BF16 = jnp.bfloat16

HEAD_DIM = 64
NORM_EPS = 1e-6
LN_X_EPS = 64e-5
N_GROUPS = 4
EXPERTS_PER_GROUP = 8
N_EXPERTS = N_GROUPS * EXPERTS_PER_GROUP
TOP_K = 2

LANES = 128
SUBLANES = 8
MXU_DIM = 256
RW_CHUNK = 64
SB_BLOCK = 256
EXPERT_ROWS = 256
UNDERFLOW_LOG = -110.0
VMEM_LIMIT = 56 * 1024 * 1024


def _nt(a, b):
    return lax.dot_general(a, b, (((1,), (1,)), ((), ())), preferred_element_type=F32)


def _tn(a, b):
    return lax.dot_general(a, b, (((0,), (0,)), ((), ())), preferred_element_type=F32)


def _mm(a, b):
    return jnp.dot(a, b, preferred_element_type=F32)


def _split2(x):
    hi = x.astype(BF16)
    lo = (x - hi.astype(F32)).astype(BF16)
    return hi, lo


def _split3(x):
    h1 = x.astype(BF16)
    r1 = x - h1.astype(F32)
    h2 = r1.astype(BF16)
    h3 = (r1 - h2.astype(F32)).astype(BF16)
    return h1, h2, h3


def _mm_exact_rhs(x, m):
    hi, lo = _split2(x)
    if 2 * x.shape[1] <= MXU_DIM:
        return _mm(jnp.concatenate([hi, lo], axis=1), jnp.concatenate([m, m], axis=0))
    return _mm(hi, m) + _mm(lo, m)


def _head_ones(width):
    r = lax.broadcasted_iota(jnp.int32, (width, width), 0) // HEAD_DIM
    c = lax.broadcasted_iota(jnp.int32, (width, width), 1) // HEAD_DIM
    return jnp.where(r == c, 1.0, 0.0).astype(BF16)


def _softplus(z):
    return jnp.maximum(z, 0.0) + jnp.log(1.0 + jnp.exp(-jnp.abs(z)))


def _sigmoid(z):
    return 1.0 / (1.0 + jnp.exp(-z))


def _store_row_tiles(ref, base, x):
    pieces = x.shape[1] // LANES
    for s in range(pieces):
        ref[pl.ds(base + s, x.shape[0], stride=pieces), :] = x[:, s * LANES:(s + 1) * LANES]


def _load_row_tiles(ref, base, rows, pieces):
    return jnp.concatenate([ref[pl.ds(base + s, rows, stride=pieces), :] for s in range(pieces)], axis=1)


def _tile_copy(src_ref, src_row, dst_ref, dst_row, sem):
    src = src_ref.at[pl.ds(pl.multiple_of(src_row * SUBLANES, SUBLANES), SUBLANES)]
    dst = dst_ref.at[pl.ds(pl.multiple_of(dst_row * SUBLANES, SUBLANES), SUBLANES)]
    return pltpu.make_async_copy(src, dst, sem)


def _inproj_kernel(x_ref, g_ref, wsb_ref, wrw_ref, wlo_ref, sb_ref, rw_ref, lo_ref):
    x = x_ref[...]
    ms = jnp.mean(x * x, axis=-1, keepdims=True)
    h = (x * lax.rsqrt(ms + NORM_EPS) * g_ref[...]).astype(BF16)
    sb_ref[...] = _mm(h, wsb_ref[...])
    rw_ref[...] = _mm(h, wrw_ref[...])
    lo_ref[...] = _mm(h, wlo_ref[...])


def _inproj(x2, gain, w_sb, w_rw, w_lo, tm):
    n, d = x2.shape
    full = lambda a: pl.BlockSpec(a.shape, lambda i: (0,) * a.ndim)
    rows = lambda c: pl.BlockSpec((tm, c), lambda i: (i, 0))
    return pl.pallas_call(
        _inproj_kernel,
        grid=(n // tm,),
        in_specs=[rows(d), full(gain), full(w_sb), full(w_rw), full(w_lo)],
        out_specs=[rows(w_sb.shape[1]), rows(w_rw.shape[1]), rows(w_lo.shape[1])],
        out_shape=[jax.ShapeDtypeStruct((n, w.shape[1]), F32) for w in (w_sb, w_rw, w_lo)],
        compiler_params=pltpu.CompilerParams(dimension_semantics=("arbitrary",),
                                             vmem_limit_bytes=VMEM_LIMIT),
        name="inproj",
    )(x2, gain, w_sb, w_rw, w_lo)


def _head_rms_norm(x, gain, ones):
    ms = _mm_exact_rhs(x * x, ones) * (1.0 / HEAD_DIM)
    return x * lax.rsqrt(ms + NORM_EPS) * gain


def _sbattn_kernel(q_ref, k_ref, v_ref, gq_ref, gk_ref, o_ref, kn_ref, vb_ref, *, blk, pairs):
    qb = pl.program_id(1)
    ones = _head_ones(pairs * LANES)

    @pl.when(qb == 0)
    def _():
        kn_ref[...] = _head_rms_norm(k_ref[...], gk_ref[...], ones).astype(BF16)
        vb_ref[...] = v_ref[...].astype(BF16)

    lane = lax.broadcasted_iota(jnp.int32, (1, pairs * LANES), 1)
    first = (lane % LANES) < HEAD_DIM
    qn = _head_rms_norm(q_ref[...], gq_ref[...], ones) * (HEAD_DIM ** -0.5)
    q_first = jnp.where(first, qn, 0.0).astype(BF16)
    q_second = jnp.where(first, 0.0, qn).astype(BF16)
    lanes_of = lambda p: slice(p * LANES, (p + 1) * LANES)
    qq = [jnp.concatenate([q_first[:, lanes_of(p)], q_second[:, lanes_of(p)]], axis=0) for p in range(pairs)]

    jj = lax.broadcasted_iota(jnp.int32, (blk, blk), 0)
    ss = lax.broadcasted_iota(jnp.int32, (blk, blk), 1)
    uu = jnp.where(jj >= ss, 1.0, 0.0).astype(BF16)

    rows = 2 * blk
    tt = lax.broadcasted_iota(jnp.int32, (pairs * rows, blk), 0) % blk
    sk = lax.broadcasted_iota(jnp.int32, (pairs * rows, blk), 1)
    causal = sk < tt

    def step(j, c, acc, diagonal):
        start = pl.multiple_of(j * blk, blk)
        z = jnp.concatenate([_nt(qq[p], kn_ref[pl.ds(start, blk), lanes_of(p)]) for p in range(pairs)], axis=0)
        lk = -_softplus(z)
        if diagonal:
            lk = jnp.where(causal, lk, 0.0)
        sums = _mm_exact_rhs(lk, uu)
        w = jnp.exp(z + sums + c)
        if diagonal:
            w = jnp.where(causal, w, 0.0)
        w = w.astype(BF16)
        pv = jnp.concatenate([_mm(w[p * rows:(p + 1) * rows], vb_ref[pl.ds(start, blk), lanes_of(p)])
                              for p in range(pairs)], axis=0)
        c = c + sums[:, 0:1]
        return c, acc + pv, jnp.max(c) > UNDERFLOW_LOG

    carry = (qb - 1,) + step(qb, jnp.zeros((pairs * rows, 1), F32), jnp.zeros((pairs * rows, LANES), F32), True)
    _, _, acc, _ = lax.while_loop(lambda cr: (cr[0] >= 0) & cr[3],
                                  lambda cr: (cr[0] - 1,) + step(cr[0], cr[1], cr[2], False), carry)
    first_pair = first[:, :LANES]
    for p in range(pairs):
        o_ref[:, lanes_of(p)] = jnp.where(first_pair, acc[p * rows:p * rows + blk], acc[p * rows + blk:(p + 1) * rows])


def _sbattn(sb3, gq, gk, blk):
    b, s, w3 = sb3.shape
    width = w3 // 3
    gspec = pl.BlockSpec((1, width), lambda bi, qb: (0, 0))
    return pl.pallas_call(
        functools.partial(_sbattn_kernel, blk=blk, pairs=width // LANES),
        grid=(b, s // blk),
        in_specs=[pl.BlockSpec((None, blk, width), lambda bi, qb: (bi, qb, 0)),
                  pl.BlockSpec((None, s, width), lambda bi, qb: (bi, 0, 1)),
                  pl.BlockSpec((None, s, width), lambda bi, qb: (bi, 0, 2)), gspec, gspec],
        out_specs=pl.BlockSpec((None, blk, width), lambda bi, qb: (bi, qb, 0)),
        out_shape=jax.ShapeDtypeStruct((b, s, width), F32),
        scratch_shapes=[pltpu.VMEM((s, width), BF16), pltpu.VMEM((s, width), BF16)],
        compiler_params=pltpu.CompilerParams(dimension_semantics=("arbitrary",) * 2,
                                             vmem_limit_bytes=VMEM_LIMIT),
        name="sbattn",
    )(sb3, sb3, sb3, gq, gk)


def _token_shift(p, prev_ref, mu, first_tile):
    rows = lax.broadcasted_iota(jnp.int32, (p.shape[0], 1), 0)
    last_prev = jnp.where(first_tile, 0.0, prev_ref[7:8, :])
    prev = jnp.where(rows == 0, last_prev, pltpu.roll(p, 1, axis=0))
    return p + (prev - p) * mu


def _rwprep_stage(rw_ref, rwp_ref, lo_ref, lop_ref, mur_ref, mul_ref, w0_ref, a0_ref, kk_ref, ka_ref,
                  wup_ref, aup_ref, gup_ref, *, width):
    first_tile = pl.program_id(1) == 0
    pf = _token_shift(rw_ref[...], rwp_ref, mur_ref[...], first_tile)
    lf = _token_shift(lo_ref[...], lop_ref, mul_ref[...], first_tile)
    r = pf[:, :width]
    k = pf[:, width:2 * width]
    v = pf[:, 2 * width:]
    w = -_softplus(-(w0_ref[...] + _mm(jnp.tanh(lf).astype(BF16), wup_ref[...]))) - 0.5
    lr = _sigmoid(a0_ref[...] + _mm(lf.astype(BF16), aup_ref[...]))
    kk = k * kk_ref[...]
    ss = _mm_exact_rhs(kk * kk, _head_ones(width))
    kk = kk / jnp.maximum(jnp.sqrt(ss), 1e-12)
    log_decay = -jnp.exp(w)
    gate = _mm(_sigmoid(lf).astype(BF16), gup_ref[...])
    return r, log_decay, k * (1.0 + (lr - 1.0) * ka_ref[...]), v, -kk, kk * lr, gate


def _rwscan_kernel(rw_ref, rwp_ref, lo_ref, lop_ref, mur_ref, mul_ref, w0_ref, a0_ref, kk_ref, ka_ref,
                   wup_ref, aup_ref, gup_ref, rk_ref, lnw_ref, lnb_ref,
                   o_ref, st_ref, y_ref, *, tb, width):
    r_all, lw_all, k_all, v_all, a_all, b_all, g_all = _rwprep_stage(
        rw_ref, rwp_ref, lo_ref, lop_ref, mur_ref, mul_ref, w0_ref, a0_ref, kk_ref, ka_ref,
        wup_ref, aup_ref, gup_ref, width=width)

    t = RW_CHUNK
    n = LANES
    m = MXU_DIM
    groups = width // m
    pairs = width // n

    @pl.when(pl.program_id(1) == 0)
    def _():
        st_ref[...] = jnp.zeros_like(st_ref)

    ri = lax.broadcasted_iota(jnp.int32, (2 * n, 2 * n), 0)
    ci = lax.broadcasted_iota(jnp.int32, (2 * n, 2 * n), 1)
    same = ((ri % n) // HEAD_DIM) == ((ci % n) // HEAD_DIM)
    gmask = same & ((ci % HEAD_DIM) < (ri % HEAD_DIM) + jnp.where(ri >= n, 1, 0))
    rn = lax.broadcasted_iota(jnp.int32, (n, n), 0)
    cn = lax.broadcasted_iota(jnp.int32, (n, n), 1)
    bdmask = jnp.where((rn // HEAD_DIM) == (cn // HEAD_DIM), 1.0, 0.0)
    eye = jnp.where(rn == cn, 1.0, 0.0)
    ti = lax.broadcasted_iota(jnp.int32, (t, 4 * t), 0)
    si = lax.broadcasted_iota(jnp.int32, (t, 4 * t), 1)
    tri3 = jnp.where((si % t <= ti) & (si < 3 * t), 1.0, 0.0).astype(BF16)
    right_half = jnp.where(lax.broadcasted_iota(jnp.int32, (n, 2 * n), 1) >= n, 1.0, 0.0)
    cat = jnp.concatenate

    def blockdiag32(x):
        return cat([x, x], axis=0) * bdmask

    def prepare(rows, cols):
        r = r_all[rows, cols]
        lw = lw_all[rows, cols]
        k = k_all[rows, cols]
        v = v_all[rows, cols]
        a = a_all[rows, cols]
        b = b_all[rows, cols]

        l1, l2, l3 = _split3(lw)
        cum = _mm(tri3, cat([l1, l2, l3, l1], axis=0))
        yield
        cum_end = cum[t - 1:t, :]
        p_inv = jnp.exp(-cum)
        p_rest = jnp.exp(cum_end - cum)
        r32 = blockdiag32(r * jnp.exp(cum))
        r_bd = r32.astype(BF16)
        a_bd = blockdiag32(a * jnp.exp(cum - lw)).astype(BF16)
        k_bd = blockdiag32(k * p_inv).astype(BF16)
        b_bd = blockdiag32(b * p_inv).astype(BF16)
        kd_bd = blockdiag32(k * p_rest).astype(BF16)
        bd_bd = blockdiag32(b * p_rest).astype(BF16)
        v32 = blockdiag32(v)
        v_bd = v32.astype(BF16)
        zero_v = (cat([v32, v32], axis=1) * right_half).astype(BF16)

        g = jnp.where(gmask, _nt(cat([a_bd, r_bd], axis=0), cat([b_bd, k_bd], axis=0)), 0.0)
        yield
        aab = g[:n, :n]
        aak = g[:n, n:].astype(BF16)
        brb_brk = g[n:, :].astype(BF16)

        x = eye + aab
        p = aab.astype(BF16)
        p = _mm(p, p).astype(BF16)
        av = _mm(aak, v_bd).astype(BF16)
        yield
        steps = (t - 1).bit_length() - 1
        for i in range(steps):
            if i + 1 < steps:
                px_pp = _mm(p, cat([x.astype(BF16), p], axis=1))
                x = x + px_pp[:, :n]
                p = px_pp[:, n:].astype(BF16)
            else:
                x = x + _mm(p, x.astype(BF16))
            yield
        tinv = x.astype(BF16)

        w_u0 = _mm(tinv, cat([a_bd, av], axis=1)).astype(BF16)
        yield
        stack = cat([w_u0, zero_v], axis=0)
        q_y0 = _mm(brb_brk, stack)
        m_c0 = _tn(cat([bd_bd, kd_bd], axis=0), stack)
        yield
        q_mat = (r32 + q_y0[:, :n]).astype(BF16)
        m_mat = (eye * jnp.exp(cum_end) + m_c0[:, :n]).astype(BF16)
        lhs = cat([cat([q_mat, q_mat], axis=1), cat([m_mat, m_mat], axis=1)], axis=0)
        return lhs, q_y0[:, n:], m_c0[:, n:]

    def lockstep(generators):
        results = [None] * len(generators)
        live = list(enumerate(generators))
        while live:
            still = []
            for idx, gen in live:
                try:
                    next(gen)
                    still.append((idx, gen))
                except StopIteration as done:
                    results[idx] = done.value
            live = still
        return results

    lane_cols = [slice(pr * n, (pr + 1) * n) for pr in range(pairs)]
    time_rows = [slice(ch * t, (ch + 1) * t) for ch in range(tb // t)]
    prepared = lockstep([prepare(rows, cols) for rows in time_rows for cols in lane_cols])
    states = [st_ref[pr] for pr in range(pairs)]
    for ch, rows in enumerate(time_rows):
        for pr, cols in enumerate(lane_cols):
            lhs, y0, c0 = prepared[ch * pairs + pr]
            y_s = _mm(lhs, cat(_split2(states[pr]), axis=0))
            y = y_s[:n] + y0
            y_ref[rows, cols] = y[:t] + y[t:]
            states[pr] = y_s[n:] + c0
    for pr in range(pairs):
        st_ref[pr] = states[pr]

    ones = _head_ones(m)
    inv = 1.0 / HEAD_DIM
    for grp in range(groups):
        cols = slice(grp * m, (grp + 1) * m)
        y = y_ref[:, cols]
        mean = _mm_exact_rhs(y, ones) * inv
        d = y - mean
        var = _mm_exact_rhs(d * d, ones) * inv
        yn = d * lax.rsqrt(var + LN_X_EPS) * lnw_ref[:, cols] + lnb_ref[:, cols]
        bonus = _mm_exact_rhs(r_all[:, cols] * k_all[:, cols] * rk_ref[:, cols], ones) * v_all[:, cols]
        o_ref[:, cols] = (yn + bonus) * g_all[:, cols]


def _rwscan(rw3, lo3, mu_rw, mu_lo, w0, a0, k_k, k_a, wup, aup, gup, r_k, ln_w, ln_b, tb):
    bsz, s, w3 = rw3.shape
    width = w3 // 3
    nlo = lo3.shape[2]
    cur = lambda c: pl.BlockSpec((None, tb, c), lambda bi, i: (bi, i, 0))
    prev = lambda c: pl.BlockSpec((None, SUBLANES, c),
                                  lambda bi, i: (bi, jnp.maximum(i * (tb // SUBLANES) - 1, 0), 0))
    full = lambda x: pl.BlockSpec(x.shape, lambda bi, i: (0,) * x.ndim)
    params = (mu_rw, mu_lo, w0, a0, k_k, k_a, wup, aup, gup, r_k, ln_w, ln_b)
    return pl.pallas_call(
        functools.partial(_rwscan_kernel, tb=tb, width=width),
        grid=(bsz, s // tb),
        in_specs=[cur(w3), prev(w3), cur(nlo), prev(nlo)] + [full(p) for p in params],
        out_specs=cur(width),
        out_shape=jax.ShapeDtypeStruct((bsz, s, width), F32),
        scratch_shapes=[pltpu.VMEM((width // LANES, LANES, LANES), F32)] + [pltpu.VMEM((tb, width), F32)],
        compiler_params=pltpu.CompilerParams(dimension_semantics=("arbitrary",) * 2,
                                             vmem_limit_bytes=VMEM_LIMIT),
        name="rwscan",
    )(rw3, rw3, lo3, lo3, *params)


def _outproj_kernel(sb_ref, rw_ref, x_ref, wsb_ref, wrw_ref, g_ref, wr_ref,
                    x1_ref, h2_ref, meta_ref, route_ref, cnt_ref, run_ref, *, tm):
    @pl.when(pl.program_id(0) == 0)
    def _():
        run_ref[...] = jnp.zeros_like(run_ref)

    x1 = x_ref[...] + _mm(sb_ref[...].astype(BF16), wsb_ref[...]) + _mm(rw_ref[...].astype(BF16), wrw_ref[...])
    x1_ref[...] = x1
    ms = jnp.mean(x1 * x1, axis=-1, keepdims=True)
    h2 = x1 * lax.rsqrt(ms + NORM_EPS) * g_ref[...]
    _store_row_tiles(h2_ref, 0, h2)

    h_hi, h_lo = _split2(h2)
    w_hi, w_lo = _split2(wr_ref[...])
    lg = _mm(h_hi, w_hi) + _mm(h_hi, w_lo) + _mm(h_lo, w_hi)

    lane = lax.broadcasted_iota(jnp.int32, (tm, LANES), 1).astype(F32)
    neg = -jnp.inf
    big = float(LANES)
    is_group = lane < N_GROUPS
    gl = jnp.where(is_group, lg, neg)
    gmax = jnp.max(gl, axis=1, keepdims=True)
    gidx = jnp.min(jnp.where(gl == gmax, lane, big), axis=1, keepdims=True)
    group_gate = 1.0 / jnp.sum(jnp.where(is_group, jnp.exp(lg - gmax), 0.0), axis=1, keepdims=True)
    lo_lane = N_GROUPS + EXPERTS_PER_GROUP * gidx
    el = jnp.where((lane >= lo_lane) & (lane < lo_lane + EXPERTS_PER_GROUP), lg, neg)
    m1 = jnp.max(el, axis=1, keepdims=True)
    i1 = jnp.min(jnp.where(el == m1, lane, big), axis=1, keepdims=True)
    el2 = jnp.where(lane == i1, neg, el)
    m2 = jnp.max(el2, axis=1, keepdims=True)
    i2 = jnp.min(jnp.where(el2 == m2, lane, big), axis=1, keepdims=True)
    p2 = jnp.exp(m2 - m1)
    gate1 = group_gate / (1.0 + p2)
    gate2 = group_gate * p2 / (1.0 + p2)

    hit1 = lane == i1
    hit2 = lane == i2
    onehot = jnp.where(hit1 | hit2, 1.0, 0.0)
    rr = lax.broadcasted_iota(jnp.int32, (tm, tm), 0)
    cc = lax.broadcasted_iota(jnp.int32, (tm, tm), 1)
    below = jnp.where(cc < rr, 1.0, 0.0).astype(BF16)
    before = run_ref[...] + _mm(below, onehot.astype(BF16))
    rank1 = jnp.sum(jnp.where(hit1, before, 0.0), axis=1, keepdims=True)
    rank2 = jnp.sum(jnp.where(hit2, before, 0.0), axis=1, keepdims=True)
    run = run_ref[...] + jnp.sum(onehot, axis=0, keepdims=True)
    run_ref[...] = run
    cnt_ref[...] = run

    vals = (i1 - N_GROUPS, i2 - N_GROUPS, gate1, gate2, rank1, rank2)
    meta = jnp.zeros((tm, LANES), F32)
    for pos, val in enumerate(vals):
        meta = jnp.where(lane == pos, val.astype(F32), meta)
    meta_ref[...] = meta
    route_ref[...] = meta.T[:SUBLANES]


def _outproj(sb_out, rw_out, x2, w_sb, w_rw, gain, w_router, tm):
    n, d = x2.shape
    rows = lambda c: pl.BlockSpec((tm, c), lambda i: (i, 0))
    full = lambda a: pl.BlockSpec(a.shape, lambda i: (0,) * a.ndim)
    return pl.pallas_call(
        functools.partial(_outproj_kernel, tm=tm),
        grid=(n // tm,),
        in_specs=[rows(sb_out.shape[1]), rows(rw_out.shape[1]), rows(d), full(w_sb), full(w_rw), full(gain),
                  full(w_router)],
        out_specs=[rows(d), pl.BlockSpec((tm * (d // LANES), LANES), lambda i: (i, 0)), rows(LANES),
                   pl.BlockSpec((SUBLANES, tm), lambda i: (0, i)), pl.BlockSpec((1, LANES), lambda i: (0, 0))],
        out_shape=[jax.ShapeDtypeStruct((n, d), F32), jax.ShapeDtypeStruct((n * (d // LANES), LANES), F32),
                   jax.ShapeDtypeStruct((n, LANES), F32), jax.ShapeDtypeStruct((SUBLANES, n), F32),
                   jax.ShapeDtypeStruct((1, LANES), F32)],
        scratch_shapes=[pltpu.VMEM((1, LANES), F32)],
        compiler_params=pltpu.CompilerParams(dimension_semantics=("arbitrary",),
                                             vmem_limit_bytes=VMEM_LIMIT),
        name="outproj",
    )(sb_out, rw_out, x2, w_sb, w_rw, gain, w_router)


def _start_tile_gather(src_ref, idx_ref, dst_ref, dst_base, sem, n_rows):
    def issue(r, _):
        _tile_copy(src_ref, idx_ref[0, r], dst_ref, dst_base + r, sem).start()
        return 0

    lax.fori_loop(0, n_rows, issue, 0, unroll=8)


def _wait_tile_gather(src_ref, dst_ref, sem, n_rows):
    def drain(r, _):
        _tile_copy(src_ref, 0, dst_ref, 0, sem).wait()
        return 0

    lax.fori_loop(0, n_rows, drain, 0, unroll=8)


def _srcmap_kernel(dest_ref, seg_ref, src_ref):
    def put(tok, _):
        for kk in range(TOP_K):
            src_ref[dest_ref[TOP_K * tok + kk]] = tok
        return 0

    lax.fori_loop(0, dest_ref.shape[0] // TOP_K, put, 0, unroll=8)

    def pad(e, _):
        first = seg_ref[0, e]

        def clear(r, _):
            src_ref[first + r] = 0
            return 0

        lax.fori_loop(seg_ref[1, e], seg_ref[2, e], clear, 0)
        return 0

    lax.fori_loop(0, seg_ref.shape[1], pad, 0)


def _srcmap(dest_flat, segments, n_rows):
    return pl.pallas_call(
        _srcmap_kernel,
        in_specs=[pl.BlockSpec(memory_space=pltpu.SMEM), pl.BlockSpec(memory_space=pltpu.SMEM)],
        out_specs=pl.BlockSpec(memory_space=pltpu.SMEM),
        out_shape=jax.ShapeDtypeStruct((n_rows,), jnp.int32),
        name="srcmap",
    )(dest_flat, segments)


def _experts_kernel(eid_ref, used_ref, cur_ref, nxt_ref, h_ref, wg_ref, wu_ref, wd_ref, o_ref,
                    x_ref, wgb_ref, wub_ref, wdb_ref, sems):
    i = pl.program_id(0)
    used = used_ref[0]
    slot = i % 2
    rows = EXPERT_ROWS

    @pl.when(i == 0)
    def _():
        _start_tile_gather(h_ref, cur_ref, x_ref, 0, sems.at[0], rows)

    @pl.when((i == 0) | (eid_ref[i] != eid_ref[jnp.maximum(i - 1, 0)]))
    def _():
        wgb_ref[...] = wg_ref[...].astype(BF16)
        wub_ref[...] = wu_ref[...].astype(BF16)
        wdb_ref[...] = wd_ref[...].astype(BF16)

    @pl.when(i < used)
    def _():
        _wait_tile_gather(h_ref, x_ref, sems.at[slot], rows)
        xb = _load_row_tiles(x_ref, slot * (rows * SUBLANES), rows, SUBLANES).astype(BF16)

        de, d = wdb_ref.shape
        pieces = [(wgb_ref, c) for c in range(de // MXU_DIM)] + [(wub_ref, c) for c in range(de // MXU_DIM)] + \
                 [(wdb_ref, c) for c in range(d // MXU_DIM)]
        per = rows // len(pieces)

        def prefetch(piece):
            for r in range(piece * per, (piece + 1) * per):
                _tile_copy(h_ref, nxt_ref[0, r], x_ref, (1 - slot) * rows + r, sems.at[1 - slot]).start()

        cols = lambda c: slice(c * MXU_DIM, (c + 1) * MXU_DIM)
        up = []
        for piece, (w_ref, c) in enumerate(pieces[:2 * (de // MXU_DIM)]):
            prefetch(piece)
            up.append(_mm(xb, w_ref[:, cols(c)]))
        hg = jnp.concatenate(up[:de // MXU_DIM], axis=1)
        hu = jnp.concatenate(up[de // MXU_DIM:], axis=1)
        act = (hg * _sigmoid(hg) * hu).astype(BF16)
        down = []
        for piece, (w_ref, c) in enumerate(pieces[2 * (de // MXU_DIM):], start=2 * (de // MXU_DIM)):
            prefetch(piece)
            down.append(_mm(act, w_ref[:, cols(c)]))
        _store_row_tiles(o_ref, 0, jnp.concatenate(down, axis=1))

    @pl.when(i == used - 1)
    def _():
        _wait_tile_gather(h_ref, x_ref, sems.at[1 - slot], rows)

    @pl.when(i >= used)
    def _():
        o_ref[...] = jnp.zeros_like(o_ref)


def _experts(blk_eid, n_used, src3, h2_tiles, w_g, w_u, w_d):
    n_blocks = src3.shape[0]
    d, de = w_g.shape[1], w_g.shape[2]
    assert d == SUBLANES * LANES
    w_map = lambda i, eid, used: (eid[i], 0, 0)
    idx_spec = lambda ahead: pl.BlockSpec(
        (None, 1, EXPERT_ROWS), lambda i, eid, used: (jnp.minimum(i + ahead, used[0] - 1), 0, 0),
        memory_space=pltpu.SMEM)
    grid_spec = pltpu.PrefetchScalarGridSpec(
        num_scalar_prefetch=2,
        grid=(n_blocks,),
        in_specs=[idx_spec(0), idx_spec(1), pl.BlockSpec(memory_space=pl.ANY),
                  pl.BlockSpec((None, d, de), w_map), pl.BlockSpec((None, d, de), w_map),
                  pl.BlockSpec((None, de, d), w_map)],
        out_specs=pl.BlockSpec((EXPERT_ROWS * SUBLANES, LANES), lambda i, eid, used: (i, 0)),
        scratch_shapes=[pltpu.VMEM((2 * EXPERT_ROWS * SUBLANES, LANES), F32),
                        pltpu.VMEM((d, de), BF16), pltpu.VMEM((d, de), BF16), pltpu.VMEM((de, d), BF16),
                        pltpu.SemaphoreType.DMA((2,))],
    )
    return pl.pallas_call(
        _experts_kernel,
        grid_spec=grid_spec,
        out_shape=jax.ShapeDtypeStruct((n_blocks * EXPERT_ROWS * SUBLANES, LANES), F32),
        compiler_params=pltpu.CompilerParams(dimension_semantics=("arbitrary",),
                                             vmem_limit_bytes=VMEM_LIMIT),
        name="experts",
    )(blk_eid, n_used, src3, src3, h2_tiles, w_g, w_u, w_d)


def _combine_kernel(dest_ref, rows_ref, x1_ref, meta_ref, o_ref, got_ref, sem, *, tm):
    def issue(t, _):
        for kk in range(TOP_K):
            _tile_copy(rows_ref, dest_ref[0, TOP_K * t + kk], got_ref, kk * tm + t, sem).start(priority=kk)
        return 0

    lax.fori_loop(0, tm, issue, 0, unroll=4)
    _wait_tile_gather(rows_ref, got_ref, sem, TOP_K * tm)
    meta = meta_ref[...]
    y = x1_ref[...]
    for kk in range(TOP_K):
        y = y + _load_row_tiles(got_ref, kk * tm * SUBLANES, tm, SUBLANES) * meta[:, 2 + kk:3 + kk]
    o_ref[...] = y


def _combine(dest3, exp_out_tiles, x1, meta, tm):
    n, d = x1.shape
    rows = lambda c: pl.BlockSpec((tm, c), lambda i: (i, 0))
    return pl.pallas_call(
        functools.partial(_combine_kernel, tm=tm),
        grid=(n // tm,),
        in_specs=[pl.BlockSpec((None, 1, TOP_K * tm), lambda i: (i, 0, 0), memory_space=pltpu.SMEM),
                  pl.BlockSpec(memory_space=pl.ANY), rows(d), rows(LANES)],
        out_specs=rows(d),
        out_shape=jax.ShapeDtypeStruct((n, d), F32),
        scratch_shapes=[pltpu.VMEM((TOP_K * tm * SUBLANES, LANES), F32), pltpu.SemaphoreType.DMA(())],
        compiler_params=pltpu.CompilerParams(dimension_semantics=("arbitrary",),
                                             vmem_limit_bytes=VMEM_LIMIT),
        name="combine",
    )(dest3, exp_out_tiles, x1, meta)


def _layer(x, norm1_gain, w_in, sb_q_gain, sb_k_gain, rw_shift_mu, rw_w0, rw_w_up, rw_a0, rw_a_up, rw_g_up,
           rw_k_k, rw_k_a, rw_r_k, rw_ln_w, rw_ln_b, w_out, norm2_gain, w_router_group, w_router_expert,
           w_exp_gate, w_exp_up, w_exp_down):
    bsz, s, d = x.shape
    n = bsz * s
    sbw = d // 2
    rww = d - sbw
    n_decay, n_aaa = rw_w_up.shape[0], rw_a_up.shape[0]
    n_lora = n_decay + n_aaa + rw_g_up.shape[0]
    tm = min(256, n)
    row = lambda a: a.reshape(1, -1).astype(F32)

    x2 = x.reshape(n, d)
    w_bf = w_in.astype(BF16)
    sb, rw, lo = _inproj(x2, row(norm1_gain), w_bf[:, :3 * sbw], w_bf[:, 3 * sbw:3 * sbw + 3 * rww],
                         w_bf[:, 3 * sbw + 3 * rww:], tm)
    heads = lambda gvec: jnp.tile(row(gvec), (1, sbw // HEAD_DIM))
    sb_out = _sbattn(sb.reshape(bsz, s, 3 * sbw), heads(sb_q_gain), heads(sb_k_gain), min(SB_BLOCK, s))

    pad_rows = lambda w_up, start: jnp.zeros((n_lora, rww), BF16).at[start:start + w_up.shape[0]].set(
        w_up.astype(BF16))
    mu = row(rw_shift_mu)
    rw_out = _rwscan(rw.reshape(bsz, s, 3 * rww), lo.reshape(bsz, s, n_lora), mu[:, :3 * rww], mu[:, 3 * rww:],
                     row(rw_w0), row(rw_a0), row(rw_k_k), row(rw_k_a),
                     pad_rows(rw_w_up, 0), pad_rows(rw_a_up, n_decay), pad_rows(rw_g_up, n_decay + n_aaa),
                     row(rw_r_k), row(rw_ln_w), row(rw_ln_b), min(256, s))

    w_out_bf = w_out.astype(BF16)
    w_router = jnp.zeros((d, LANES), F32).at[:, :N_GROUPS].set(w_router_group).at[
        :, N_GROUPS:N_GROUPS + N_EXPERTS].set(w_router_expert)
    x1, h2, meta, route, counts = _outproj(sb_out.reshape(n, sbw), rw_out.reshape(n, rww), x2, w_out_bf[:sbw],
                                    w_out_bf[sbw:], row(norm2_gain), w_router, tm)

    counts = counts[0, N_GROUPS:N_GROUPS + N_EXPERTS].astype(jnp.int32)
    padded = (counts + EXPERT_ROWS - 1) // EXPERT_ROWS * EXPERT_ROWS
    pad_end = jnp.cumsum(padded)
    pad_start = pad_end - padded
    eid = route[0:TOP_K].astype(jnp.int32)
    rank = route[4:4 + TOP_K].astype(jnp.int32)
    dest = (pad_start[eid] + rank).T
    dest3 = dest.reshape(n // tm, 1, TOP_K * tm)
    n_rows = n * TOP_K + N_EXPERTS * EXPERT_ROWS
    n_blocks = n_rows // EXPERT_ROWS
    block_start = jnp.arange(n_blocks, dtype=jnp.int32) * EXPERT_ROWS
    blk_eid = jnp.minimum(jnp.sum((pad_end[None, :] <= block_start[:, None]).astype(jnp.int32), axis=1),
                          N_EXPERTS - 1)
    n_used = (pad_end[-1:] // EXPERT_ROWS).astype(jnp.int32)

    tail = jnp.zeros((1,), jnp.int32)
    segments = jnp.stack([jnp.concatenate([pad_start, pad_end[-1:]]), jnp.concatenate([counts, tail]),
                          jnp.concatenate([padded, n_rows - pad_end[-1:]])]).astype(jnp.int32)
    src3 = _srcmap(dest.reshape(-1), segments, n_rows).reshape(n_blocks, 1, EXPERT_ROWS)

    exp_out = _experts(blk_eid, n_used, src3, h2, w_exp_gate, w_exp_up, w_exp_down)
    return _combine(dest3, exp_out, x1, meta, tm).reshape(bsz, s, d)


def kernel(x, norm1_gain, w_in, sb_q_gain, sb_k_gain, rw_shift_mu, rw_w0, rw_w_up, rw_a0, rw_a_up, rw_g_up,
           rw_k_k, rw_k_a, rw_r_k, rw_ln_w, rw_ln_b, w_out, norm2_gain, w_router_group, w_router_expert,
           w_exp_gate, w_exp_up, w_exp_down):
    params = (norm1_gain, w_in, sb_q_gain, sb_k_gain, rw_shift_mu, rw_w0, rw_w_up, rw_a0, rw_a_up, rw_g_up,
              rw_k_k, rw_k_a, rw_r_k, rw_ln_w, rw_ln_b, w_out, norm2_gain, w_router_group, w_router_expert,
              w_exp_gate, w_exp_up, w_exp_down)
    for layer in range(norm1_gain.shape[0]):
        x = _layer(x, *(p[layer] for p in params))
    return x
```

```python
import functools

import jax
import jax.numpy as jnp
from jax import lax
from jax.experimental import pallas as pl
from jax.experimental.pallas import tpu as pltpu

F32 = jnp.float32
BF16 = jnp.bfloat16

HEAD_DIM = 64
NORM_EPS = 1e-6
LN_X_EPS = 64e-5
N_GROUPS = 4
EXPERTS_PER_GROUP = 8
N_EXPERTS = N_GROUPS * EXPERTS_PER_GROUP
TOP_K = 2

LANES = 128
SUBLANES = 8
MXU_DIM = 256
RW_CHUNK = 64
IN_TILE = 512
RW_TILE = 256
SB_BLOCK = 256
EXPERT_ROWS = 256
UNDERFLOW_LOG = -110.0
VMEM_LIMIT = 56 * 1024 * 1024


def _nt(a, b):
    return lax.dot_general(a, b, (((1,), (1,)), ((), ())), preferred_element_type=F32)


def _tn(a, b):
    return lax.dot_general(a, b, (((0,), (0,)), ((), ())), preferred_element_type=F32)


def _mm(a, b):
    return jnp.dot(a, b, preferred_element_type=F32)


def _split2(x):
    hi = x.astype(BF16)
    lo = (x - hi.astype(F32)).astype(BF16)
    return hi, lo


def _split3(x):
    h1 = x.astype(BF16)
    r1 = x - h1.astype(F32)
    h2 = r1.astype(BF16)
    h3 = (r1 - h2.astype(F32)).astype(BF16)
    return h1, h2, h3


def _mm_exact_rhs(x, m):
    hi, lo = _split2(x)
    if 2 * x.shape[1] <= MXU_DIM:
        return _mm(jnp.concatenate([hi, lo], axis=1), jnp.concatenate([m, m], axis=0))
    return _mm(hi, m) + _mm(lo, m)


def _head_ones(width):
    r = lax.broadcasted_iota(jnp.int32, (width, width), 0) // HEAD_DIM
    c = lax.broadcasted_iota(jnp.int32, (width, width), 1) // HEAD_DIM
    return jnp.where(r == c, 1.0, 0.0).astype(BF16)


def _softplus(z):
    return jnp.maximum(z, 0.0) + jnp.log(1.0 + jnp.exp(-jnp.abs(z)))


def _sigmoid(z):
    return 1.0 / (1.0 + jnp.exp(-z))


def _store_row_tiles(ref, base, x):
    pieces = x.shape[1] // LANES
    for s in range(pieces):
        ref[pl.ds(base + s, x.shape[0], stride=pieces), :] = x[:, s * LANES:(s + 1) * LANES]


def _load_row_tiles(ref, base, rows, pieces):
    return jnp.concatenate([ref[pl.ds(base + s, rows, stride=pieces), :] for s in range(pieces)], axis=1)


def _tile_copy(src_ref, src_row, dst_ref, dst_row, sem):
    src = src_ref.at[pl.ds(pl.multiple_of(src_row * SUBLANES, SUBLANES), SUBLANES)]
    dst = dst_ref.at[pl.ds(pl.multiple_of(dst_row * SUBLANES, SUBLANES), SUBLANES)]
    return pltpu.make_async_copy(src, dst, sem)


def _head_rms_norm(x, gain):
    ones = _head_ones(MXU_DIM)
    groups = [x[:, c:c + MXU_DIM] for c in range(0, x.shape[1], MXU_DIM)]
    ms = jnp.concatenate([_mm_exact_rhs(g * g, ones) for g in groups], axis=1) * (1.0 / HEAD_DIM)
    return x * lax.rsqrt(ms + NORM_EPS) * gain


def _inproj_kernel(x_ref, g_ref, wsb_ref, wrw_ref, wlo_ref, gq_ref, gk_ref, sb_ref, rw_ref, lo_ref):
    x = x_ref[...]
    ms = jnp.mean(x * x, axis=-1, keepdims=True)
    h = (x * lax.rsqrt(ms + NORM_EPS) * g_ref[...]).astype(BF16)
    rw_ref[...] = _mm(h, wrw_ref[...])
    lo_ref[...] = _mm(h, wlo_ref[...])
    sb = _mm(h, wsb_ref[...])
    width = gq_ref.shape[1]
    sb_ref[:, :width] = (_head_rms_norm(sb[:, :width], gq_ref[...]) * (HEAD_DIM ** -0.5)).astype(BF16)
    sb_ref[:, width:2 * width] = _head_rms_norm(sb[:, width:2 * width], gk_ref[...]).astype(BF16)
    sb_ref[:, 2 * width:] = sb[:, 2 * width:].astype(BF16)


def _inproj(x2, gain, w_sb, w_rw, w_lo, gq, gk, tm):
    n, d = x2.shape
    full = lambda a: pl.BlockSpec(a.shape, lambda i: (0,) * a.ndim)
    rows = lambda c: pl.BlockSpec((tm, c), lambda i: (i, 0))
    return pl.pallas_call(
        _inproj_kernel,
        grid=(n // tm,),
        in_specs=[rows(d), full(gain), full(w_sb), full(w_rw), full(w_lo), full(gq), full(gk)],
        out_specs=[rows(w_sb.shape[1]), rows(w_rw.shape[1]), rows(w_lo.shape[1])],
        out_shape=[jax.ShapeDtypeStruct((n, w_sb.shape[1]), BF16), jax.ShapeDtypeStruct((n, w_rw.shape[1]), F32),
                   jax.ShapeDtypeStruct((n, w_lo.shape[1]), F32)],
        compiler_params=pltpu.CompilerParams(dimension_semantics=("arbitrary",),
                                             vmem_limit_bytes=VMEM_LIMIT),
        name="inproj",
    )(x2, gain, w_sb, w_rw, w_lo, gq, gk)


def _sbattn_kernel(q_ref, k_ref, v_ref, o_ref, *, blk, pairs):
    qb = pl.program_id(1)
    lane = lax.broadcasted_iota(jnp.int32, (1, pairs * LANES), 1)
    first = (lane % LANES) < HEAD_DIM
    qn = q_ref[...].astype(F32)
    q_first = jnp.where(first, qn, 0.0).astype(BF16)
    q_second = jnp.where(first, 0.0, qn).astype(BF16)
    lanes_of = lambda p: slice(p * LANES, (p + 1) * LANES)
    qq = [jnp.concatenate([q_first[:, lanes_of(p)], q_second[:, lanes_of(p)]], axis=0) for p in range(pairs)]

    jj = lax.broadcasted_iota(jnp.int32, (blk, blk), 0)
    ss = lax.broadcasted_iota(jnp.int32, (blk, blk), 1)
    uu = jnp.where(jj >= ss, 1.0, 0.0).astype(BF16)

    rows = 2 * blk
    tt = lax.broadcasted_iota(jnp.int32, (pairs * rows, blk), 0) % blk
    sk = lax.broadcasted_iota(jnp.int32, (pairs * rows, blk), 1)
    causal = sk < tt

    def step(j, c, acc, diagonal):
        start = pl.multiple_of(j * blk, blk)
        z = jnp.concatenate([_nt(qq[p], k_ref[pl.ds(start, blk), lanes_of(p)]) for p in range(pairs)], axis=0)
        lk = -_softplus(z)
        if diagonal:
            lk = jnp.where(causal, lk, 0.0)
        sums = _mm_exact_rhs(lk, uu)
        w = jnp.exp(z + sums + c)
        if diagonal:
            w = jnp.where(causal, w, 0.0)
        w = w.astype(BF16)
        pv = jnp.concatenate([_mm(w[p * rows:(p + 1) * rows], v_ref[pl.ds(start, blk), lanes_of(p)])
                              for p in range(pairs)], axis=0)
        c = c + sums[:, 0:1]
        return c, acc + pv, jnp.max(c) > UNDERFLOW_LOG

    carry = (qb - 1,) + step(qb, jnp.zeros((pairs * rows, 1), F32), jnp.zeros((pairs * rows, LANES), F32), True)
    _, _, acc, _ = lax.while_loop(lambda cr: (cr[0] >= 0) & cr[3],
                                  lambda cr: (cr[0] - 1,) + step(cr[0], cr[1], cr[2], False), carry)
    first_pair = first[:, :LANES]
    for p in range(pairs):
        o_ref[:, lanes_of(p)] = jnp.where(first_pair, acc[p * rows:p * rows + blk], acc[p * rows + blk:(p + 1) * rows])


def _sbattn(sb3, blk):
    b, s, w3 = sb3.shape
    width = w3 // 3
    return pl.pallas_call(
        functools.partial(_sbattn_kernel, blk=blk, pairs=width // LANES),
        grid=(b, s // blk),
        in_specs=[pl.BlockSpec((None, blk, width), lambda bi, qb: (bi, qb, 0)),
                  pl.BlockSpec((None, s, width), lambda bi, qb: (bi, 0, 1)),
                  pl.BlockSpec((None, s, width), lambda bi, qb: (bi, 0, 2))],
        out_specs=pl.BlockSpec((None, blk, width), lambda bi, qb: (bi, qb, 0)),
        out_shape=jax.ShapeDtypeStruct((b, s, width), F32),
        compiler_params=pltpu.CompilerParams(dimension_semantics=("arbitrary",) * 2,
                                             vmem_limit_bytes=VMEM_LIMIT),
        name="sbattn",
    )(sb3, sb3, sb3)


def _token_shift(p, prev_ref, mu, first_tile):
    rows = lax.broadcasted_iota(jnp.int32, (p.shape[0], 1), 0)
    last_prev = jnp.where(first_tile, 0.0, prev_ref[7:8, :])
    prev = jnp.where(rows == 0, last_prev, pltpu.roll(p, 1, axis=0))
    return p + (prev - p) * mu


def _rwprep_stage(rw_ref, rwp_ref, lo_ref, lop_ref, mur_ref, mul_ref, w0_ref, a0_ref, kk_ref, ka_ref,
                  wup_ref, aup_ref, gup_ref, *, width):
    first_tile = pl.program_id(1) == 0
    pf = _token_shift(rw_ref[...], rwp_ref, mur_ref[...], first_tile)
    lf = _token_shift(lo_ref[...], lop_ref, mul_ref[...], first_tile)
    r = pf[:, :width]
    k = pf[:, width:2 * width]
    v = pf[:, 2 * width:]
    w = -_softplus(-(w0_ref[...] + _mm(jnp.tanh(lf).astype(BF16), wup_ref[...]))) - 0.5
    lr = _sigmoid(a0_ref[...] + _mm(lf.astype(BF16), aup_ref[...]))
    kk = k * kk_ref[...]
    ss = _mm_exact_rhs(kk * kk, _head_ones(width))
    kk = kk / jnp.maximum(jnp.sqrt(ss), 1e-12)
    log_decay = -jnp.exp(w)
    gate = _mm(_sigmoid(lf).astype(BF16), gup_ref[...])
    return r, log_decay, k * (1.0 + (lr - 1.0) * ka_ref[...]), v, -kk, kk * lr, gate


def _rwscan_kernel(rw_ref, rwp_ref, lo_ref, lop_ref, mur_ref, mul_ref, w0_ref, a0_ref, kk_ref, ka_ref,
                   wup_ref, aup_ref, gup_ref, rk_ref, lnw_ref, lnb_ref,
                   o_ref, st_ref, y_ref, *, tb, width):
    r_all, lw_all, k_all, v_all, a_all, b_all, g_all = _rwprep_stage(
        rw_ref, rwp_ref, lo_ref, lop_ref, mur_ref, mul_ref, w0_ref, a0_ref, kk_ref, ka_ref,
        wup_ref, aup_ref, gup_ref, width=width)

    t = RW_CHUNK
    n = LANES
    m = MXU_DIM
    groups = width // m
    pairs = width // n

    @pl.when(pl.program_id(1) == 0)
    def _():
        st_ref[...] = jnp.zeros_like(st_ref)

    ri = lax.broadcasted_iota(jnp.int32, (2 * n, 2 * n), 0)
    ci = lax.broadcasted_iota(jnp.int32, (2 * n, 2 * n), 1)
    same = ((ri % n) // HEAD_DIM) == ((ci % n) // HEAD_DIM)
    gmask = same & ((ci % HEAD_DIM) < (ri % HEAD_DIM) + jnp.where(ri >= n, 1, 0))
    rn = lax.broadcasted_iota(jnp.int32, (n, n), 0)
    cn = lax.broadcasted_iota(jnp.int32, (n, n), 1)
    bdmask = jnp.where((rn // HEAD_DIM) == (cn // HEAD_DIM), 1.0, 0.0)
    eye = jnp.where(rn == cn, 1.0, 0.0)
    ti = lax.broadcasted_iota(jnp.int32, (t, 4 * t), 0)
    si = lax.broadcasted_iota(jnp.int32, (t, 4 * t), 1)
    tri3 = jnp.where((si % t <= ti) & (si < 3 * t), 1.0, 0.0).astype(BF16)
    right_half = jnp.where(lax.broadcasted_iota(jnp.int32, (n, 2 * n), 1) >= n, 1.0, 0.0)
    cat = jnp.concatenate

    def blockdiag32(x):
        return cat([x, x], axis=0) * bdmask

    def prepare(rows, cols):
        r = r_all[rows, cols]
        lw = lw_all[rows, cols]
        k = k_all[rows, cols]
        v = v_all[rows, cols]
        a = a_all[rows, cols]
        b = b_all[rows, cols]

        l1, l2, l3 = _split3(lw)
        cum = _mm(tri3, cat([l1, l2, l3, l1], axis=0))
        yield
        cum_end = cum[t - 1:t, :]
        p_inv = jnp.exp(-cum)
        p_rest = jnp.exp(cum_end - cum)
        r32 = blockdiag32(r * jnp.exp(cum))
        r_bd = r32.astype(BF16)
        a_bd = blockdiag32(a * jnp.exp(cum - lw)).astype(BF16)
        k_bd = blockdiag32(k * p_inv).astype(BF16)
        b_bd = blockdiag32(b * p_inv).astype(BF16)
        kd_bd = blockdiag32(k * p_rest).astype(BF16)
        bd_bd = blockdiag32(b * p_rest).astype(BF16)
        v32 = blockdiag32(v)
        v_bd = v32.astype(BF16)
        zero_v = (cat([v32, v32], axis=1) * right_half).astype(BF16)

        g = jnp.where(gmask, _nt(cat([a_bd, r_bd], axis=0), cat([b_bd, k_bd], axis=0)), 0.0)
        yield
        aab = g[:n, :n]
        aak = g[:n, n:].astype(BF16)
        brb_brk = g[n:, :].astype(BF16)

        x = eye + aab
        p = aab.astype(BF16)
        p = _mm(p, p).astype(BF16)
        av = _mm(aak, v_bd).astype(BF16)
        yield
        steps = (t - 1).bit_length() - 1
        for i in range(steps):
            if i + 1 < steps:
                px_pp = _mm(p, cat([x.astype(BF16), p], axis=1))
                x = x + px_pp[:, :n]
                p = px_pp[:, n:].astype(BF16)
            else:
                x = x + _mm(p, x.astype(BF16))
            yield
        tinv = x.astype(BF16)

        w_u0 = _mm(tinv, cat([a_bd, av], axis=1)).astype(BF16)
        yield
        stack = cat([w_u0, zero_v], axis=0)
        q_y0 = _mm(brb_brk, stack)
        m_c0 = _tn(cat([bd_bd, kd_bd], axis=0), stack)
        yield
        q_mat = (r32 + q_y0[:, :n]).astype(BF16)
        m_mat = (eye * jnp.exp(cum_end) + m_c0[:, :n]).astype(BF16)
        lhs = cat([cat([q_mat, q_mat], axis=1), cat([m_mat, m_mat], axis=1)], axis=0)
        return lhs, q_y0[:, n:], m_c0[:, n:]

    def lockstep(generators):
        results = [None] * len(generators)
        live = list(enumerate(generators))
        while live:
            still = []
            for idx, gen in live:
                try:
                    next(gen)
                    still.append((idx, gen))
                except StopIteration as done:
                    results[idx] = done.value
            live = still
        return results

    lane_cols = [slice(pr * n, (pr + 1) * n) for pr in range(pairs)]
    time_rows = [slice(ch * t, (ch + 1) * t) for ch in range(tb // t)]
    prepared = lockstep([prepare(rows, cols) for rows in time_rows for cols in lane_cols])
    states = [st_ref[pr] for pr in range(pairs)]
    for ch, rows in enumerate(time_rows):
        for pr, cols in enumerate(lane_cols):
            lhs, y0, c0 = prepared[ch * pairs + pr]
            y_s = _mm(lhs, cat(_split2(states[pr]), axis=0))
            y = y_s[:n] + y0
            y_ref[rows, cols] = y[:t] + y[t:]
            states[pr] = y_s[n:] + c0
    for pr in range(pairs):
        st_ref[pr] = states[pr]

    ones = _head_ones(m)
    inv = 1.0 / HEAD_DIM
    for grp in range(groups):
        cols = slice(grp * m, (grp + 1) * m)
        y = y_ref[:, cols]
        mean = _mm_exact_rhs(y, ones) * inv
        d = y - mean
        var = _mm_exact_rhs(d * d, ones) * inv
        yn = d * lax.rsqrt(var + LN_X_EPS) * lnw_ref[:, cols] + lnb_ref[:, cols]
        bonus = _mm_exact_rhs(r_all[:, cols] * k_all[:, cols] * rk_ref[:, cols], ones) * v_all[:, cols]
        o_ref[:, cols] = (yn + bonus) * g_all[:, cols]


def _rwscan(rw3, lo3, mu_rw, mu_lo, w0, a0, k_k, k_a, wup, aup, gup, r_k, ln_w, ln_b, tb):
    bsz, s, w3 = rw3.shape
    width = w3 // 3
    nlo = lo3.shape[2]
    cur = lambda c: pl.BlockSpec((None, tb, c), lambda bi, i: (bi, i, 0))
    prev = lambda c: pl.BlockSpec((None, SUBLANES, c),
                                  lambda bi, i: (bi, jnp.maximum(i * (tb // SUBLANES) - 1, 0), 0))
    full = lambda x: pl.BlockSpec(x.shape, lambda bi, i: (0,) * x.ndim)
    params = (mu_rw, mu_lo, w0, a0, k_k, k_a, wup, aup, gup, r_k, ln_w, ln_b)
    return pl.pallas_call(
        functools.partial(_rwscan_kernel, tb=tb, width=width),
        grid=(bsz, s // tb),
        in_specs=[cur(w3), prev(w3), cur(nlo), prev(nlo)] + [full(p) for p in params],
        out_specs=cur(width),
        out_shape=jax.ShapeDtypeStruct((bsz, s, width), F32),
        scratch_shapes=[pltpu.VMEM((width // LANES, LANES, LANES), F32)] + [pltpu.VMEM((tb, width), F32)],
        compiler_params=pltpu.CompilerParams(dimension_semantics=("arbitrary",) * 2,
                                             vmem_limit_bytes=VMEM_LIMIT),
        name="rwscan",
    )(rw3, rw3, lo3, lo3, *params)


def _outproj_kernel(sb_ref, rw_ref, x_ref, wsb_ref, wrw_ref, g_ref, wr_ref,
                    x1_ref, h2_ref, meta_ref, route_ref, cnt_ref, run_ref, *, tm):
    @pl.when(pl.program_id(0) == 0)
    def _():
        run_ref[...] = jnp.zeros_like(run_ref)

    x1 = x_ref[...] + _mm(sb_ref[...].astype(BF16), wsb_ref[...]) + _mm(rw_ref[...].astype(BF16), wrw_ref[...])
    x1_ref[...] = x1
    ms = jnp.mean(x1 * x1, axis=-1, keepdims=True)
    h2 = x1 * lax.rsqrt(ms + NORM_EPS) * g_ref[...]
    _store_row_tiles(h2_ref, 0, h2)

    h_hi, h_lo = _split2(h2)
    w_hi, w_lo = _split2(wr_ref[...])
    lg = _mm(h_hi, w_hi) + _mm(h_hi, w_lo) + _mm(h_lo, w_hi)

    lane = lax.broadcasted_iota(jnp.int32, (tm, LANES), 1).astype(F32)
    neg = -jnp.inf
    big = float(LANES)
    is_group = lane < N_GROUPS
    gl = jnp.where(is_group, lg, neg)
    gmax = jnp.max(gl, axis=1, keepdims=True)
    gidx = jnp.min(jnp.where(gl == gmax, lane, big), axis=1, keepdims=True)
    group_gate = 1.0 / jnp.sum(jnp.where(is_group, jnp.exp(lg - gmax), 0.0), axis=1, keepdims=True)
    lo_lane = N_GROUPS + EXPERTS_PER_GROUP * gidx
    el = jnp.where((lane >= lo_lane) & (lane < lo_lane + EXPERTS_PER_GROUP), lg, neg)
    m1 = jnp.max(el, axis=1, keepdims=True)
    i1 = jnp.min(jnp.where(el == m1, lane, big), axis=1, keepdims=True)
    el2 = jnp.where(lane == i1, neg, el)
    m2 = jnp.max(el2, axis=1, keepdims=True)
    i2 = jnp.min(jnp.where(el2 == m2, lane, big), axis=1, keepdims=True)
    p2 = jnp.exp(m2 - m1)
    gate1 = group_gate / (1.0 + p2)
    gate2 = group_gate * p2 / (1.0 + p2)

    hit1 = lane == i1
    hit2 = lane == i2
    onehot = jnp.where(hit1 | hit2, 1.0, 0.0)
    rr = lax.broadcasted_iota(jnp.int32, (tm, tm), 0)
    cc = lax.broadcasted_iota(jnp.int32, (tm, tm), 1)
    below = jnp.where(cc < rr, 1.0, 0.0).astype(BF16)
    before = run_ref[...] + _mm(below, onehot.astype(BF16))
    rank1 = jnp.sum(jnp.where(hit1, before, 0.0), axis=1, keepdims=True)
    rank2 = jnp.sum(jnp.where(hit2, before, 0.0), axis=1, keepdims=True)
    run = run_ref[...] + jnp.sum(onehot, axis=0, keepdims=True)
    run_ref[...] = run
    cnt_ref[...] = run

    vals = (i1 - N_GROUPS, i2 - N_GROUPS, gate1, gate2, rank1, rank2)
    meta = jnp.zeros((tm, LANES), F32)
    for pos, val in enumerate(vals):
        meta = jnp.where(lane == pos, val.astype(F32), meta)
    meta_ref[...] = meta
    route_ref[...] = meta.T[:SUBLANES]


def _outproj(sb_out, rw_out, x2, w_sb, w_rw, gain, w_router, tm):
    n, d = x2.shape
    rows = lambda c: pl.BlockSpec((tm, c), lambda i: (i, 0))
    full = lambda a: pl.BlockSpec(a.shape, lambda i: (0,) * a.ndim)
    return pl.pallas_call(
        functools.partial(_outproj_kernel, tm=tm),
        grid=(n // tm,),
        in_specs=[rows(sb_out.shape[1]), rows(rw_out.shape[1]), rows(d), full(w_sb), full(w_rw), full(gain),
                  full(w_router)],
        out_specs=[rows(d), pl.BlockSpec((tm * (d // LANES), LANES), lambda i: (i, 0)), rows(LANES),
                   pl.BlockSpec((SUBLANES, tm), lambda i: (0, i)), pl.BlockSpec((1, LANES), lambda i: (0, 0))],
        out_shape=[jax.ShapeDtypeStruct((n, d), F32), jax.ShapeDtypeStruct((n * (d // LANES), LANES), F32),
                   jax.ShapeDtypeStruct((n, LANES), F32), jax.ShapeDtypeStruct((SUBLANES, n), F32),
                   jax.ShapeDtypeStruct((1, LANES), F32)],
        scratch_shapes=[pltpu.VMEM((1, LANES), F32)],
        compiler_params=pltpu.CompilerParams(dimension_semantics=("arbitrary",),
                                             vmem_limit_bytes=VMEM_LIMIT),
        name="outproj",
    )(sb_out, rw_out, x2, w_sb, w_rw, gain, w_router)


def _start_tile_gather(src_ref, idx_ref, dst_ref, dst_base, sem, n_rows):
    def issue(r, _):
        _tile_copy(src_ref, idx_ref[0, r], dst_ref, dst_base + r, sem).start()
        return 0

    lax.fori_loop(0, n_rows, issue, 0, unroll=8)


def _wait_tile_gather(src_ref, dst_ref, sem, n_rows):
    def drain(r, _):
        _tile_copy(src_ref, 0, dst_ref, 0, sem).wait()
        return 0

    lax.fori_loop(0, n_rows, drain, 0, unroll=8)


def _srcmap_kernel(dest_ref, seg_ref, src_ref):
    def put(tok, _):
        for kk in range(TOP_K):
            src_ref[dest_ref[TOP_K * tok + kk]] = tok
        return 0

    lax.fori_loop(0, dest_ref.shape[0] // TOP_K, put, 0, unroll=8)

    def pad(e, _):
        first = seg_ref[0, e]

        def clear(r, _):
            src_ref[first + r] = 0
            return 0

        lax.fori_loop(seg_ref[1, e], seg_ref[2, e], clear, 0)
        return 0

    lax.fori_loop(0, seg_ref.shape[1], pad, 0)


def _srcmap(dest_flat, segments, n_rows):
    return pl.pallas_call(
        _srcmap_kernel,
        in_specs=[pl.BlockSpec(memory_space=pltpu.SMEM), pl.BlockSpec(memory_space=pltpu.SMEM)],
        out_specs=pl.BlockSpec(memory_space=pltpu.SMEM),
        out_shape=jax.ShapeDtypeStruct((n_rows,), jnp.int32),
        name="srcmap",
    )(dest_flat, segments)


def _experts_kernel(eid_ref, used_ref, cur_ref, nxt_ref, h_ref, wg_ref, wu_ref, wd_ref, o_ref,
                    x_ref, wgb_ref, wub_ref, wdb_ref, sems):
    i = pl.program_id(0)
    used = used_ref[0]
    slot = i % 2
    rows = EXPERT_ROWS

    @pl.when(i == 0)
    def _():
        _start_tile_gather(h_ref, cur_ref, x_ref, 0, sems.at[0], rows)

    @pl.when(i + 1 < used)
    def _():
        _start_tile_gather(h_ref, nxt_ref, x_ref, (1 - slot) * rows, sems.at[1 - slot], rows)

    @pl.when((i == 0) | (eid_ref[i] != eid_ref[jnp.maximum(i - 1, 0)]))
    def _():
        wgb_ref[...] = wg_ref[...].astype(BF16)
        wub_ref[...] = wu_ref[...].astype(BF16)
        wdb_ref[...] = wd_ref[...].astype(BF16)

    @pl.when(i < used)
    def _():
        _wait_tile_gather(h_ref, x_ref, sems.at[slot], rows)
        xb = _load_row_tiles(x_ref, slot * (rows * SUBLANES), rows, SUBLANES).astype(BF16)
        hg = _mm(xb, wgb_ref[...])
        hu = _mm(xb, wub_ref[...])
        act = hg * _sigmoid(hg) * hu
        _store_row_tiles(o_ref, 0, _mm(act.astype(BF16), wdb_ref[...]))

    @pl.when(i >= used)
    def _():
        o_ref[...] = jnp.zeros_like(o_ref)


def _experts(blk_eid, n_used, src3, h2_tiles, w_g, w_u, w_d):
    n_blocks = src3.shape[0]
    d, de = w_g.shape[1], w_g.shape[2]
    assert d == SUBLANES * LANES
    w_map = lambda i, eid, used: (eid[i], 0, 0)
    idx_spec = lambda ahead: pl.BlockSpec(
        (None, 1, EXPERT_ROWS), lambda i, eid, used: (jnp.minimum(i + ahead, used[0] - 1), 0, 0),
        memory_space=pltpu.SMEM)
    grid_spec = pltpu.PrefetchScalarGridSpec(
        num_scalar_prefetch=2,
        grid=(n_blocks,),
        in_specs=[idx_spec(0), idx_spec(1), pl.BlockSpec(memory_space=pl.ANY),
                  pl.BlockSpec((None, d, de), w_map), pl.BlockSpec((None, d, de), w_map),
                  pl.BlockSpec((None, de, d), w_map)],
        out_specs=pl.BlockSpec((EXPERT_ROWS * SUBLANES, LANES), lambda i, eid, used: (i, 0)),
        scratch_shapes=[pltpu.VMEM((2 * EXPERT_ROWS * SUBLANES, LANES), F32),
                        pltpu.VMEM((d, de), BF16), pltpu.VMEM((d, de), BF16), pltpu.VMEM((de, d), BF16),
                        pltpu.SemaphoreType.DMA((2,))],
    )
    return pl.pallas_call(
        _experts_kernel,
        grid_spec=grid_spec,
        out_shape=jax.ShapeDtypeStruct((n_blocks * EXPERT_ROWS * SUBLANES, LANES), F32),
        compiler_params=pltpu.CompilerParams(dimension_semantics=("arbitrary",),
                                             vmem_limit_bytes=VMEM_LIMIT),
        name="experts",
    )(blk_eid, n_used, src3, src3, h2_tiles, w_g, w_u, w_d)


def _combine_kernel(dest_ref, rows_ref, x1_ref, meta_ref, o_ref, got_ref, sem, *, tm):
    def issue(t, _):
        for kk in range(TOP_K):
            _tile_copy(rows_ref, dest_ref[0, TOP_K * t + kk], got_ref, kk * tm + t, sem).start(priority=kk)
        return 0

    lax.fori_loop(0, tm, issue, 0, unroll=4)
    _wait_tile_gather(rows_ref, got_ref, sem, TOP_K * tm)
    meta = meta_ref[...]
    y = x1_ref[...]
    for kk in range(TOP_K):
        y = y + _load_row_tiles(got_ref, kk * tm * SUBLANES, tm, SUBLANES) * meta[:, 2 + kk:3 + kk]
    o_ref[...] = y


def _combine(dest3, exp_out_tiles, x1, meta, tm):
    n, d = x1.shape
    rows = lambda c: pl.BlockSpec((tm, c), lambda i: (i, 0))
    return pl.pallas_call(
        functools.partial(_combine_kernel, tm=tm),
        grid=(n // tm,),
        in_specs=[pl.BlockSpec((None, 1, TOP_K * tm), lambda i: (i, 0, 0), memory_space=pltpu.SMEM),
                  pl.BlockSpec(memory_space=pl.ANY), rows(d), rows(LANES)],
        out_specs=rows(d),
        out_shape=jax.ShapeDtypeStruct((n, d), F32),
        scratch_shapes=[pltpu.VMEM((TOP_K * tm * SUBLANES, LANES), F32), pltpu.SemaphoreType.DMA(())],
        compiler_params=pltpu.CompilerParams(dimension_semantics=("arbitrary",),
                                             vmem_limit_bytes=VMEM_LIMIT),
        name="combine",
    )(dest3, exp_out_tiles, x1, meta)


def _layer(x, norm1_gain, w_in, sb_q_gain, sb_k_gain, rw_shift_mu, rw_w0, rw_w_up, rw_a0, rw_a_up, rw_g_up,
           rw_k_k, rw_k_a, rw_r_k, rw_ln_w, rw_ln_b, w_out, norm2_gain, w_router_group, w_router_expert,
           w_exp_gate, w_exp_up, w_exp_down):
    bsz, s, d = x.shape
    n = bsz * s
    sbw = d // 2
    rww = d - sbw
    n_decay, n_aaa = rw_w_up.shape[0], rw_a_up.shape[0]
    n_lora = n_decay + n_aaa + rw_g_up.shape[0]
    tm = min(256, n)
    row = lambda a: a.reshape(1, -1).astype(F32)

    x2 = x.reshape(n, d)
    w_bf = w_in.astype(BF16)
    heads = lambda gvec: jnp.tile(row(gvec), (1, sbw // HEAD_DIM))
    sb, rw, lo = _inproj(x2, row(norm1_gain), w_bf[:, :3 * sbw], w_bf[:, 3 * sbw:3 * sbw + 3 * rww],
                         w_bf[:, 3 * sbw + 3 * rww:], heads(sb_q_gain), heads(sb_k_gain), min(IN_TILE, n))
    sb_out = _sbattn(sb.reshape(bsz, s, 3 * sbw), min(SB_BLOCK, s))

    pad_rows = lambda w_up, start: jnp.zeros((n_lora, rww), BF16).at[start:start + w_up.shape[0]].set(
        w_up.astype(BF16))
    mu = row(rw_shift_mu)
    rw_out = _rwscan(rw.reshape(bsz, s, 3 * rww), lo.reshape(bsz, s, n_lora), mu[:, :3 * rww], mu[:, 3 * rww:],
                     row(rw_w0), row(rw_a0), row(rw_k_k), row(rw_k_a),
                     pad_rows(rw_w_up, 0), pad_rows(rw_a_up, n_decay), pad_rows(rw_g_up, n_decay + n_aaa),
                     row(rw_r_k), row(rw_ln_w), row(rw_ln_b), min(RW_TILE, s))

    w_out_bf = w_out.astype(BF16)
    w_router = jnp.zeros((d, LANES), F32).at[:, :N_GROUPS].set(w_router_group).at[
        :, N_GROUPS:N_GROUPS + N_EXPERTS].set(w_router_expert)
    x1, h2, meta, route, counts = _outproj(sb_out.reshape(n, sbw), rw_out.reshape(n, rww), x2, w_out_bf[:sbw],
                                    w_out_bf[sbw:], row(norm2_gain), w_router, tm)

    counts = counts[0, N_GROUPS:N_GROUPS + N_EXPERTS].astype(jnp.int32)
    padded = (counts + EXPERT_ROWS - 1) // EXPERT_ROWS * EXPERT_ROWS
    pad_end = jnp.cumsum(padded)
    pad_start = pad_end - padded
    eid = route[0:TOP_K].astype(jnp.int32)
    rank = route[4:4 + TOP_K].astype(jnp.int32)
    segment_start = jnp.sum(jnp.where(eid[None] == jnp.arange(N_EXPERTS, dtype=jnp.int32)[:, None, None],
                                      pad_start[:, None, None], 0), axis=0)
    dest = (segment_start + rank).T
    dest3 = dest.reshape(n // tm, 1, TOP_K * tm)
    n_rows = n * TOP_K + N_EXPERTS * EXPERT_ROWS
    n_blocks = n_rows // EXPERT_ROWS
    block_start = jnp.arange(n_blocks, dtype=jnp.int32) * EXPERT_ROWS
    blk_eid = jnp.minimum(jnp.sum((pad_end[None, :] <= block_start[:, None]).astype(jnp.int32), axis=1),
                          N_EXPERTS - 1)
    n_used = (pad_end[-1:] // EXPERT_ROWS).astype(jnp.int32)

    tail = jnp.zeros((1,), jnp.int32)
    segments = jnp.stack([jnp.concatenate([pad_start, pad_end[-1:]]), jnp.concatenate([counts, tail]),
                          jnp.concatenate([padded, n_rows - pad_end[-1:]])]).astype(jnp.int32)
    src3 = _srcmap(dest.reshape(-1), segments, n_rows).reshape(n_blocks, 1, EXPERT_ROWS)

    exp_out = _experts(blk_eid, n_used, src3, h2, w_exp_gate, w_exp_up, w_exp_down)
    return _combine(dest3, exp_out, x1, meta, tm).reshape(bsz, s, d)


def kernel(x, norm1_gain, w_in, sb_q_gain, sb_k_gain, rw_shift_mu, rw_w0, rw_w_up, rw_a0, rw_a_up, rw_g_up,
           rw_k_k, rw_k_a, rw_r_k, rw_ln_w, rw_ln_b, w_out, norm2_gain, w_router_group, w_router_expert,
           w_exp_gate, w_exp_up, w_exp_down):
    params = (norm1_gain, w_in, sb_q_gain, sb_k_gain, rw_shift_mu, rw_w0, rw_w_up, rw_a0, rw_a_up, rw_g_up,
              rw_k_k, rw_k_a, rw_r_k, rw_ln_w, rw_ln_b, w_out, norm2_gain, w_router_group, w_router_expert,
              w_exp_gate, w_exp_up, w_exp_down)
    for layer in range(norm1_gain.shape[0]):
        x = _layer(x, *(p[layer] for p in params))
    return x
```

```python
import functools

import jax
import jax.numpy as jnp
from jax import lax
from jax.experimental import pallas as pl
from jax.experimental.pallas import tpu as pltpu

F32 = jnp.float32
BF16 = jnp.bfloat16

HEAD_DIM = 64
NORM_EPS = 1e-6
LN_X_EPS = 64e-5
N_GROUPS = 4
EXPERTS_PER_GROUP = 8
N_EXPERTS = N_GROUPS * EXPERTS_PER_GROUP
TOP_K = 2

LANES = 128
SUBLANES = 8
MXU_DIM = 256
RW_CHUNK = 64
IN_TILE = 512
RW_TILE = 256
RW_WAVE = 16
GATHER_DEPTH = 3
SB_BLOCK = 256
EXPERT_ROWS = 256
UNDERFLOW_LOG = -110.0
VMEM_LIMIT = 56 * 1024 * 1024


def _nt(a, b):
    return lax.dot_general(a, b, (((1,), (1,)), ((), ())), preferred_element_type=F32)


def _tn(a, b):
    return lax.dot_general(a, b, (((0,), (0,)), ((), ())), preferred_element_type=F32)


def _mm(a, b):
    return jnp.dot(a, b, preferred_element_type=F32)


def _split2(x):
    hi = x.astype(BF16)
    lo = (x - hi.astype(F32)).astype(BF16)
    return hi, lo


def _split3(x):
    h1 = x.astype(BF16)
    r1 = x - h1.astype(F32)
    h2 = r1.astype(BF16)
    h3 = (r1 - h2.astype(F32)).astype(BF16)
    return h1, h2, h3


def _mm_exact_rhs(x, m):
    hi, lo = _split2(x)
    if 2 * x.shape[1] <= MXU_DIM:
        return _mm(jnp.concatenate([hi, lo], axis=1), jnp.concatenate([m, m], axis=0))
    return _mm(hi, m) + _mm(lo, m)


def _head_ones(width):
    r = lax.broadcasted_iota(jnp.int32, (width, width), 0) // HEAD_DIM
    c = lax.broadcasted_iota(jnp.int32, (width, width), 1) // HEAD_DIM
    return jnp.where(r == c, 1.0, 0.0).astype(BF16)


def _softplus(z):
    return jnp.maximum(z, 0.0) + jnp.log(1.0 + jnp.exp(-jnp.abs(z)))


def _sigmoid(z):
    return 1.0 / (1.0 + jnp.exp(-z))


def _store_row_tiles(ref, base, x):
    pieces = x.shape[1] // LANES
    for s in range(pieces):
        ref[pl.ds(base + s, x.shape[0], stride=pieces), :] = x[:, s * LANES:(s + 1) * LANES]


def _load_row_tiles(ref, base, rows, pieces):
    return jnp.concatenate([ref[pl.ds(base + s, rows, stride=pieces), :] for s in range(pieces)], axis=1)


def _tile_copy(src_ref, src_row, dst_ref, dst_row, sem):
    src = src_ref.at[pl.ds(pl.multiple_of(src_row * SUBLANES, SUBLANES), SUBLANES)]
    dst = dst_ref.at[pl.ds(pl.multiple_of(dst_row * SUBLANES, SUBLANES), SUBLANES)]
    return pltpu.make_async_copy(src, dst, sem)


def _head_rms_norm(x, gain):
    ones = _head_ones(MXU_DIM)
    groups = [x[:, c:c + MXU_DIM] for c in range(0, x.shape[1], MXU_DIM)]
    ms = jnp.concatenate([_mm_exact_rhs(g * g, ones) for g in groups], axis=1) * (1.0 / HEAD_DIM)
    return x * lax.rsqrt(ms + NORM_EPS) * gain


def _inproj_kernel(x_ref, g_ref, wsb_ref, wrw_ref, wlo_ref, gq_ref, gk_ref, sb_ref, rw_ref, lo_ref):
    x = x_ref[...]
    ms = jnp.mean(x * x, axis=-1, keepdims=True)
    h = (x * lax.rsqrt(ms + NORM_EPS) * g_ref[...]).astype(BF16)
    rw_ref[...] = _mm(h, wrw_ref[...])
    lo_ref[...] = _mm(h, wlo_ref[...])
    sb = _mm(h, wsb_ref[...])
    width = gq_ref.shape[1]
    sb_ref[:, :width] = (_head_rms_norm(sb[:, :width], gq_ref[...]) * (HEAD_DIM ** -0.5)).astype(BF16)
    sb_ref[:, width:2 * width] = _head_rms_norm(sb[:, width:2 * width], gk_ref[...]).astype(BF16)
    sb_ref[:, 2 * width:] = sb[:, 2 * width:].astype(BF16)


def _inproj(x2, gain, w_sb, w_rw, w_lo, gq, gk, tm):
    n, d = x2.shape
    full = lambda a: pl.BlockSpec(a.shape, lambda i: (0,) * a.ndim)
    rows = lambda c: pl.BlockSpec((tm, c), lambda i: (i, 0))
    return pl.pallas_call(
        _inproj_kernel,
        grid=(n // tm,),
        in_specs=[rows(d), full(gain), full(w_sb), full(w_rw), full(w_lo), full(gq), full(gk)],
        out_specs=[rows(w_sb.shape[1]), rows(w_rw.shape[1]), rows(w_lo.shape[1])],
        out_shape=[jax.ShapeDtypeStruct((n, w_sb.shape[1]), BF16), jax.ShapeDtypeStruct((n, w_rw.shape[1]), F32),
                   jax.ShapeDtypeStruct((n, w_lo.shape[1]), F32)],
        compiler_params=pltpu.CompilerParams(dimension_semantics=("arbitrary",),
                                             vmem_limit_bytes=VMEM_LIMIT),
        name="inproj",
    )(x2, gain, w_sb, w_rw, w_lo, gq, gk)


def _sbattn_kernel(q_ref, k_ref, v_ref, o_ref, *, blk, pairs):
    qb = pl.program_id(1)
    lane = lax.broadcasted_iota(jnp.int32, (1, pairs * LANES), 1)
    first = (lane % LANES) < HEAD_DIM
    qn = q_ref[...].astype(F32)
    q_first = jnp.where(first, qn, 0.0).astype(BF16)
    q_second = jnp.where(first, 0.0, qn).astype(BF16)
    lanes_of = lambda p: slice(p * LANES, (p + 1) * LANES)
    qq = [jnp.concatenate([q_first[:, lanes_of(p)], q_second[:, lanes_of(p)]], axis=0) for p in range(pairs)]

    jj = lax.broadcasted_iota(jnp.int32, (blk, blk), 0)
    ss = lax.broadcasted_iota(jnp.int32, (blk, blk), 1)
    uu = jnp.where(jj >= ss, 1.0, 0.0).astype(BF16)

    rows = 2 * blk
    tt = lax.broadcasted_iota(jnp.int32, (pairs * rows, blk), 0) % blk
    sk = lax.broadcasted_iota(jnp.int32, (pairs * rows, blk), 1)
    causal = sk < tt

    def step(j, c, acc, diagonal):
        start = pl.multiple_of(j * blk, blk)
        z = jnp.concatenate([_nt(qq[p], k_ref[pl.ds(start, blk), lanes_of(p)]) for p in range(pairs)], axis=0)
        lk = -_softplus(z)
        if diagonal:
            lk = jnp.where(causal, lk, 0.0)
        sums = _mm_exact_rhs(lk, uu)
        w = jnp.exp(z + sums + c)
        if diagonal:
            w = jnp.where(causal, w, 0.0)
        w = w.astype(BF16)
        pv = jnp.concatenate([_mm(w[p * rows:(p + 1) * rows], v_ref[pl.ds(start, blk), lanes_of(p)])
                              for p in range(pairs)], axis=0)
        c = c + sums[:, 0:1]
        return c, acc + pv, jnp.max(c) > UNDERFLOW_LOG

    carry = (qb - 1,) + step(qb, jnp.zeros((pairs * rows, 1), F32), jnp.zeros((pairs * rows, LANES), F32), True)
    _, _, acc, _ = lax.while_loop(lambda cr: (cr[0] >= 0) & cr[3],
                                  lambda cr: (cr[0] - 1,) + step(cr[0], cr[1], cr[2], False), carry)
    first_pair = first[:, :LANES]
    for p in range(pairs):
        o_ref[:, lanes_of(p)] = jnp.where(first_pair, acc[p * rows:p * rows + blk], acc[p * rows + blk:(p + 1) * rows])


def _sbattn(sb3, blk):
    b, s, w3 = sb3.shape
    width = w3 // 3
    return pl.pallas_call(
        functools.partial(_sbattn_kernel, blk=blk, pairs=width // LANES),
        grid=(b, s // blk),
        in_specs=[pl.BlockSpec((None, blk, width), lambda bi, qb: (bi, qb, 0)),
                  pl.BlockSpec((None, s, width), lambda bi, qb: (bi, 0, 1)),
                  pl.BlockSpec((None, s, width), lambda bi, qb: (bi, 0, 2))],
        out_specs=pl.BlockSpec((None, blk, width), lambda bi, qb: (bi, qb, 0)),
        out_shape=jax.ShapeDtypeStruct((b, s, width), F32),
        compiler_params=pltpu.CompilerParams(dimension_semantics=("arbitrary",) * 2,
                                             vmem_limit_bytes=VMEM_LIMIT),
        name="sbattn",
    )(sb3, sb3, sb3)


def _token_shift(p, prev_ref, mu, first_tile):
    rows = lax.broadcasted_iota(jnp.int32, (p.shape[0], 1), 0)
    last_prev = jnp.where(first_tile, 0.0, prev_ref[7:8, :])
    prev = jnp.where(rows == 0, last_prev, pltpu.roll(p, 1, axis=0))
    return p + (prev - p) * mu


def _rwprep_stage(rw_ref, rwp_ref, lo_ref, lop_ref, mur_ref, mul_ref, w0_ref, a0_ref, kk_ref, ka_ref,
                  wup_ref, aup_ref, gup_ref, *, width):
    first_tile = pl.program_id(1) == 0
    pf = _token_shift(rw_ref[...], rwp_ref, mur_ref[...], first_tile)
    lf = _token_shift(lo_ref[...], lop_ref, mul_ref[...], first_tile)
    r = pf[:, :width]
    k = pf[:, width:2 * width]
    v = pf[:, 2 * width:]
    w = -_softplus(-(w0_ref[...] + _mm(jnp.tanh(lf).astype(BF16), wup_ref[...]))) - 0.5
    lr = _sigmoid(a0_ref[...] + _mm(lf.astype(BF16), aup_ref[...]))
    kk = k * kk_ref[...]
    ss = _mm_exact_rhs(kk * kk, _head_ones(width))
    kk = kk / jnp.maximum(jnp.sqrt(ss), 1e-12)
    log_decay = -jnp.exp(w)
    gate = _mm(_sigmoid(lf).astype(BF16), gup_ref[...])
    return r, log_decay, k * (1.0 + (lr - 1.0) * ka_ref[...]), v, -kk, kk * lr, gate


def _rwscan_kernel(rw_ref, rwp_ref, lo_ref, lop_ref, mur_ref, mul_ref, w0_ref, a0_ref, kk_ref, ka_ref,
                   wup_ref, aup_ref, gup_ref, rk_ref, lnw_ref, lnb_ref,
                   o_ref, st_ref, y_ref, *, tb, width):
    r_all, lw_all, k_all, v_all, a_all, b_all, g_all = _rwprep_stage(
        rw_ref, rwp_ref, lo_ref, lop_ref, mur_ref, mul_ref, w0_ref, a0_ref, kk_ref, ka_ref,
        wup_ref, aup_ref, gup_ref, width=width)

    t = RW_CHUNK
    n = LANES
    m = MXU_DIM
    groups = width // m
    pairs = width // n

    @pl.when(pl.program_id(1) == 0)
    def _():
        st_ref[...] = jnp.zeros_like(st_ref)

    ri = lax.broadcasted_iota(jnp.int32, (2 * n, 2 * n), 0)
    ci = lax.broadcasted_iota(jnp.int32, (2 * n, 2 * n), 1)
    same = ((ri % n) // HEAD_DIM) == ((ci % n) // HEAD_DIM)
    gmask = same & ((ci % HEAD_DIM) < (ri % HEAD_DIM) + jnp.where(ri >= n, 1, 0))
    rn = lax.broadcasted_iota(jnp.int32, (n, n), 0)
    cn = lax.broadcasted_iota(jnp.int32, (n, n), 1)
    bdmask = jnp.where((rn // HEAD_DIM) == (cn // HEAD_DIM), 1.0, 0.0)
    eye = jnp.where(rn == cn, 1.0, 0.0)
    ti = lax.broadcasted_iota(jnp.int32, (t, 4 * t), 0)
    si = lax.broadcasted_iota(jnp.int32, (t, 4 * t), 1)
    tri3 = jnp.where((si % t <= ti) & (si < 3 * t), 1.0, 0.0).astype(BF16)
    right_half = jnp.where(lax.broadcasted_iota(jnp.int32, (n, 2 * n), 1) >= n, 1.0, 0.0)
    cat = jnp.concatenate

    def blockdiag32(x):
        return cat([x, x], axis=0) * bdmask

    def prepare(rows, cols):
        r = r_all[rows, cols]
        lw = lw_all[rows, cols]
        k = k_all[rows, cols]
        v = v_all[rows, cols]
        a = a_all[rows, cols]
        b = b_all[rows, cols]

        l1, l2, l3 = _split3(lw)
        cum = _mm(tri3, cat([l1, l2, l3, l1], axis=0))
        yield
        cum_end = cum[t - 1:t, :]
        p_inv = jnp.exp(-cum)
        p_rest = jnp.exp(cum_end - cum)
        r32 = blockdiag32(r * jnp.exp(cum))
        r_bd = r32.astype(BF16)
        a_bd = blockdiag32(a * jnp.exp(cum - lw)).astype(BF16)
        k_bd = blockdiag32(k * p_inv).astype(BF16)
        b_bd = blockdiag32(b * p_inv).astype(BF16)
        kd_bd = blockdiag32(k * p_rest).astype(BF16)
        bd_bd = blockdiag32(b * p_rest).astype(BF16)
        v32 = blockdiag32(v)
        v_bd = v32.astype(BF16)
        zero_v = (cat([v32, v32], axis=1) * right_half).astype(BF16)

        g = jnp.where(gmask, _nt(cat([a_bd, r_bd], axis=0), cat([b_bd, k_bd], axis=0)), 0.0)
        yield
        aab = g[:n, :n]
        aak = g[:n, n:].astype(BF16)
        brb_brk = g[n:, :].astype(BF16)

        x = eye + aab
        p = aab.astype(BF16)
        p = _mm(p, p).astype(BF16)
        av = _mm(aak, v_bd).astype(BF16)
        yield
        steps = (t - 1).bit_length() - 1
        for i in range(steps):
            if i + 1 < steps:
                px_pp = _mm(p, cat([x.astype(BF16), p], axis=1))
                x = x + px_pp[:, :n]
                p = px_pp[:, n:].astype(BF16)
            else:
                x = x + _mm(p, x.astype(BF16))
            yield
        tinv = x.astype(BF16)

        w_u0 = _mm(tinv, cat([a_bd, av], axis=1)).astype(BF16)
        yield
        stack = cat([w_u0, zero_v], axis=0)
        q_y0 = _mm(brb_brk, stack)
        m_c0 = _tn(cat([bd_bd, kd_bd], axis=0), stack)
        yield
        q_mat = (r32 + q_y0[:, :n]).astype(BF16)
        m_mat = (eye * jnp.exp(cum_end) + m_c0[:, :n]).astype(BF16)
        lhs = cat([cat([q_mat, q_mat], axis=1), cat([m_mat, m_mat], axis=1)], axis=0)
        return lhs, q_y0[:, n:], m_c0[:, n:]

    def lockstep(generators):
        results = [None] * len(generators)
        live = list(enumerate(generators))
        while live:
            still = []
            for idx, gen in live:
                try:
                    next(gen)
                    still.append((idx, gen))
                except StopIteration as done:
                    results[idx] = done.value
            live = still
        return results

    lane_cols = [slice(pr * n, (pr + 1) * n) for pr in range(pairs)]
    time_rows = [slice(ch * t, (ch + 1) * t) for ch in range(tb // t)]
    units = [(rows, cols) for rows in time_rows for cols in lane_cols]
    prepared = []
    for first in range(0, len(units), RW_WAVE):
        prepared += lockstep([prepare(rows, cols) for rows, cols in units[first:first + RW_WAVE]])
    states = [st_ref[pr] for pr in range(pairs)]
    for ch, rows in enumerate(time_rows):
        for pr, cols in enumerate(lane_cols):
            lhs, y0, c0 = prepared[ch * pairs + pr]
            y_s = _mm(lhs, cat(_split2(states[pr]), axis=0))
            y = y_s[:n] + y0
            y_ref[rows, cols] = y[:t] + y[t:]
            states[pr] = y_s[n:] + c0
    for pr in range(pairs):
        st_ref[pr] = states[pr]

    ones = _head_ones(m)
    inv = 1.0 / HEAD_DIM
    for grp in range(groups):
        cols = slice(grp * m, (grp + 1) * m)
        y = y_ref[:, cols]
        mean = _mm_exact_rhs(y, ones) * inv
        d = y - mean
        var = _mm_exact_rhs(d * d, ones) * inv
        yn = d * lax.rsqrt(var + LN_X_EPS) * lnw_ref[:, cols] + lnb_ref[:, cols]
        bonus = _mm_exact_rhs(r_all[:, cols] * k_all[:, cols] * rk_ref[:, cols], ones) * v_all[:, cols]
        o_ref[:, cols] = (yn + bonus) * g_all[:, cols]


def _rwscan(rw3, lo3, mu_rw, mu_lo, w0, a0, k_k, k_a, wup, aup, gup, r_k, ln_w, ln_b, tb):
    bsz, s, w3 = rw3.shape
    width = w3 // 3
    nlo = lo3.shape[2]
    cur = lambda c: pl.BlockSpec((None, tb, c), lambda bi, i: (bi, i, 0))
    prev = lambda c: pl.BlockSpec((None, SUBLANES, c),
                                  lambda bi, i: (bi, jnp.maximum(i * (tb // SUBLANES) - 1, 0), 0))
    full = lambda x: pl.BlockSpec(x.shape, lambda bi, i: (0,) * x.ndim)
    params = (mu_rw, mu_lo, w0, a0, k_k, k_a, wup, aup, gup, r_k, ln_w, ln_b)
    return pl.pallas_call(
        functools.partial(_rwscan_kernel, tb=tb, width=width),
        grid=(bsz, s // tb),
        in_specs=[cur(w3), prev(w3), cur(nlo), prev(nlo)] + [full(p) for p in params],
        out_specs=cur(width),
        out_shape=jax.ShapeDtypeStruct((bsz, s, width), F32),
        scratch_shapes=[pltpu.VMEM((width // LANES, LANES, LANES), F32)] + [pltpu.VMEM((tb, width), F32)],
        compiler_params=pltpu.CompilerParams(dimension_semantics=("arbitrary",) * 2,
                                             vmem_limit_bytes=VMEM_LIMIT),
        name="rwscan",
    )(rw3, rw3, lo3, lo3, *params)


def _outproj_kernel(sb_ref, rw_ref, x_ref, wsb_ref, wrw_ref, g_ref, wr_ref,
                    x1_ref, h2_ref, meta_ref, route_ref, cnt_ref, run_ref, *, tm):
    @pl.when(pl.program_id(0) == 0)
    def _():
        run_ref[...] = jnp.zeros_like(run_ref)

    x1 = x_ref[...] + _mm(sb_ref[...].astype(BF16), wsb_ref[...]) + _mm(rw_ref[...].astype(BF16), wrw_ref[...])
    x1_ref[...] = x1
    ms = jnp.mean(x1 * x1, axis=-1, keepdims=True)
    h2 = x1 * lax.rsqrt(ms + NORM_EPS) * g_ref[...]
    _store_row_tiles(h2_ref, 0, h2)

    h_hi, h_lo = _split2(h2)
    w_hi, w_lo = _split2(wr_ref[...])
    lg = _mm(h_hi, w_hi) + _mm(h_hi, w_lo) + _mm(h_lo, w_hi)

    lane = lax.broadcasted_iota(jnp.int32, (tm, LANES), 1).astype(F32)
    neg = -jnp.inf
    big = float(LANES)
    is_group = lane < N_GROUPS
    gl = jnp.where(is_group, lg, neg)
    gmax = jnp.max(gl, axis=1, keepdims=True)
    gidx = jnp.min(jnp.where(gl == gmax, lane, big), axis=1, keepdims=True)
    group_gate = 1.0 / jnp.sum(jnp.where(is_group, jnp.exp(lg - gmax), 0.0), axis=1, keepdims=True)
    lo_lane = N_GROUPS + EXPERTS_PER_GROUP * gidx
    el = jnp.where((lane >= lo_lane) & (lane < lo_lane + EXPERTS_PER_GROUP), lg, neg)
    m1 = jnp.max(el, axis=1, keepdims=True)
    i1 = jnp.min(jnp.where(el == m1, lane, big), axis=1, keepdims=True)
    el2 = jnp.where(lane == i1, neg, el)
    m2 = jnp.max(el2, axis=1, keepdims=True)
    i2 = jnp.min(jnp.where(el2 == m2, lane, big), axis=1, keepdims=True)
    p2 = jnp.exp(m2 - m1)
    gate1 = group_gate / (1.0 + p2)
    gate2 = group_gate * p2 / (1.0 + p2)

    hit1 = lane == i1
    hit2 = lane == i2
    onehot = jnp.where(hit1 | hit2, 1.0, 0.0)
    rr = lax.broadcasted_iota(jnp.int32, (tm, tm), 0)
    cc = lax.broadcasted_iota(jnp.int32, (tm, tm), 1)
    below = jnp.where(cc < rr, 1.0, 0.0).astype(BF16)
    before = run_ref[...] + _mm(below, onehot.astype(BF16))
    rank1 = jnp.sum(jnp.where(hit1, before, 0.0), axis=1, keepdims=True)
    rank2 = jnp.sum(jnp.where(hit2, before, 0.0), axis=1, keepdims=True)
    run = run_ref[...] + jnp.sum(onehot, axis=0, keepdims=True)
    run_ref[...] = run
    cnt_ref[...] = run

    vals = (i1 - N_GROUPS, i2 - N_GROUPS, gate1, gate2, rank1, rank2)
    meta = jnp.zeros((tm, LANES), F32)
    for pos, val in enumerate(vals):
        meta = jnp.where(lane == pos, val.astype(F32), meta)
    meta_ref[...] = meta
    route_ref[...] = meta.T[:SUBLANES]


def _outproj(sb_out, rw_out, x2, w_sb, w_rw, gain, w_router, tm):
    n, d = x2.shape
    rows = lambda c: pl.BlockSpec((tm, c), lambda i: (i, 0))
    full = lambda a: pl.BlockSpec(a.shape, lambda i: (0,) * a.ndim)
    return pl.pallas_call(
        functools.partial(_outproj_kernel, tm=tm),
        grid=(n // tm,),
        in_specs=[rows(sb_out.shape[1]), rows(rw_out.shape[1]), rows(d), full(w_sb), full(w_rw), full(gain),
                  full(w_router)],
        out_specs=[rows(d), pl.BlockSpec((tm * (d // LANES), LANES), lambda i: (i, 0)), rows(LANES),
                   pl.BlockSpec((SUBLANES, tm), lambda i: (0, i)), pl.BlockSpec((1, LANES), lambda i: (0, 0))],
        out_shape=[jax.ShapeDtypeStruct((n, d), F32), jax.ShapeDtypeStruct((n * (d // LANES), LANES), F32),
                   jax.ShapeDtypeStruct((n, LANES), F32), jax.ShapeDtypeStruct((SUBLANES, n), F32),
                   jax.ShapeDtypeStruct((1, LANES), F32)],
        scratch_shapes=[pltpu.VMEM((1, LANES), F32)],
        compiler_params=pltpu.CompilerParams(dimension_semantics=("arbitrary",),
                                             vmem_limit_bytes=VMEM_LIMIT),
        name="outproj",
    )(sb_out, rw_out, x2, w_sb, w_rw, gain, w_router)


def _start_tile_gather(src_ref, idx_ref, dst_ref, dst_base, sem, n_rows):
    def issue(r, _):
        _tile_copy(src_ref, idx_ref[0, r], dst_ref, dst_base + r, sem).start()
        return 0

    lax.fori_loop(0, n_rows, issue, 0, unroll=8)


def _wait_tile_gather(src_ref, dst_ref, sem, n_rows):
    def drain(r, _):
        _tile_copy(src_ref, 0, dst_ref, 0, sem).wait()
        return 0

    lax.fori_loop(0, n_rows, drain, 0, unroll=8)


def _srcmap_kernel(dest_ref, seg_ref, src_ref):
    def put(tok, _):
        for kk in range(TOP_K):
            src_ref[dest_ref[TOP_K * tok + kk]] = tok
        return 0

    lax.fori_loop(0, dest_ref.shape[0] // TOP_K, put, 0, unroll=8)

    def pad(e, _):
        first = seg_ref[0, e]

        def clear(r, _):
            src_ref[first + r] = 0
            return 0

        lax.fori_loop(seg_ref[1, e], seg_ref[2, e], clear, 0)
        return 0

    lax.fori_loop(0, seg_ref.shape[1], pad, 0)


def _srcmap(dest_flat, segments, n_rows):
    return pl.pallas_call(
        _srcmap_kernel,
        in_specs=[pl.BlockSpec(memory_space=pltpu.SMEM), pl.BlockSpec(memory_space=pltpu.SMEM)],
        out_specs=pl.BlockSpec(memory_space=pltpu.SMEM),
        out_shape=jax.ShapeDtypeStruct((n_rows,), jnp.int32),
        name="srcmap",
    )(dest_flat, segments)


def _experts_kernel(eid_ref, used_ref, *refs):
    idx_refs = refs[:GATHER_DEPTH]
    h_ref, wg_ref, wu_ref, wd_ref, o_ref, x_ref, wgb_ref, wub_ref, wdb_ref, sems = refs[GATHER_DEPTH:]
    i = pl.program_id(0)
    used = used_ref[0]
    slot = lax.rem(i, GATHER_DEPTH)
    rows = EXPERT_ROWS

    def start_block(ahead):
        dst = lax.rem(i + ahead, GATHER_DEPTH)
        _start_tile_gather(h_ref, idx_refs[ahead], x_ref, dst * rows, sems.at[dst], rows)

    for ahead in range(GATHER_DEPTH - 1):
        @pl.when((i == 0) & (ahead < used))
        def _():
            start_block(ahead)

    @pl.when(i + GATHER_DEPTH - 1 < used)
    def _():
        start_block(GATHER_DEPTH - 1)

    @pl.when((i == 0) | (eid_ref[i] != eid_ref[jnp.maximum(i - 1, 0)]))
    def _():
        wgb_ref[...] = wg_ref[...].astype(BF16)
        wub_ref[...] = wu_ref[...].astype(BF16)
        wdb_ref[...] = wd_ref[...].astype(BF16)

    @pl.when(i < used)
    def _():
        _wait_tile_gather(h_ref, x_ref, sems.at[slot], rows)
        xb = _load_row_tiles(x_ref, slot * (rows * SUBLANES), rows, SUBLANES).astype(BF16)
        hg = _mm(xb, wgb_ref[...])
        hu = _mm(xb, wub_ref[...])
        act = hg * _sigmoid(hg) * hu
        _store_row_tiles(o_ref, 0, _mm(act.astype(BF16), wdb_ref[...]))

    @pl.when(i >= used)
    def _():
        o_ref[...] = jnp.zeros_like(o_ref)


def _experts(blk_eid, n_used, src3, h2_tiles, w_g, w_u, w_d):
    n_blocks = src3.shape[0]
    d, de = w_g.shape[1], w_g.shape[2]
    assert d == SUBLANES * LANES
    w_map = lambda i, eid, used: (eid[i], 0, 0)
    idx_spec = lambda ahead: pl.BlockSpec(
        (None, 1, EXPERT_ROWS), lambda i, eid, used: (jnp.minimum(i + ahead, used[0] - 1), 0, 0),
        memory_space=pltpu.SMEM)
    grid_spec = pltpu.PrefetchScalarGridSpec(
        num_scalar_prefetch=2,
        grid=(n_blocks,),
        in_specs=[idx_spec(ahead) for ahead in range(GATHER_DEPTH)] + [
                  pl.BlockSpec(memory_space=pl.ANY),
                  pl.BlockSpec((None, d, de), w_map), pl.BlockSpec((None, d, de), w_map),
                  pl.BlockSpec((None, de, d), w_map)],
        out_specs=pl.BlockSpec((EXPERT_ROWS * SUBLANES, LANES), lambda i, eid, used: (i, 0)),
        scratch_shapes=[pltpu.VMEM((GATHER_DEPTH * EXPERT_ROWS * SUBLANES, LANES), F32),
                        pltpu.VMEM((d, de), BF16), pltpu.VMEM((d, de), BF16), pltpu.VMEM((de, d), BF16),
                        pltpu.SemaphoreType.DMA((GATHER_DEPTH,))],
    )
    return pl.pallas_call(
        _experts_kernel,
        grid_spec=grid_spec,
        out_shape=jax.ShapeDtypeStruct((n_blocks * EXPERT_ROWS * SUBLANES, LANES), F32),
        compiler_params=pltpu.CompilerParams(dimension_semantics=("arbitrary",),
                                             vmem_limit_bytes=VMEM_LIMIT),
        name="experts",
    )(blk_eid, n_used, *([src3] * GATHER_DEPTH), h2_tiles, w_g, w_u, w_d)


def _combine_kernel(cur_ref, nxt_ref, rows_ref, x1_ref, meta_ref, o_ref, got_ref, sems, *, tm):
    i = pl.program_id(0)
    slot = i % 2
    slot_rows = TOP_K * tm

    def start_tile(dest_ref, dst_slot):
        def issue(t, _):
            for kk in range(TOP_K):
                _tile_copy(rows_ref, dest_ref[0, TOP_K * t + kk], got_ref, dst_slot * slot_rows + kk * tm + t,
                           sems.at[dst_slot]).start(priority=kk)
            return 0

        lax.fori_loop(0, tm, issue, 0, unroll=4)

    @pl.when(i == 0)
    def _():
        start_tile(cur_ref, 0)

    @pl.when(i + 1 < pl.num_programs(0))
    def _():
        start_tile(nxt_ref, 1 - slot)

    _wait_tile_gather(rows_ref, got_ref, sems.at[slot], slot_rows)
    meta = meta_ref[...]
    y = x1_ref[...]
    for kk in range(TOP_K):
        y = y + _load_row_tiles(got_ref, (slot * slot_rows + kk * tm) * SUBLANES, tm, SUBLANES) * meta[:, 2 + kk:3 + kk]
    o_ref[...] = y


def _combine(dest3, exp_out_tiles, x1, meta, tm):
    n, d = x1.shape
    steps = n // tm
    rows = lambda c: pl.BlockSpec((tm, c), lambda i: (i, 0))
    idx_spec = lambda ahead: pl.BlockSpec((None, 1, TOP_K * tm), lambda i: (jnp.minimum(i + ahead, steps - 1), 0, 0),
                                          memory_space=pltpu.SMEM)
    return pl.pallas_call(
        functools.partial(_combine_kernel, tm=tm),
        grid=(steps,),
        in_specs=[idx_spec(0), idx_spec(1), pl.BlockSpec(memory_space=pl.ANY), rows(d), rows(LANES)],
        out_specs=rows(d),
        out_shape=jax.ShapeDtypeStruct((n, d), F32),
        scratch_shapes=[pltpu.VMEM((2 * TOP_K * tm * SUBLANES, LANES), F32), pltpu.SemaphoreType.DMA((2,))],
        compiler_params=pltpu.CompilerParams(dimension_semantics=("arbitrary",),
                                             vmem_limit_bytes=VMEM_LIMIT),
        name="combine",
    )(dest3, dest3, exp_out_tiles, x1, meta)


def _layer(x, norm1_gain, w_in, sb_q_gain, sb_k_gain, rw_shift_mu, rw_w0, rw_w_up, rw_a0, rw_a_up, rw_g_up,
           rw_k_k, rw_k_a, rw_r_k, rw_ln_w, rw_ln_b, w_out, norm2_gain, w_router_group, w_router_expert,
           w_exp_gate, w_exp_up, w_exp_down):
    bsz, s, d = x.shape
    n = bsz * s
    sbw = d // 2
    rww = d - sbw
    n_decay, n_aaa = rw_w_up.shape[0], rw_a_up.shape[0]
    n_lora = n_decay + n_aaa + rw_g_up.shape[0]
    tm = min(256, n)
    row = lambda a: a.reshape(1, -1).astype(F32)

    x2 = x.reshape(n, d)
    w_bf = w_in.astype(BF16)
    heads = lambda gvec: jnp.tile(row(gvec), (1, sbw // HEAD_DIM))
    sb, rw, lo = _inproj(x2, row(norm1_gain), w_bf[:, :3 * sbw], w_bf[:, 3 * sbw:3 * sbw + 3 * rww],
                         w_bf[:, 3 * sbw + 3 * rww:], heads(sb_q_gain), heads(sb_k_gain), min(IN_TILE, n))
    sb_out = _sbattn(sb.reshape(bsz, s, 3 * sbw), min(SB_BLOCK, s))

    pad_rows = lambda w_up, start: jnp.zeros((n_lora, rww), BF16).at[start:start + w_up.shape[0]].set(
        w_up.astype(BF16))
    mu = row(rw_shift_mu)
    rw_out = _rwscan(rw.reshape(bsz, s, 3 * rww), lo.reshape(bsz, s, n_lora), mu[:, :3 * rww], mu[:, 3 * rww:],
                     row(rw_w0), row(rw_a0), row(rw_k_k), row(rw_k_a),
                     pad_rows(rw_w_up, 0), pad_rows(rw_a_up, n_decay), pad_rows(rw_g_up, n_decay + n_aaa),
                     row(rw_r_k), row(rw_ln_w), row(rw_ln_b), min(RW_TILE, s))

    w_out_bf = w_out.astype(BF16)
    w_router = jnp.zeros((d, LANES), F32).at[:, :N_GROUPS].set(w_router_group).at[
        :, N_GROUPS:N_GROUPS + N_EXPERTS].set(w_router_expert)
    x1, h2, meta, route, counts = _outproj(sb_out.reshape(n, sbw), rw_out.reshape(n, rww), x2, w_out_bf[:sbw],
                                    w_out_bf[sbw:], row(norm2_gain), w_router, tm)

    counts = counts[0, N_GROUPS:N_GROUPS + N_EXPERTS].astype(jnp.int32)
    padded = (counts + EXPERT_ROWS - 1) // EXPERT_ROWS * EXPERT_ROWS
    pad_end = jnp.cumsum(padded)
    pad_start = pad_end - padded
    eid = route[0:TOP_K].astype(jnp.int32)
    rank = route[4:4 + TOP_K].astype(jnp.int32)
    segment_start = jnp.sum(jnp.where(eid[None] == jnp.arange(N_EXPERTS, dtype=jnp.int32)[:, None, None],
                                      pad_start[:, None, None], 0), axis=0)
    dest = (segment_start + rank).T
    dest3 = dest.reshape(n // tm, 1, TOP_K * tm)
    n_rows = n * TOP_K + N_EXPERTS * EXPERT_ROWS
    n_blocks = n_rows // EXPERT_ROWS
    block_start = jnp.arange(n_blocks, dtype=jnp.int32) * EXPERT_ROWS
    blk_eid = jnp.minimum(jnp.sum((pad_end[None, :] <= block_start[:, None]).astype(jnp.int32), axis=1),
                          N_EXPERTS - 1)
    n_used = (pad_end[-1:] // EXPERT_ROWS).astype(jnp.int32)

    tail = jnp.zeros((1,), jnp.int32)
    segments = jnp.stack([jnp.concatenate([pad_start, pad_end[-1:]]), jnp.concatenate([counts, tail]),
                          jnp.concatenate([padded, n_rows - pad_end[-1:]])]).astype(jnp.int32)
    src3 = _srcmap(dest.reshape(-1), segments, n_rows).reshape(n_blocks, 1, EXPERT_ROWS)

    exp_out = _experts(blk_eid, n_used, src3, h2, w_exp_gate, w_exp_up, w_exp_down)
    return _combine(dest3, exp_out, x1, meta, tm).reshape(bsz, s, d)


def kernel(x, norm1_gain, w_in, sb_q_gain, sb_k_gain, rw_shift_mu, rw_w0, rw_w_up, rw_a0, rw_a_up, rw_g_up,
           rw_k_k, rw_k_a, rw_r_k, rw_ln_w, rw_ln_b, w_out, norm2_gain, w_router_group, w_router_expert,
           w_exp_gate, w_exp_up, w_exp_down):
    params = (norm1_gain, w_in, sb_q_gain, sb_k_gain, rw_shift_mu, rw_w0, rw_w_up, rw_a0, rw_a_up, rw_g_up,
              rw_k_k, rw_k_a, rw_r_k, rw_ln_w, rw_ln_b, w_out, norm2_gain, w_router_group, w_router_expert,
              w_exp_gate, w_exp_up, w_exp_down)
    for layer in range(norm1_gain.shape[0]):
        x = _layer(x, *(p[layer] for p in params))
    return x
```

```python
import functools

import jax
import jax.numpy as jnp
from jax import lax
from jax.experimental import pallas as pl
from jax.experimental.pallas import tpu as pltpu

F32 = jnp.float32
BF16 = jnp.bfloat16

HEAD_DIM = 64
NORM_EPS = 1e-6
LN_X_EPS = 64e-5
N_GROUPS = 4
EXPERTS_PER_GROUP = 8
N_EXPERTS = N_GROUPS * EXPERTS_PER_GROUP
TOP_K = 2

LANES = 128
SUBLANES = 8
MXU_DIM = 256
RW_CHUNK = 64
IN_TILE = 512
RW_TILE = 256
RW_WAVE = 16
GATHER_DEPTH = 3
DMA_PRIORITIES = 2
SB_BLOCK = 256
EXPERT_ROWS = 256
UNDERFLOW_LOG = -110.0
VMEM_LIMIT = 56 * 1024 * 1024


def _nt(a, b):
    return lax.dot_general(a, b, (((1,), (1,)), ((), ())), preferred_element_type=F32)


def _tn(a, b):
    return lax.dot_general(a, b, (((0,), (0,)), ((), ())), preferred_element_type=F32)


def _mm(a, b):
    return jnp.dot(a, b, preferred_element_type=F32)


def _split2(x):
    hi = x.astype(BF16)
    lo = (x - hi.astype(F32)).astype(BF16)
    return hi, lo


def _split3(x):
    h1 = x.astype(BF16)
    r1 = x - h1.astype(F32)
    h2 = r1.astype(BF16)
    h3 = (r1 - h2.astype(F32)).astype(BF16)
    return h1, h2, h3


def _mm_exact_rhs(x, m):
    hi, lo = _split2(x)
    if 2 * x.shape[1] <= MXU_DIM:
        return _mm(jnp.concatenate([hi, lo], axis=1), jnp.concatenate([m, m], axis=0))
    return _mm(hi, m) + _mm(lo, m)


def _head_ones(width):
    r = lax.broadcasted_iota(jnp.int32, (width, width), 0) // HEAD_DIM
    c = lax.broadcasted_iota(jnp.int32, (width, width), 1) // HEAD_DIM
    return jnp.where(r == c, 1.0, 0.0).astype(BF16)


def _softplus(z):
    return jnp.maximum(z, 0.0) + jnp.log(1.0 + jnp.exp(-jnp.abs(z)))


def _sigmoid(z):
    return 1.0 / (1.0 + jnp.exp(-z))


def _store_row_tiles(ref, base, x):
    pieces = x.shape[1] // LANES
    for s in range(pieces):
        ref[pl.ds(base + s, x.shape[0], stride=pieces), :] = x[:, s * LANES:(s + 1) * LANES]


def _load_row_tiles(ref, base, rows, pieces):
    return jnp.concatenate([ref[pl.ds(base + s, rows, stride=pieces), :] for s in range(pieces)], axis=1)


def _tile_copy(src_ref, src_row, dst_ref, dst_row, sem):
    src = src_ref.at[pl.ds(pl.multiple_of(src_row * SUBLANES, SUBLANES), SUBLANES)]
    dst = dst_ref.at[pl.ds(pl.multiple_of(dst_row * SUBLANES, SUBLANES), SUBLANES)]
    return pltpu.make_async_copy(src, dst, sem)


def _head_rms_norm(x, gain):
    ones = _head_ones(MXU_DIM)
    groups = [x[:, c:c + MXU_DIM] for c in range(0, x.shape[1], MXU_DIM)]
    ms = jnp.concatenate([_mm_exact_rhs(g * g, ones) for g in groups], axis=1) * (1.0 / HEAD_DIM)
    return x * lax.rsqrt(ms + NORM_EPS) * gain


def _inproj_kernel(x_ref, g_ref, wsb_ref, wrw_ref, wlo_ref, gq_ref, gk_ref, sb_ref, rw_ref, lo_ref):
    x = x_ref[...]
    ms = jnp.mean(x * x, axis=-1, keepdims=True)
    h = (x * lax.rsqrt(ms + NORM_EPS) * g_ref[...]).astype(BF16)
    rw_ref[...] = _mm(h, wrw_ref[...])
    lo_ref[...] = _mm(h, wlo_ref[...])
    sb = _mm(h, wsb_ref[...])
    width = gq_ref.shape[1]
    sb_ref[:, :width] = (_head_rms_norm(sb[:, :width], gq_ref[...]) * (HEAD_DIM ** -0.5)).astype(BF16)
    sb_ref[:, width:2 * width] = _head_rms_norm(sb[:, width:2 * width], gk_ref[...]).astype(BF16)
    sb_ref[:, 2 * width:] = sb[:, 2 * width:].astype(BF16)


def _inproj(x2, gain, w_sb, w_rw, w_lo, gq, gk, tm):
    n, d = x2.shape
    full = lambda a: pl.BlockSpec(a.shape, lambda i: (0,) * a.ndim)
    rows = lambda c: pl.BlockSpec((tm, c), lambda i: (i, 0))
    return pl.pallas_call(
        _inproj_kernel,
        grid=(n // tm,),
        in_specs=[rows(d), full(gain), full(w_sb), full(w_rw), full(w_lo), full(gq), full(gk)],
        out_specs=[rows(w_sb.shape[1]), rows(w_rw.shape[1]), rows(w_lo.shape[1])],
        out_shape=[jax.ShapeDtypeStruct((n, w_sb.shape[1]), BF16), jax.ShapeDtypeStruct((n, w_rw.shape[1]), F32),
                   jax.ShapeDtypeStruct((n, w_lo.shape[1]), F32)],
        compiler_params=pltpu.CompilerParams(dimension_semantics=("arbitrary",),
                                             vmem_limit_bytes=VMEM_LIMIT),
        name="inproj",
    )(x2, gain, w_sb, w_rw, w_lo, gq, gk)


def _sbattn_kernel(q_ref, k_ref, v_ref, o_ref, *, blk, pairs):
    qb = pl.program_id(1)
    lane = lax.broadcasted_iota(jnp.int32, (1, pairs * LANES), 1)
    first = (lane % LANES) < HEAD_DIM
    qn = q_ref[...].astype(F32)
    q_first = jnp.where(first, qn, 0.0).astype(BF16)
    q_second = jnp.where(first, 0.0, qn).astype(BF16)
    lanes_of = lambda p: slice(p * LANES, (p + 1) * LANES)
    qq = [jnp.concatenate([q_first[:, lanes_of(p)], q_second[:, lanes_of(p)]], axis=0) for p in range(pairs)]

    jj = lax.broadcasted_iota(jnp.int32, (blk, blk), 0)
    ss = lax.broadcasted_iota(jnp.int32, (blk, blk), 1)
    uu = jnp.where(jj >= ss, 1.0, 0.0).astype(BF16)

    rows = 2 * blk
    tt = lax.broadcasted_iota(jnp.int32, (pairs * rows, blk), 0) % blk
    sk = lax.broadcasted_iota(jnp.int32, (pairs * rows, blk), 1)
    causal = sk < tt

    def step(j, c, acc, diagonal):
        start = pl.multiple_of(j * blk, blk)
        z = jnp.concatenate([_nt(qq[p], k_ref[pl.ds(start, blk), lanes_of(p)]) for p in range(pairs)], axis=0)
        lk = -_softplus(z)
        if diagonal:
            lk = jnp.where(causal, lk, 0.0)
        sums = _mm_exact_rhs(lk, uu)
        w = jnp.exp(z + sums + c)
        if diagonal:
            w = jnp.where(causal, w, 0.0)
        w = w.astype(BF16)
        pv = jnp.concatenate([_mm(w[p * rows:(p + 1) * rows], v_ref[pl.ds(start, blk), lanes_of(p)])
                              for p in range(pairs)], axis=0)
        c = c + sums[:, 0:1]
        return c, acc + pv, jnp.max(c) > UNDERFLOW_LOG

    carry = (qb - 1,) + step(qb, jnp.zeros((pairs * rows, 1), F32), jnp.zeros((pairs * rows, LANES), F32), True)
    _, _, acc, _ = lax.while_loop(lambda cr: (cr[0] >= 0) & cr[3],
                                  lambda cr: (cr[0] - 1,) + step(cr[0], cr[1], cr[2], False), carry)
    first_pair = first[:, :LANES]
    for p in range(pairs):
        o_ref[:, lanes_of(p)] = jnp.where(first_pair, acc[p * rows:p * rows + blk], acc[p * rows + blk:(p + 1) * rows])


def _sbattn(sb3, blk):
    b, s, w3 = sb3.shape
    width = w3 // 3
    return pl.pallas_call(
        functools.partial(_sbattn_kernel, blk=blk, pairs=width // LANES),
        grid=(b, s // blk),
        in_specs=[pl.BlockSpec((None, blk, width), lambda bi, qb: (bi, qb, 0)),
                  pl.BlockSpec((None, s, width), lambda bi, qb: (bi, 0, 1)),
                  pl.BlockSpec((None, s, width), lambda bi, qb: (bi, 0, 2))],
        out_specs=pl.BlockSpec((None, blk, width), lambda bi, qb: (bi, qb, 0)),
        out_shape=jax.ShapeDtypeStruct((b, s, width), F32),
        compiler_params=pltpu.CompilerParams(dimension_semantics=("arbitrary",) * 2,
                                             vmem_limit_bytes=VMEM_LIMIT),
        name="sbattn",
    )(sb3, sb3, sb3)


def _token_shift(p, prev_ref, mu, first_tile):
    rows = lax.broadcasted_iota(jnp.int32, (p.shape[0], 1), 0)
    last_prev = jnp.where(first_tile, 0.0, prev_ref[7:8, :])
    prev = jnp.where(rows == 0, last_prev, pltpu.roll(p, 1, axis=0))
    return p + (prev - p) * mu


def _rwprep_stage(rw_ref, rwp_ref, lo_ref, lop_ref, mur_ref, mul_ref, w0_ref, a0_ref, kk_ref, ka_ref,
                  wup_ref, aup_ref, gup_ref, *, width):
    first_tile = pl.program_id(1) == 0
    pf = _token_shift(rw_ref[...], rwp_ref, mur_ref[...], first_tile)
    lf = _token_shift(lo_ref[...], lop_ref, mul_ref[...], first_tile)
    r = pf[:, :width]
    k = pf[:, width:2 * width]
    v = pf[:, 2 * width:]
    w = -_softplus(-(w0_ref[...] + _mm(jnp.tanh(lf).astype(BF16), wup_ref[...]))) - 0.5
    lr = _sigmoid(a0_ref[...] + _mm(lf.astype(BF16), aup_ref[...]))
    kk = k * kk_ref[...]
    ss = _mm_exact_rhs(kk * kk, _head_ones(width))
    kk = kk / jnp.maximum(jnp.sqrt(ss), 1e-12)
    log_decay = -jnp.exp(w)
    gate = _mm(_sigmoid(lf).astype(BF16), gup_ref[...])
    return r, log_decay, k * (1.0 + (lr - 1.0) * ka_ref[...]), v, -kk, kk * lr, gate


def _rwscan_kernel(rw_ref, rwp_ref, lo_ref, lop_ref, mur_ref, mul_ref, w0_ref, a0_ref, kk_ref, ka_ref,
                   wup_ref, aup_ref, gup_ref, rk_ref, lnw_ref, lnb_ref,
                   o_ref, st_ref, y_ref, *, tb, width):
    r_all, lw_all, k_all, v_all, a_all, b_all, g_all = _rwprep_stage(
        rw_ref, rwp_ref, lo_ref, lop_ref, mur_ref, mul_ref, w0_ref, a0_ref, kk_ref, ka_ref,
        wup_ref, aup_ref, gup_ref, width=width)

    t = RW_CHUNK
    n = LANES
    m = MXU_DIM
    groups = width // m
    pairs = width // n

    @pl.when(pl.program_id(1) == 0)
    def _():
        st_ref[...] = jnp.zeros_like(st_ref)

    ri = lax.broadcasted_iota(jnp.int32, (2 * n, 2 * n), 0)
    ci = lax.broadcasted_iota(jnp.int32, (2 * n, 2 * n), 1)
    same = ((ri % n) // HEAD_DIM) == ((ci % n) // HEAD_DIM)
    gmask = same & ((ci % HEAD_DIM) < (ri % HEAD_DIM) + jnp.where(ri >= n, 1, 0))
    rn = lax.broadcasted_iota(jnp.int32, (n, n), 0)
    cn = lax.broadcasted_iota(jnp.int32, (n, n), 1)
    bdmask = jnp.where((rn // HEAD_DIM) == (cn // HEAD_DIM), 1.0, 0.0)
    eye = jnp.where(rn == cn, 1.0, 0.0)
    ti = lax.broadcasted_iota(jnp.int32, (t, 4 * t), 0)
    si = lax.broadcasted_iota(jnp.int32, (t, 4 * t), 1)
    tri3 = jnp.where((si % t <= ti) & (si < 3 * t), 1.0, 0.0).astype(BF16)
    right_half = jnp.where(lax.broadcasted_iota(jnp.int32, (n, 2 * n), 1) >= n, 1.0, 0.0)
    cat = jnp.concatenate

    def blockdiag32(x):
        return cat([x, x], axis=0) * bdmask

    def prepare(rows, cols):
        r = r_all[rows, cols]
        lw = lw_all[rows, cols]
        k = k_all[rows, cols]
        v = v_all[rows, cols]
        a = a_all[rows, cols]
        b = b_all[rows, cols]

        l1, l2, l3 = _split3(lw)
        cum = _mm(tri3, cat([l1, l2, l3, l1], axis=0))
        yield
        cum_end = cum[t - 1:t, :]
        p_inv = jnp.exp(-cum)
        p_rest = jnp.exp(cum_end - cum)
        r32 = blockdiag32(r * jnp.exp(cum))
        r_bd = r32.astype(BF16)
        a_bd = blockdiag32(a * jnp.exp(cum - lw)).astype(BF16)
        k_bd = blockdiag32(k * p_inv).astype(BF16)
        b_bd = blockdiag32(b * p_inv).astype(BF16)
        kd_bd = blockdiag32(k * p_rest).astype(BF16)
        bd_bd = blockdiag32(b * p_rest).astype(BF16)
        v32 = blockdiag32(v)
        v_bd = v32.astype(BF16)
        zero_v = (cat([v32, v32], axis=1) * right_half).astype(BF16)

        g = jnp.where(gmask, _nt(cat([a_bd, r_bd], axis=0), cat([b_bd, k_bd], axis=0)), 0.0)
        yield
        aab = g[:n, :n]
        aak = g[:n, n:].astype(BF16)
        brb_brk = g[n:, :].astype(BF16)

        x = eye + aab
        p = aab.astype(BF16)
        p = _mm(p, p).astype(BF16)
        av = _mm(aak, v_bd).astype(BF16)
        yield
        steps = (t - 1).bit_length() - 1
        for i in range(steps):
            if i + 1 < steps:
                px_pp = _mm(p, cat([x.astype(BF16), p], axis=1))
                x = x + px_pp[:, :n]
                p = px_pp[:, n:].astype(BF16)
            else:
                x = x + _mm(p, x.astype(BF16))
            yield
        tinv = x.astype(BF16)

        w_u0 = _mm(tinv, cat([a_bd, av], axis=1)).astype(BF16)
        yield
        stack = cat([w_u0, zero_v], axis=0)
        q_y0 = _mm(brb_brk, stack)
        m_c0 = _tn(cat([bd_bd, kd_bd], axis=0), stack)
        yield
        q_mat = (r32 + q_y0[:, :n]).astype(BF16)
        m_mat = (eye * jnp.exp(cum_end) + m_c0[:, :n]).astype(BF16)
        lhs = cat([cat([q_mat, q_mat], axis=1), cat([m_mat, m_mat], axis=1)], axis=0)
        return lhs, q_y0[:, n:], m_c0[:, n:]

    def lockstep(generators):
        results = [None] * len(generators)
        live = list(enumerate(generators))
        while live:
            still = []
            for idx, gen in live:
                try:
                    next(gen)
                    still.append((idx, gen))
                except StopIteration as done:
                    results[idx] = done.value
            live = still
        return results

    lane_cols = [slice(pr * n, (pr + 1) * n) for pr in range(pairs)]
    time_rows = [slice(ch * t, (ch + 1) * t) for ch in range(tb // t)]
    units = [(rows, cols) for rows in time_rows for cols in lane_cols]
    prepared = []
    for first in range(0, len(units), RW_WAVE):
        prepared += lockstep([prepare(rows, cols) for rows, cols in units[first:first + RW_WAVE]])
    states = [st_ref[pr] for pr in range(pairs)]
    for ch, rows in enumerate(time_rows):
        for pr, cols in enumerate(lane_cols):
            lhs, y0, c0 = prepared[ch * pairs + pr]
            y_s = _mm(lhs, cat(_split2(states[pr]), axis=0))
            y = y_s[:n] + y0
            y_ref[rows, cols] = y[:t] + y[t:]
            states[pr] = y_s[n:] + c0
    for pr in range(pairs):
        st_ref[pr] = states[pr]

    ones = _head_ones(m)
    inv = 1.0 / HEAD_DIM
    for grp in range(groups):
        cols = slice(grp * m, (grp + 1) * m)
        y = y_ref[:, cols]
        mean = _mm_exact_rhs(y, ones) * inv
        d = y - mean
        var = _mm_exact_rhs(d * d, ones) * inv
        yn = d * lax.rsqrt(var + LN_X_EPS) * lnw_ref[:, cols] + lnb_ref[:, cols]
        bonus = _mm_exact_rhs(r_all[:, cols] * k_all[:, cols] * rk_ref[:, cols], ones) * v_all[:, cols]
        o_ref[:, cols] = (yn + bonus) * g_all[:, cols]


def _rwscan(rw3, lo3, mu_rw, mu_lo, w0, a0, k_k, k_a, wup, aup, gup, r_k, ln_w, ln_b, tb):
    bsz, s, w3 = rw3.shape
    width = w3 // 3
    nlo = lo3.shape[2]
    cur = lambda c: pl.BlockSpec((None, tb, c), lambda bi, i: (bi, i, 0))
    prev = lambda c: pl.BlockSpec((None, SUBLANES, c),
                                  lambda bi, i: (bi, jnp.maximum(i * (tb // SUBLANES) - 1, 0), 0))
    full = lambda x: pl.BlockSpec(x.shape, lambda bi, i: (0,) * x.ndim)
    params = (mu_rw, mu_lo, w0, a0, k_k, k_a, wup, aup, gup, r_k, ln_w, ln_b)
    return pl.pallas_call(
        functools.partial(_rwscan_kernel, tb=tb, width=width),
        grid=(bsz, s // tb),
        in_specs=[cur(w3), prev(w3), cur(nlo), prev(nlo)] + [full(p) for p in params],
        out_specs=cur(width),
        out_shape=jax.ShapeDtypeStruct((bsz, s, width), F32),
        scratch_shapes=[pltpu.VMEM((width // LANES, LANES, LANES), F32)] + [pltpu.VMEM((tb, width), F32)],
        compiler_params=pltpu.CompilerParams(dimension_semantics=("arbitrary",) * 2,
                                             vmem_limit_bytes=VMEM_LIMIT),
        name="rwscan",
    )(rw3, rw3, lo3, lo3, *params)


def _outproj_kernel(sb_ref, rw_ref, x_ref, wsb_ref, wrw_ref, g_ref, wr_ref,
                    x1_ref, h2_ref, meta_ref, route_ref, cnt_ref, run_ref, *, tm):
    @pl.when(pl.program_id(0) == 0)
    def _():
        run_ref[...] = jnp.zeros_like(run_ref)

    x1 = x_ref[...] + _mm(sb_ref[...].astype(BF16), wsb_ref[...]) + _mm(rw_ref[...].astype(BF16), wrw_ref[...])
    x1_ref[...] = x1
    ms = jnp.mean(x1 * x1, axis=-1, keepdims=True)
    h2 = x1 * lax.rsqrt(ms + NORM_EPS) * g_ref[...]
    _store_row_tiles(h2_ref, 0, h2)

    h_hi, h_lo = _split2(h2)
    w_hi, w_lo = _split2(wr_ref[...])
    lg = _mm(h_hi, w_hi) + _mm(h_hi, w_lo) + _mm(h_lo, w_hi)

    lane = lax.broadcasted_iota(jnp.int32, (tm, LANES), 1).astype(F32)
    neg = -jnp.inf
    big = float(LANES)
    is_group = lane < N_GROUPS
    gl = jnp.where(is_group, lg, neg)
    gmax = jnp.max(gl, axis=1, keepdims=True)
    gidx = jnp.min(jnp.where(gl == gmax, lane, big), axis=1, keepdims=True)
    group_gate = 1.0 / jnp.sum(jnp.where(is_group, jnp.exp(lg - gmax), 0.0), axis=1, keepdims=True)
    lo_lane = N_GROUPS + EXPERTS_PER_GROUP * gidx
    el = jnp.where((lane >= lo_lane) & (lane < lo_lane + EXPERTS_PER_GROUP), lg, neg)
    m1 = jnp.max(el, axis=1, keepdims=True)
    i1 = jnp.min(jnp.where(el == m1, lane, big), axis=1, keepdims=True)
    el2 = jnp.where(lane == i1, neg, el)
    m2 = jnp.max(el2, axis=1, keepdims=True)
    i2 = jnp.min(jnp.where(el2 == m2, lane, big), axis=1, keepdims=True)
    p2 = jnp.exp(m2 - m1)
    gate1 = group_gate / (1.0 + p2)
    gate2 = group_gate * p2 / (1.0 + p2)

    hit1 = lane == i1
    hit2 = lane == i2
    onehot = jnp.where(hit1 | hit2, 1.0, 0.0)
    rr = lax.broadcasted_iota(jnp.int32, (tm, tm), 0)
    cc = lax.broadcasted_iota(jnp.int32, (tm, tm), 1)
    below = jnp.where(cc < rr, 1.0, 0.0).astype(BF16)
    before = run_ref[...] + _mm(below, onehot.astype(BF16))
    rank1 = jnp.sum(jnp.where(hit1, before, 0.0), axis=1, keepdims=True)
    rank2 = jnp.sum(jnp.where(hit2, before, 0.0), axis=1, keepdims=True)
    run = run_ref[...] + jnp.sum(onehot, axis=0, keepdims=True)
    run_ref[...] = run
    cnt_ref[...] = run

    vals = (i1 - N_GROUPS, i2 - N_GROUPS, gate1, gate2, rank1, rank2)
    meta = jnp.zeros((tm, LANES), F32)
    for pos, val in enumerate(vals):
        meta = jnp.where(lane == pos, val.astype(F32), meta)
    meta_ref[...] = meta
    route_ref[...] = meta.T[:SUBLANES]


def _outproj(sb_out, rw_out, x2, w_sb, w_rw, gain, w_router, tm):
    n, d = x2.shape
    rows = lambda c: pl.BlockSpec((tm, c), lambda i: (i, 0))
    full = lambda a: pl.BlockSpec(a.shape, lambda i: (0,) * a.ndim)
    return pl.pallas_call(
        functools.partial(_outproj_kernel, tm=tm),
        grid=(n // tm,),
        in_specs=[rows(sb_out.shape[1]), rows(rw_out.shape[1]), rows(d), full(w_sb), full(w_rw), full(gain),
                  full(w_router)],
        out_specs=[rows(d), pl.BlockSpec((tm * (d // LANES), LANES), lambda i: (i, 0)), rows(LANES),
                   pl.BlockSpec((SUBLANES, tm), lambda i: (0, i)), pl.BlockSpec((1, LANES), lambda i: (0, 0))],
        out_shape=[jax.ShapeDtypeStruct((n, d), F32), jax.ShapeDtypeStruct((n * (d // LANES), LANES), F32),
                   jax.ShapeDtypeStruct((n, LANES), F32), jax.ShapeDtypeStruct((SUBLANES, n), F32),
                   jax.ShapeDtypeStruct((1, LANES), F32)],
        scratch_shapes=[pltpu.VMEM((1, LANES), F32)],
        compiler_params=pltpu.CompilerParams(dimension_semantics=("arbitrary",),
                                             vmem_limit_bytes=VMEM_LIMIT),
        name="outproj",
    )(sb_out, rw_out, x2, w_sb, w_rw, gain, w_router)


def _start_tile_gather(src_ref, idx_ref, dst_ref, dst_base, sem, n_rows):
    def issue(pair, _):
        for u in range(DMA_PRIORITIES):
            r = DMA_PRIORITIES * pair + u
            _tile_copy(src_ref, idx_ref[0, r], dst_ref, dst_base + r, sem).start(priority=u)
        return 0

    lax.fori_loop(0, n_rows // DMA_PRIORITIES, issue, 0, unroll=4)


def _wait_tile_gather(src_ref, dst_ref, sem, n_rows):
    def drain(r, _):
        _tile_copy(src_ref, 0, dst_ref, 0, sem).wait()
        return 0

    lax.fori_loop(0, n_rows, drain, 0, unroll=8)


def _srcmap_kernel(dest_ref, seg_ref, src_ref):
    tokens = dest_ref.shape[0] // TOP_K
    first_token = pl.program_id(0) * tokens

    def put(t, _):
        for kk in range(TOP_K):
            src_ref[dest_ref[TOP_K * t + kk]] = first_token + t
        return 0

    lax.fori_loop(0, tokens, put, 0, unroll=8)

    @pl.when(pl.program_id(0) == pl.num_programs(0) - 1)
    def _():
        def pad(e, _):
            first = seg_ref[0, e]

            def clear(r, _):
                src_ref[first + r] = 0
                return 0

            lax.fori_loop(seg_ref[1, e], seg_ref[2, e], clear, 0)
            return 0

        lax.fori_loop(0, seg_ref.shape[1], pad, 0)


def _srcmap(dest_flat, segments, n_rows):
    steps = 8 if dest_flat.shape[0] % (8 * 1024) == 0 else 1
    return pl.pallas_call(
        _srcmap_kernel,
        grid=(steps,),
        in_specs=[pl.BlockSpec((dest_flat.shape[0] // steps,), lambda i: (i,), memory_space=pltpu.SMEM),
                  pl.BlockSpec(segments.shape, lambda i: (0, 0), memory_space=pltpu.SMEM)],
        out_specs=pl.BlockSpec((n_rows,), lambda i: (0,), memory_space=pltpu.SMEM),
        out_shape=jax.ShapeDtypeStruct((n_rows,), jnp.int32),
        compiler_params=pltpu.CompilerParams(dimension_semantics=("arbitrary",)),
        name="srcmap",
    )(dest_flat, segments)


def _experts_kernel(eid_ref, used_ref, *refs):
    idx_refs = refs[:GATHER_DEPTH]
    h_ref, wg_ref, wu_ref, wd_ref, o_ref, x_ref, wgb_ref, wub_ref, wdb_ref, sems = refs[GATHER_DEPTH:]
    i = pl.program_id(0)
    used = used_ref[0]
    slot = lax.rem(i, GATHER_DEPTH)
    rows = EXPERT_ROWS

    def start_block(ahead):
        dst = lax.rem(i + ahead, GATHER_DEPTH)
        _start_tile_gather(h_ref, idx_refs[ahead], x_ref, dst * rows, sems.at[dst], rows)

    for ahead in range(GATHER_DEPTH - 1):
        @pl.when((i == 0) & (ahead < used))
        def _():
            start_block(ahead)

    @pl.when(i + GATHER_DEPTH - 1 < used)
    def _():
        start_block(GATHER_DEPTH - 1)

    @pl.when((i == 0) | (eid_ref[i] != eid_ref[jnp.maximum(i - 1, 0)]))
    def _():
        wgb_ref[...] = wg_ref[...].astype(BF16)
        wub_ref[...] = wu_ref[...].astype(BF16)
        wdb_ref[...] = wd_ref[...].astype(BF16)

    @pl.when(i < used)
    def _():
        _wait_tile_gather(h_ref, x_ref, sems.at[slot], rows)
        xb = _load_row_tiles(x_ref, slot * (rows * SUBLANES), rows, SUBLANES).astype(BF16)
        hg = _mm(xb, wgb_ref[...])
        hu = _mm(xb, wub_ref[...])
        act = hg * _sigmoid(hg) * hu
        _store_row_tiles(o_ref, 0, _mm(act.astype(BF16), wdb_ref[...]))

    @pl.when(i >= used)
    def _():
        o_ref[...] = jnp.zeros_like(o_ref)


def _experts(blk_eid, n_used, src3, h2_tiles, w_g, w_u, w_d):
    n_blocks = src3.shape[0]
    d, de = w_g.shape[1], w_g.shape[2]
    assert d == SUBLANES * LANES
    w_map = lambda i, eid, used: (eid[i], 0, 0)
    idx_spec = lambda ahead: pl.BlockSpec(
        (None, 1, EXPERT_ROWS), lambda i, eid, used: (jnp.minimum(i + ahead, used[0] - 1), 0, 0),
        memory_space=pltpu.SMEM)
    grid_spec = pltpu.PrefetchScalarGridSpec(
        num_scalar_prefetch=2,
        grid=(n_blocks,),
        in_specs=[idx_spec(ahead) for ahead in range(GATHER_DEPTH)] + [
                  pl.BlockSpec(memory_space=pl.ANY),
                  pl.BlockSpec((None, d, de), w_map), pl.BlockSpec((None, d, de), w_map),
                  pl.BlockSpec((None, de, d), w_map)],
        out_specs=pl.BlockSpec((EXPERT_ROWS * SUBLANES, LANES), lambda i, eid, used: (i, 0)),
        scratch_shapes=[pltpu.VMEM((GATHER_DEPTH * EXPERT_ROWS * SUBLANES, LANES), F32),
                        pltpu.VMEM((d, de), BF16), pltpu.VMEM((d, de), BF16), pltpu.VMEM((de, d), BF16),
                        pltpu.SemaphoreType.DMA((GATHER_DEPTH,))],
    )
    return pl.pallas_call(
        _experts_kernel,
        grid_spec=grid_spec,
        out_shape=jax.ShapeDtypeStruct((n_blocks * EXPERT_ROWS * SUBLANES, LANES), F32),
        compiler_params=pltpu.CompilerParams(dimension_semantics=("arbitrary",),
                                             vmem_limit_bytes=VMEM_LIMIT),
        name="experts",
    )(blk_eid, n_used, *([src3] * GATHER_DEPTH), h2_tiles, w_g, w_u, w_d)


def _combine_kernel(cur_ref, nxt_ref, rows_ref, x1_ref, meta_ref, o_ref, got_ref, sems, *, tm):
    i = pl.program_id(0)
    slot = i % 2
    slot_rows = TOP_K * tm

    def start_tile(dest_ref, dst_slot):
        def issue(t, _):
            for kk in range(TOP_K):
                _tile_copy(rows_ref, dest_ref[0, TOP_K * t + kk], got_ref, dst_slot * slot_rows + kk * tm + t,
                           sems.at[dst_slot]).start(priority=kk)
            return 0

        lax.fori_loop(0, tm, issue, 0, unroll=4)

    @pl.when(i == 0)
    def _():
        start_tile(cur_ref, 0)

    @pl.when(i + 1 < pl.num_programs(0))
    def _():
        start_tile(nxt_ref, 1 - slot)

    _wait_tile_gather(rows_ref, got_ref, sems.at[slot], slot_rows)
    meta = meta_ref[...]
    y = x1_ref[...]
    for kk in range(TOP_K):
        y = y + _load_row_tiles(got_ref, (slot * slot_rows + kk * tm) * SUBLANES, tm, SUBLANES) * meta[:, 2 + kk:3 + kk]
    o_ref[...] = y


def _combine(dest3, exp_out_tiles, x1, meta, tm):
    n, d = x1.shape
    steps = n // tm
    rows = lambda c: pl.BlockSpec((tm, c), lambda i: (i, 0))
    idx_spec = lambda ahead: pl.BlockSpec((None, 1, TOP_K * tm), lambda i: (jnp.minimum(i + ahead, steps - 1), 0, 0),
                                          memory_space=pltpu.SMEM)
    return pl.pallas_call(
        functools.partial(_combine_kernel, tm=tm),
        grid=(steps,),
        in_specs=[idx_spec(0), idx_spec(1), pl.BlockSpec(memory_space=pl.ANY), rows(d), rows(LANES)],
        out_specs=rows(d),
        out_shape=jax.ShapeDtypeStruct((n, d), F32),
        scratch_shapes=[pltpu.VMEM((2 * TOP_K * tm * SUBLANES, LANES), F32), pltpu.SemaphoreType.DMA((2,))],
        compiler_params=pltpu.CompilerParams(dimension_semantics=("arbitrary",),
                                             vmem_limit_bytes=VMEM_LIMIT),
        name="combine",
    )(dest3, dest3, exp_out_tiles, x1, meta)


def _layer(x, norm1_gain, w_in, sb_q_gain, sb_k_gain, rw_shift_mu, rw_w0, rw_w_up, rw_a0, rw_a_up, rw_g_up,
           rw_k_k, rw_k_a, rw_r_k, rw_ln_w, rw_ln_b, w_out, norm2_gain, w_router_group, w_router_expert,
           w_exp_gate, w_exp_up, w_exp_down):
    bsz, s, d = x.shape
    n = bsz * s
    sbw = d // 2
    rww = d - sbw
    n_decay, n_aaa = rw_w_up.shape[0], rw_a_up.shape[0]
    n_lora = n_decay + n_aaa + rw_g_up.shape[0]
    tm = min(256, n)
    row = lambda a: a.reshape(1, -1).astype(F32)

    x2 = x.reshape(n, d)
    w_bf = w_in.astype(BF16)
    heads = lambda gvec: jnp.tile(row(gvec), (1, sbw // HEAD_DIM))
    sb, rw, lo = _inproj(x2, row(norm1_gain), w_bf[:, :3 * sbw], w_bf[:, 3 * sbw:3 * sbw + 3 * rww],
                         w_bf[:, 3 * sbw + 3 * rww:], heads(sb_q_gain), heads(sb_k_gain), min(IN_TILE, n))
    sb_out = _sbattn(sb.reshape(bsz, s, 3 * sbw), min(SB_BLOCK, s))

    pad_rows = lambda w_up, start: jnp.zeros((n_lora, rww), BF16).at[start:start + w_up.shape[0]].set(
        w_up.astype(BF16))
    mu = row(rw_shift_mu)
    rw_out = _rwscan(rw.reshape(bsz, s, 3 * rww), lo.reshape(bsz, s, n_lora), mu[:, :3 * rww], mu[:, 3 * rww:],
                     row(rw_w0), row(rw_a0), row(rw_k_k), row(rw_k_a),
                     pad_rows(rw_w_up, 0), pad_rows(rw_a_up, n_decay), pad_rows(rw_g_up, n_decay + n_aaa),
                     row(rw_r_k), row(rw_ln_w), row(rw_ln_b), min(RW_TILE, s))

    w_out_bf = w_out.astype(BF16)
    w_router = jnp.zeros((d, LANES), F32).at[:, :N_GROUPS].set(w_router_group).at[
        :, N_GROUPS:N_GROUPS + N_EXPERTS].set(w_router_expert)
    x1, h2, meta, route, counts = _outproj(sb_out.reshape(n, sbw), rw_out.reshape(n, rww), x2, w_out_bf[:sbw],
                                    w_out_bf[sbw:], row(norm2_gain), w_router, tm)

    counts = counts[0, N_GROUPS:N_GROUPS + N_EXPERTS].astype(jnp.int32)
    padded = (counts + EXPERT_ROWS - 1) // EXPERT_ROWS * EXPERT_ROWS
    pad_end = jnp.cumsum(padded)
    pad_start = pad_end - padded
    eid = route[0:TOP_K].astype(jnp.int32)
    rank = route[4:4 + TOP_K].astype(jnp.int32)
    segment_start = jnp.sum(jnp.where(eid[None] == jnp.arange(N_EXPERTS, dtype=jnp.int32)[:, None, None],
                                      pad_start[:, None, None], 0), axis=0)
    dest = (segment_start + rank).T
    dest3 = dest.reshape(n // tm, 1, TOP_K * tm)
    n_rows = n * TOP_K + N_EXPERTS * EXPERT_ROWS
    n_blocks = n_rows // EXPERT_ROWS
    block_start = jnp.arange(n_blocks, dtype=jnp.int32) * EXPERT_ROWS
    blk_eid = jnp.minimum(jnp.sum((pad_end[None, :] <= block_start[:, None]).astype(jnp.int32), axis=1),
                          N_EXPERTS - 1)
    n_used = (pad_end[-1:] // EXPERT_ROWS).astype(jnp.int32)

    tail = jnp.zeros((1,), jnp.int32)
    segments = jnp.stack([jnp.concatenate([pad_start, pad_end[-1:]]), jnp.concatenate([counts, tail]),
                          jnp.concatenate([padded, n_rows - pad_end[-1:]])]).astype(jnp.int32)
    src3 = _srcmap(dest.reshape(-1), segments, n_rows).reshape(n_blocks, 1, EXPERT_ROWS)

    exp_out = _experts(blk_eid, n_used, src3, h2, w_exp_gate, w_exp_up, w_exp_down)
    return _combine(dest3, exp_out, x1, meta, tm).reshape(bsz, s, d)


def kernel(x, norm1_gain, w_in, sb_q_gain, sb_k_gain, rw_shift_mu, rw_w0, rw_w_up, rw_a0, rw_a_up, rw_g_up,
           rw_k_k, rw_k_a, rw_r_k, rw_ln_w, rw_ln_b, w_out, norm2_gain, w_router_group, w_router_expert,
           w_exp_gate, w_exp_up, w_exp_down):
    params = (norm1_gain, w_in, sb_q_gain, sb_k_gain, rw_shift_mu, rw_w0, rw_w_up, rw_a0, rw_a_up, rw_g_up,
              rw_k_k, rw_k_a, rw_r_k, rw_ln_w, rw_ln_b, w_out, norm2_gain, w_router_group, w_router_expert,
              w_exp_gate, w_exp_up, w_exp_down)
    for layer in range(norm1_gain.shape[0]):
        x = _layer(x, *(p[layer] for p in params))
    return x
```

```python
import functools

import jax
import jax.numpy as jnp
from jax import lax
from jax.experimental import pallas as pl
from jax.experimental.pallas import tpu as pltpu

F32 = jnp.float32
BF16 = jnp.bfloat16

HEAD_DIM = 64
NORM_EPS = 1e-6
LN_X_EPS = 64e-5
N_GROUPS = 4
EXPERTS_PER_GROUP = 8
N_EXPERTS = N_GROUPS * EXPERTS_PER_GROUP
TOP_K = 2

LANES = 128
SUBLANES = 8
MXU_DIM = 256
RW_CHUNK = 64
IN_TILE = 512
RW_TILE = 256
RW_WAVE = 16
GATHER_DEPTH = 3
SB_BLOCK = 256
EXPERT_ROWS = 256
UNDERFLOW_LOG = -110.0
VMEM_LIMIT = 56 * 1024 * 1024


def _nt(a, b):
    return lax.dot_general(a, b, (((1,), (1,)), ((), ())), preferred_element_type=F32)


def _tn(a, b):
    return lax.dot_general(a, b, (((0,), (0,)), ((), ())), preferred_element_type=F32)


def _mm(a, b):
    return jnp.dot(a, b, preferred_element_type=F32)


def _split2(x):
    hi = x.astype(BF16)
    lo = (x - hi.astype(F32)).astype(BF16)
    return hi, lo


def _split3(x):
    h1 = x.astype(BF16)
    r1 = x - h1.astype(F32)
    h2 = r1.astype(BF16)
    h3 = (r1 - h2.astype(F32)).astype(BF16)
    return h1, h2, h3


def _mm_exact_rhs(x, m):
    hi, lo = _split2(x)
    if 2 * x.shape[1] <= MXU_DIM:
        return _mm(jnp.concatenate([hi, lo], axis=1), jnp.concatenate([m, m], axis=0))
    return _mm(hi, m) + _mm(lo, m)


def _head_ones(width):
    r = lax.broadcasted_iota(jnp.int32, (width, width), 0) // HEAD_DIM
    c = lax.broadcasted_iota(jnp.int32, (width, width), 1) // HEAD_DIM
    return jnp.where(r == c, 1.0, 0.0).astype(BF16)


def _softplus(z):
    return jnp.maximum(z, 0.0) + jnp.log(1.0 + jnp.exp(-jnp.abs(z)))


def _sigmoid(z):
    return 1.0 / (1.0 + jnp.exp(-z))


def _store_row_tiles(ref, base, x):
    pieces = x.shape[1] // LANES
    for s in range(pieces):
        ref[pl.ds(base + s, x.shape[0], stride=pieces), :] = x[:, s * LANES:(s + 1) * LANES]


def _load_row_tiles(ref, base, rows, pieces):
    return jnp.concatenate([ref[pl.ds(base + s, rows, stride=pieces), :] for s in range(pieces)], axis=1)


def _tile_copy(src_ref, src_row, dst_ref, dst_row, sem):
    src = src_ref.at[pl.ds(pl.multiple_of(src_row * SUBLANES, SUBLANES), SUBLANES)]
    dst = dst_ref.at[pl.ds(pl.multiple_of(dst_row * SUBLANES, SUBLANES), SUBLANES)]
    return pltpu.make_async_copy(src, dst, sem)


def _head_rms_norm(x, gain):
    ones = _head_ones(MXU_DIM)
    groups = [x[:, c:c + MXU_DIM] for c in range(0, x.shape[1], MXU_DIM)]
    ms = jnp.concatenate([_mm_exact_rhs(g * g, ones) for g in groups], axis=1) * (1.0 / HEAD_DIM)
    return x * lax.rsqrt(ms + NORM_EPS) * gain


def _inproj_kernel(x_ref, g_ref, wsb_ref, wrw_ref, wlo_ref, gq_ref, gk_ref, sb_ref, rw_ref, lo_ref):
    x = x_ref[...]
    ms = jnp.mean(x * x, axis=-1, keepdims=True)
    h = (x * lax.rsqrt(ms + NORM_EPS) * g_ref[...]).astype(BF16)
    rw_ref[...] = _mm(h, wrw_ref[...])
    lo_ref[...] = _mm(h, wlo_ref[...])
    sb = _mm(h, wsb_ref[...])
    width = gq_ref.shape[1]
    sb_ref[:, :width] = (_head_rms_norm(sb[:, :width], gq_ref[...]) * (HEAD_DIM ** -0.5)).astype(BF16)
    sb_ref[:, width:2 * width] = _head_rms_norm(sb[:, width:2 * width], gk_ref[...]).astype(BF16)
    sb_ref[:, 2 * width:] = sb[:, 2 * width:].astype(BF16)


def _inproj(x2, gain, w_sb, w_rw, w_lo, gq, gk, tm):
    n, d = x2.shape
    full = lambda a: pl.BlockSpec(a.shape, lambda i: (0,) * a.ndim)
    rows = lambda c: pl.BlockSpec((tm, c), lambda i: (i, 0))
    return pl.pallas_call(
        _inproj_kernel,
        grid=(n // tm,),
        in_specs=[rows(d), full(gain), full(w_sb), full(w_rw), full(w_lo), full(gq), full(gk)],
        out_specs=[rows(w_sb.shape[1]), rows(w_rw.shape[1]), rows(w_lo.shape[1])],
        out_shape=[jax.ShapeDtypeStruct((n, w_sb.shape[1]), BF16), jax.ShapeDtypeStruct((n, w_rw.shape[1]), F32),
                   jax.ShapeDtypeStruct((n, w_lo.shape[1]), F32)],
        compiler_params=pltpu.CompilerParams(dimension_semantics=("arbitrary",),
                                             vmem_limit_bytes=VMEM_LIMIT),
        name="inproj",
    )(x2, gain, w_sb, w_rw, w_lo, gq, gk)


def _sbattn_kernel(q_ref, k_ref, v_ref, o_ref, *, blk, pairs):
    qb = pl.program_id(1)
    lane = lax.broadcasted_iota(jnp.int32, (1, pairs * LANES), 1)
    first = (lane % LANES) < HEAD_DIM
    qn = q_ref[...].astype(F32)
    q_first = jnp.where(first, qn, 0.0).astype(BF16)
    q_second = jnp.where(first, 0.0, qn).astype(BF16)
    lanes_of = lambda p: slice(p * LANES, (p + 1) * LANES)
    qq = [jnp.concatenate([q_first[:, lanes_of(p)], q_second[:, lanes_of(p)]], axis=0) for p in range(pairs)]

    jj = lax.broadcasted_iota(jnp.int32, (blk, blk), 0)
    ss = lax.broadcasted_iota(jnp.int32, (blk, blk), 1)
    uu = jnp.where(jj >= ss, 1.0, 0.0).astype(BF16)

    rows = 2 * blk
    tt = lax.broadcasted_iota(jnp.int32, (pairs * rows, blk), 0) % blk
    sk = lax.broadcasted_iota(jnp.int32, (pairs * rows, blk), 1)
    causal = sk < tt

    def step(j, c, acc, diagonal):
        start = pl.multiple_of(j * blk, blk)
        z = jnp.concatenate([_nt(qq[p], k_ref[pl.ds(start, blk), lanes_of(p)]) for p in range(pairs)], axis=0)
        lk = -_softplus(z)
        if diagonal:
            lk = jnp.where(causal, lk, 0.0)
        sums = _mm_exact_rhs(lk, uu)
        w = jnp.exp(z + sums + c)
        if diagonal:
            w = jnp.where(causal, w, 0.0)
        w = w.astype(BF16)
        pv = jnp.concatenate([_mm(w[p * rows:(p + 1) * rows], v_ref[pl.ds(start, blk), lanes_of(p)])
                              for p in range(pairs)], axis=0)
        c = c + sums[:, 0:1]
        return c, acc + pv, jnp.max(c) > UNDERFLOW_LOG

    carry = (qb - 1,) + step(qb, jnp.zeros((pairs * rows, 1), F32), jnp.zeros((pairs * rows, LANES), F32), True)
    _, _, acc, _ = lax.while_loop(lambda cr: (cr[0] >= 0) & cr[3],
                                  lambda cr: (cr[0] - 1,) + step(cr[0], cr[1], cr[2], False), carry)
    first_pair = first[:, :LANES]
    for p in range(pairs):
        o_ref[:, lanes_of(p)] = jnp.where(first_pair, acc[p * rows:p * rows + blk], acc[p * rows + blk:(p + 1) * rows])


def _sbattn(sb3, blk):
    b, s, w3 = sb3.shape
    width = w3 // 3
    return pl.pallas_call(
        functools.partial(_sbattn_kernel, blk=blk, pairs=width // LANES),
        grid=(b, s // blk),
        in_specs=[pl.BlockSpec((None, blk, width), lambda bi, qb: (bi, qb, 0)),
                  pl.BlockSpec((None, s, width), lambda bi, qb: (bi, 0, 1)),
                  pl.BlockSpec((None, s, width), lambda bi, qb: (bi, 0, 2))],
        out_specs=pl.BlockSpec((None, blk, width), lambda bi, qb: (bi, qb, 0)),
        out_shape=jax.ShapeDtypeStruct((b, s, width), F32),
        compiler_params=pltpu.CompilerParams(dimension_semantics=("arbitrary",) * 2,
                                             vmem_limit_bytes=VMEM_LIMIT),
        name="sbattn",
    )(sb3, sb3, sb3)


def _token_shift(p, prev_ref, mu, first_tile):
    rows = lax.broadcasted_iota(jnp.int32, (p.shape[0], 1), 0)
    last_prev = jnp.where(first_tile, 0.0, prev_ref[7:8, :])
    prev = jnp.where(rows == 0, last_prev, pltpu.roll(p, 1, axis=0))
    return p + (prev - p) * mu


def _rwprep_stage(rw_ref, rwp_ref, lo_ref, lop_ref, mur_ref, mul_ref, w0_ref, a0_ref, kk_ref, ka_ref,
                  wup_ref, aup_ref, gup_ref, *, width):
    first_tile = pl.program_id(1) == 0
    pf = _token_shift(rw_ref[...], rwp_ref, mur_ref[...], first_tile)
    lf = _token_shift(lo_ref[...], lop_ref, mul_ref[...], first_tile)
    r = pf[:, :width]
    k = pf[:, width:2 * width]
    v = pf[:, 2 * width:]
    w = -_softplus(-(w0_ref[...] + _mm(jnp.tanh(lf).astype(BF16), wup_ref[...]))) - 0.5
    lr = _sigmoid(a0_ref[...] + _mm(lf.astype(BF16), aup_ref[...]))
    kk = k * kk_ref[...]
    ss = _mm_exact_rhs(kk * kk, _head_ones(width))
    kk = kk / jnp.maximum(jnp.sqrt(ss), 1e-12)
    log_decay = -jnp.exp(w)
    gate = _mm(_sigmoid(lf).astype(BF16), gup_ref[...])
    return r, log_decay, k * (1.0 + (lr - 1.0) * ka_ref[...]), v, -kk, kk * lr, gate


def _rwscan_kernel(rw_ref, rwp_ref, lo_ref, lop_ref, mur_ref, mul_ref, w0_ref, a0_ref, kk_ref, ka_ref,
                   wup_ref, aup_ref, gup_ref, rk_ref, lnw_ref, lnb_ref,
                   o_ref, st_ref, y_ref, *, tb, width):
    r_all, lw_all, k_all, v_all, a_all, b_all, g_all = _rwprep_stage(
        rw_ref, rwp_ref, lo_ref, lop_ref, mur_ref, mul_ref, w0_ref, a0_ref, kk_ref, ka_ref,
        wup_ref, aup_ref, gup_ref, width=width)

    t = RW_CHUNK
    n = LANES
    m = MXU_DIM
    groups = width // m
    pairs = width // n

    @pl.when(pl.program_id(1) == 0)
    def _():
        st_ref[...] = jnp.zeros_like(st_ref)

    ri = lax.broadcasted_iota(jnp.int32, (2 * n, 2 * n), 0)
    ci = lax.broadcasted_iota(jnp.int32, (2 * n, 2 * n), 1)
    same = ((ri % n) // HEAD_DIM) == ((ci % n) // HEAD_DIM)
    gmask = same & ((ci % HEAD_DIM) < (ri % HEAD_DIM) + jnp.where(ri >= n, 1, 0))
    rn = lax.broadcasted_iota(jnp.int32, (n, n), 0)
    cn = lax.broadcasted_iota(jnp.int32, (n, n), 1)
    bdmask = jnp.where((rn // HEAD_DIM) == (cn // HEAD_DIM), 1.0, 0.0)
    eye = jnp.where(rn == cn, 1.0, 0.0)
    ti = lax.broadcasted_iota(jnp.int32, (t, 4 * t), 0)
    si = lax.broadcasted_iota(jnp.int32, (t, 4 * t), 1)
    tri3 = jnp.where((si % t <= ti) & (si < 3 * t), 1.0, 0.0).astype(BF16)
    right_half = jnp.where(lax.broadcasted_iota(jnp.int32, (n, 2 * n), 1) >= n, 1.0, 0.0)
    cat = jnp.concatenate

    def blockdiag32(x):
        return cat([x, x], axis=0) * bdmask

    def prepare(rows, cols):
        r = r_all[rows, cols]
        lw = lw_all[rows, cols]
        k = k_all[rows, cols]
        v = v_all[rows, cols]
        a = a_all[rows, cols]
        b = b_all[rows, cols]

        l1, l2, l3 = _split3(lw)
        cum = _mm(tri3, cat([l1, l2, l3, l1], axis=0))
        yield
        cum_end = cum[t - 1:t, :]
        p_inv = jnp.exp(-cum)
        p_rest = jnp.exp(cum_end - cum)
        r32 = blockdiag32(r * jnp.exp(cum))
        r_bd = r32.astype(BF16)
        a_bd = blockdiag32(a * jnp.exp(cum - lw)).astype(BF16)
        k_bd = blockdiag32(k * p_inv).astype(BF16)
        b_bd = blockdiag32(b * p_inv).astype(BF16)
        kd_bd = blockdiag32(k * p_rest).astype(BF16)
        bd_bd = blockdiag32(b * p_rest).astype(BF16)
        v32 = blockdiag32(v)
        v_bd = v32.astype(BF16)
        zero_v = (cat([v32, v32], axis=1) * right_half).astype(BF16)

        g = jnp.where(gmask, _nt(cat([a_bd, r_bd], axis=0), cat([b_bd, k_bd], axis=0)), 0.0)
        yield
        aab = g[:n, :n]
        aak = g[:n, n:].astype(BF16)
        brb_brk = g[n:, :].astype(BF16)

        x = eye + aab
        p = aab.astype(BF16)
        p = _mm(p, p).astype(BF16)
        av = _mm(aak, v_bd).astype(BF16)
        yield
        steps = (t - 1).bit_length() - 1
        for i in range(steps):
            if i + 1 < steps:
                px_pp = _mm(p, cat([x.astype(BF16), p], axis=1))
                x = x + px_pp[:, :n]
                p = px_pp[:, n:].astype(BF16)
            else:
                x = x + _mm(p, x.astype(BF16))
            yield
        tinv = x.astype(BF16)

        w_u0 = _mm(tinv, cat([a_bd, av], axis=1)).astype(BF16)
        yield
        stack = cat([w_u0, zero_v], axis=0)
        q_y0 = _mm(brb_brk, stack)
        m_c0 = _tn(cat([bd_bd, kd_bd], axis=0), stack)
        yield
        q_mat = (r32 + q_y0[:, :n]).astype(BF16)
        m_mat = (eye * jnp.exp(cum_end) + m_c0[:, :n]).astype(BF16)
        lhs = cat([cat([q_mat, q_mat], axis=1), cat([m_mat, m_mat], axis=1)], axis=0)
        return lhs, q_y0[:, n:], m_c0[:, n:]

    def lockstep(generators):
        results = [None] * len(generators)
        live = list(enumerate(generators))
        while live:
            still = []
            for idx, gen in live:
                try:
                    next(gen)
                    still.append((idx, gen))
                except StopIteration as done:
                    results[idx] = done.value
            live = still
        return results

    lane_cols = [slice(pr * n, (pr + 1) * n) for pr in range(pairs)]
    time_rows = [slice(ch * t, (ch + 1) * t) for ch in range(tb // t)]
    units = [(rows, cols) for rows in time_rows for cols in lane_cols]
    prepared = []
    for first in range(0, len(units), RW_WAVE):
        prepared += lockstep([prepare(rows, cols) for rows, cols in units[first:first + RW_WAVE]])
    states = [st_ref[pr] for pr in range(pairs)]
    for ch, rows in enumerate(time_rows):
        for pr, cols in enumerate(lane_cols):
            lhs, y0, c0 = prepared[ch * pairs + pr]
            y_s = _mm(lhs, cat(_split2(states[pr]), axis=0))
            y = y_s[:n] + y0
            y_ref[rows, cols] = y[:t] + y[t:]
            states[pr] = y_s[n:] + c0
    for pr in range(pairs):
        st_ref[pr] = states[pr]

    ones = _head_ones(m)
    inv = 1.0 / HEAD_DIM
    for grp in range(groups):
        cols = slice(grp * m, (grp + 1) * m)
        y = y_ref[:, cols]
        mean = _mm_exact_rhs(y, ones) * inv
        d = y - mean
        var = _mm_exact_rhs(d * d, ones) * inv
        yn = d * lax.rsqrt(var + LN_X_EPS) * lnw_ref[:, cols] + lnb_ref[:, cols]
        bonus = _mm_exact_rhs(r_all[:, cols] * k_all[:, cols] * rk_ref[:, cols], ones) * v_all[:, cols]
        o_ref[:, cols] = (yn + bonus) * g_all[:, cols]


def _rwscan(rw3, lo3, mu_rw, mu_lo, w0, a0, k_k, k_a, wup, aup, gup, r_k, ln_w, ln_b, tb):
    bsz, s, w3 = rw3.shape
    width = w3 // 3
    nlo = lo3.shape[2]
    cur = lambda c: pl.BlockSpec((None, tb, c), lambda bi, i: (bi, i, 0))
    prev = lambda c: pl.BlockSpec((None, SUBLANES, c),
                                  lambda bi, i: (bi, jnp.maximum(i * (tb // SUBLANES) - 1, 0), 0))
    full = lambda x: pl.BlockSpec(x.shape, lambda bi, i: (0,) * x.ndim)
    params = (mu_rw, mu_lo, w0, a0, k_k, k_a, wup, aup, gup, r_k, ln_w, ln_b)
    return pl.pallas_call(
        functools.partial(_rwscan_kernel, tb=tb, width=width),
        grid=(bsz, s // tb),
        in_specs=[cur(w3), prev(w3), cur(nlo), prev(nlo)] + [full(p) for p in params],
        out_specs=cur(width),
        out_shape=jax.ShapeDtypeStruct((bsz, s, width), F32),
        scratch_shapes=[pltpu.VMEM((width // LANES, LANES, LANES), F32)] + [pltpu.VMEM((tb, width), F32)],
        compiler_params=pltpu.CompilerParams(dimension_semantics=("arbitrary",) * 2,
                                             vmem_limit_bytes=VMEM_LIMIT),
        name="rwscan",
    )(rw3, rw3, lo3, lo3, *params)


def _outproj_kernel(sb_ref, rw_ref, x_ref, wsb_ref, wrw_ref, g_ref, wr_ref,
                    x1_ref, h2_ref, meta_ref, route_ref, cnt_ref, run_ref, *, tm):
    @pl.when(pl.program_id(0) == 0)
    def _():
        run_ref[...] = jnp.zeros_like(run_ref)

    x1 = x_ref[...] + _mm(sb_ref[...].astype(BF16), wsb_ref[...]) + _mm(rw_ref[...].astype(BF16), wrw_ref[...])
    x1_ref[...] = x1
    ms = jnp.mean(x1 * x1, axis=-1, keepdims=True)
    h2 = x1 * lax.rsqrt(ms + NORM_EPS) * g_ref[...]
    _store_row_tiles(h2_ref, 0, h2)

    h_hi, h_lo = _split2(h2)
    w_hi, w_lo = _split2(wr_ref[...])
    lg = _mm(h_hi, w_hi) + _mm(h_hi, w_lo) + _mm(h_lo, w_hi)

    lane = lax.broadcasted_iota(jnp.int32, (tm, LANES), 1).astype(F32)
    neg = -jnp.inf
    big = float(LANES)
    is_group = lane < N_GROUPS
    gl = jnp.where(is_group, lg, neg)
    gmax = jnp.max(gl, axis=1, keepdims=True)
    gidx = jnp.min(jnp.where(gl == gmax, lane, big), axis=1, keepdims=True)
    group_gate = 1.0 / jnp.sum(jnp.where(is_group, jnp.exp(lg - gmax), 0.0), axis=1, keepdims=True)
    lo_lane = N_GROUPS + EXPERTS_PER_GROUP * gidx
    el = jnp.where((lane >= lo_lane) & (lane < lo_lane + EXPERTS_PER_GROUP), lg, neg)
    m1 = jnp.max(el, axis=1, keepdims=True)
    i1 = jnp.min(jnp.where(el == m1, lane, big), axis=1, keepdims=True)
    el2 = jnp.where(lane == i1, neg, el)
    m2 = jnp.max(el2, axis=1, keepdims=True)
    i2 = jnp.min(jnp.where(el2 == m2, lane, big), axis=1, keepdims=True)
    p2 = jnp.exp(m2 - m1)
    gate1 = group_gate / (1.0 + p2)
    gate2 = group_gate * p2 / (1.0 + p2)

    hit1 = lane == i1
    hit2 = lane == i2
    onehot = jnp.where(hit1 | hit2, 1.0, 0.0)
    rr = lax.broadcasted_iota(jnp.int32, (tm, tm), 0)
    cc = lax.broadcasted_iota(jnp.int32, (tm, tm), 1)
    below = jnp.where(cc < rr, 1.0, 0.0).astype(BF16)
    before = run_ref[...] + _mm(below, onehot.astype(BF16))
    rank1 = jnp.sum(jnp.where(hit1, before, 0.0), axis=1, keepdims=True)
    rank2 = jnp.sum(jnp.where(hit2, before, 0.0), axis=1, keepdims=True)
    run = run_ref[...] + jnp.sum(onehot, axis=0, keepdims=True)
    run_ref[...] = run
    cnt_ref[...] = run

    vals = (i1 - N_GROUPS, i2 - N_GROUPS, gate1, gate2, rank1, rank2)
    meta = jnp.zeros((tm, LANES), F32)
    for pos, val in enumerate(vals):
        meta = jnp.where(lane == pos, val.astype(F32), meta)
    meta_ref[...] = meta
    route_ref[...] = meta.T[:SUBLANES]


def _outproj(sb_out, rw_out, x2, w_sb, w_rw, gain, w_router, tm):
    n, d = x2.shape
    rows = lambda c: pl.BlockSpec((tm, c), lambda i: (i, 0))
    full = lambda a: pl.BlockSpec(a.shape, lambda i: (0,) * a.ndim)
    return pl.pallas_call(
        functools.partial(_outproj_kernel, tm=tm),
        grid=(n // tm,),
        in_specs=[rows(sb_out.shape[1]), rows(rw_out.shape[1]), rows(d), full(w_sb), full(w_rw), full(gain),
                  full(w_router)],
        out_specs=[rows(d), pl.BlockSpec((tm * (d // LANES), LANES), lambda i: (i, 0)), rows(LANES),
                   pl.BlockSpec((SUBLANES, tm), lambda i: (0, i)), pl.BlockSpec((1, LANES), lambda i: (0, 0))],
        out_shape=[jax.ShapeDtypeStruct((n, d), F32), jax.ShapeDtypeStruct((n * (d // LANES), LANES), F32),
                   jax.ShapeDtypeStruct((n, LANES), F32), jax.ShapeDtypeStruct((SUBLANES, n), F32),
                   jax.ShapeDtypeStruct((1, LANES), F32)],
        scratch_shapes=[pltpu.VMEM((1, LANES), F32)],
        compiler_params=pltpu.CompilerParams(dimension_semantics=("arbitrary",),
                                             vmem_limit_bytes=VMEM_LIMIT),
        name="outproj",
    )(sb_out, rw_out, x2, w_sb, w_rw, gain, w_router)


def _start_tile_gather(src_ref, idx_ref, idx_base, dst_ref, dst_base, sem, n_rows):
    def issue(r, _):
        _tile_copy(src_ref, idx_ref[idx_base + r], dst_ref, dst_base + r, sem).start()
        return 0

    lax.fori_loop(0, n_rows, issue, 0, unroll=8)


def _wait_tile_gather(src_ref, dst_ref, sem, n_rows):
    def drain(r, _):
        _tile_copy(src_ref, 0, dst_ref, 0, sem).wait()
        return 0

    lax.fori_loop(0, n_rows, drain, 0, unroll=8)


def _srcmap_kernel(dest_ref, seg_ref, src_ref):
    def put(tok, _):
        for kk in range(TOP_K):
            src_ref[dest_ref[TOP_K * tok + kk]] = tok
        return 0

    lax.fori_loop(0, dest_ref.shape[0] // TOP_K, put, 0, unroll=8)

    def pad(e, _):
        first = seg_ref[0, e]

        def clear(r, _):
            src_ref[first + r] = 0
            return 0

        lax.fori_loop(seg_ref[1, e], seg_ref[2, e], clear, 0)
        return 0

    lax.fori_loop(0, seg_ref.shape[1], pad, 0)


def _srcmap(dest_flat, segments, n_rows):
    return pl.pallas_call(
        _srcmap_kernel,
        in_specs=[pl.BlockSpec(memory_space=pltpu.SMEM), pl.BlockSpec(memory_space=pltpu.SMEM)],
        out_specs=pl.BlockSpec(memory_space=pltpu.SMEM),
        out_shape=jax.ShapeDtypeStruct((n_rows,), jnp.int32),
        name="srcmap",
    )(dest_flat, segments)


def _experts_kernel(eid_ref, used_ref, src_ref, h_ref, wg_ref, wu_ref, wd_ref, o_ref,
                    x_ref, wgb_ref, wub_ref, wdb_ref, sems):
    i = pl.program_id(0)
    used = used_ref[0]
    slot = lax.rem(i, GATHER_DEPTH)
    rows = EXPERT_ROWS

    def start_block(block):
        dst = lax.rem(block, GATHER_DEPTH)
        _start_tile_gather(h_ref, src_ref, block * rows, x_ref, dst * rows, sems.at[dst], rows)

    for ahead in range(GATHER_DEPTH - 1):
        @pl.when((i == 0) & (ahead < used))
        def _():
            start_block(ahead)

    @pl.when(i + GATHER_DEPTH - 1 < used)
    def _():
        start_block(i + GATHER_DEPTH - 1)

    @pl.when((i == 0) | (eid_ref[i] != eid_ref[jnp.maximum(i - 1, 0)]))
    def _():
        wgb_ref[...] = wg_ref[...].astype(BF16)
        wub_ref[...] = wu_ref[...].astype(BF16)
        wdb_ref[...] = wd_ref[...].astype(BF16)

    @pl.when(i < used)
    def _():
        _wait_tile_gather(h_ref, x_ref, sems.at[slot], rows)
        xb = _load_row_tiles(x_ref, slot * (rows * SUBLANES), rows, SUBLANES).astype(BF16)
        hg = _mm(xb, wgb_ref[...])
        hu = _mm(xb, wub_ref[...])
        act = hg * _sigmoid(hg) * hu
        _store_row_tiles(o_ref, 0, _mm(act.astype(BF16), wdb_ref[...]))

    @pl.when(i >= used)
    def _():
        o_ref[...] = jnp.zeros_like(o_ref)


def _experts(blk_eid, n_used, src_rows, h2_tiles, w_g, w_u, w_d):
    n_blocks = src_rows.shape[0] // EXPERT_ROWS
    d, de = w_g.shape[1], w_g.shape[2]
    assert d == SUBLANES * LANES
    w_map = lambda i, eid, used, src: (eid[i], 0, 0)
    grid_spec = pltpu.PrefetchScalarGridSpec(
        num_scalar_prefetch=3,
        grid=(n_blocks,),
        in_specs=[pl.BlockSpec(memory_space=pl.ANY),
                  pl.BlockSpec((None, d, de), w_map), pl.BlockSpec((None, d, de), w_map),
                  pl.BlockSpec((None, de, d), w_map)],
        out_specs=pl.BlockSpec((EXPERT_ROWS * SUBLANES, LANES), lambda i, eid, used, src: (i, 0)),
        scratch_shapes=[pltpu.VMEM((GATHER_DEPTH * EXPERT_ROWS * SUBLANES, LANES), F32),
                        pltpu.VMEM((d, de), BF16), pltpu.VMEM((d, de), BF16), pltpu.VMEM((de, d), BF16),
                        pltpu.SemaphoreType.DMA((GATHER_DEPTH,))],
    )
    return pl.pallas_call(
        _experts_kernel,
        grid_spec=grid_spec,
        out_shape=jax.ShapeDtypeStruct((n_blocks * EXPERT_ROWS * SUBLANES, LANES), F32),
        compiler_params=pltpu.CompilerParams(dimension_semantics=("arbitrary",),
                                             vmem_limit_bytes=VMEM_LIMIT),
        name="experts",
    )(blk_eid, n_used, src_rows, h2_tiles, w_g, w_u, w_d)


def _combine_kernel(dest_ref, rows_ref, x1_ref, meta_ref, o_ref, got_ref, sems, *, tm):
    i = pl.program_id(0)
    slot = i % 2
    slot_rows = TOP_K * tm

    def start_tile(tile, dst_slot):
        def issue(t, _):
            for kk in range(TOP_K):
                _tile_copy(rows_ref, dest_ref[tile * slot_rows + TOP_K * t + kk], got_ref,
                           dst_slot * slot_rows + kk * tm + t, sems.at[dst_slot]).start(priority=kk)
            return 0

        lax.fori_loop(0, tm, issue, 0, unroll=4)

    @pl.when(i == 0)
    def _():
        start_tile(0, 0)

    @pl.when(i + 1 < pl.num_programs(0))
    def _():
        start_tile(i + 1, 1 - slot)

    _wait_tile_gather(rows_ref, got_ref, sems.at[slot], slot_rows)
    meta = meta_ref[...]
    y = x1_ref[...]
    for kk in range(TOP_K):
        y = y + _load_row_tiles(got_ref, (slot * slot_rows + kk * tm) * SUBLANES, tm, SUBLANES) * meta[:, 2 + kk:3 + kk]
    o_ref[...] = y


def _combine(dest_flat, exp_out_tiles, x1, meta, tm):
    n, d = x1.shape
    rows = lambda c: pl.BlockSpec((tm, c), lambda i, dest: (i, 0))
    grid_spec = pltpu.PrefetchScalarGridSpec(
        num_scalar_prefetch=1,
        grid=(n // tm,),
        in_specs=[pl.BlockSpec(memory_space=pl.ANY), rows(d), rows(LANES)],
        out_specs=rows(d),
        scratch_shapes=[pltpu.VMEM((2 * TOP_K * tm * SUBLANES, LANES), F32), pltpu.SemaphoreType.DMA((2,))],
    )
    return pl.pallas_call(
        functools.partial(_combine_kernel, tm=tm),
        grid_spec=grid_spec,
        out_shape=jax.ShapeDtypeStruct((n, d), F32),
        compiler_params=pltpu.CompilerParams(dimension_semantics=("arbitrary",),
                                             vmem_limit_bytes=VMEM_LIMIT),
        name="combine",
    )(dest_flat, exp_out_tiles, x1, meta)


def _layer(x, norm1_gain, w_in, sb_q_gain, sb_k_gain, rw_shift_mu, rw_w0, rw_w_up, rw_a0, rw_a_up, rw_g_up,
           rw_k_k, rw_k_a, rw_r_k, rw_ln_w, rw_ln_b, w_out, norm2_gain, w_router_group, w_router_expert,
           w_exp_gate, w_exp_up, w_exp_down):
    bsz, s, d = x.shape
    n = bsz * s
    sbw = d // 2
    rww = d - sbw
    n_decay, n_aaa = rw_w_up.shape[0], rw_a_up.shape[0]
    n_lora = n_decay + n_aaa + rw_g_up.shape[0]
    tm = min(256, n)
    row = lambda a: a.reshape(1, -1).astype(F32)

    x2 = x.reshape(n, d)
    w_bf = w_in.astype(BF16)
    heads = lambda gvec: jnp.tile(row(gvec), (1, sbw // HEAD_DIM))
    sb, rw, lo = _inproj(x2, row(norm1_gain), w_bf[:, :3 * sbw], w_bf[:, 3 * sbw:3 * sbw + 3 * rww],
                         w_bf[:, 3 * sbw + 3 * rww:], heads(sb_q_gain), heads(sb_k_gain), min(IN_TILE, n))
    sb_out = _sbattn(sb.reshape(bsz, s, 3 * sbw), min(SB_BLOCK, s))

    pad_rows = lambda w_up, start: jnp.zeros((n_lora, rww), BF16).at[start:start + w_up.shape[0]].set(
        w_up.astype(BF16))
    mu = row(rw_shift_mu)
    rw_out = _rwscan(rw.reshape(bsz, s, 3 * rww), lo.reshape(bsz, s, n_lora), mu[:, :3 * rww], mu[:, 3 * rww:],
                     row(rw_w0), row(rw_a0), row(rw_k_k), row(rw_k_a),
                     pad_rows(rw_w_up, 0), pad_rows(rw_a_up, n_decay), pad_rows(rw_g_up, n_decay + n_aaa),
                     row(rw_r_k), row(rw_ln_w), row(rw_ln_b), min(RW_TILE, s))

    w_out_bf = w_out.astype(BF16)
    w_router = jnp.zeros((d, LANES), F32).at[:, :N_GROUPS].set(w_router_group).at[
        :, N_GROUPS:N_GROUPS + N_EXPERTS].set(w_router_expert)
    x1, h2, meta, route, counts = _outproj(sb_out.reshape(n, sbw), rw_out.reshape(n, rww), x2, w_out_bf[:sbw],
                                    w_out_bf[sbw:], row(norm2_gain), w_router, tm)

    counts = counts[0, N_GROUPS:N_GROUPS + N_EXPERTS].astype(jnp.int32)
    padded = (counts + EXPERT_ROWS - 1) // EXPERT_ROWS * EXPERT_ROWS
    pad_end = jnp.cumsum(padded)
    pad_start = pad_end - padded
    eid = route[0:TOP_K].astype(jnp.int32)
    rank = route[4:4 + TOP_K].astype(jnp.int32)
    segment_start = jnp.sum(jnp.where(eid[None] == jnp.arange(N_EXPERTS, dtype=jnp.int32)[:, None, None],
                                      pad_start[:, None, None], 0), axis=0)
    dest = (segment_start + rank).T
    dest_flat = dest.reshape(-1)
    n_rows = n * TOP_K + N_EXPERTS * EXPERT_ROWS
    n_blocks = n_rows // EXPERT_ROWS
    block_start = jnp.arange(n_blocks, dtype=jnp.int32) * EXPERT_ROWS
    blk_eid = jnp.minimum(jnp.sum((pad_end[None, :] <= block_start[:, None]).astype(jnp.int32), axis=1),
                          N_EXPERTS - 1)
    n_used = (pad_end[-1:] // EXPERT_ROWS).astype(jnp.int32)

    tail = jnp.zeros((1,), jnp.int32)
    segments = jnp.stack([jnp.concatenate([pad_start, pad_end[-1:]]), jnp.concatenate([counts, tail]),
                          jnp.concatenate([padded, n_rows - pad_end[-1:]])]).astype(jnp.int32)
    src_rows = _srcmap(dest_flat, segments, n_rows)

    exp_out = _experts(blk_eid, n_used, src_rows, h2, w_exp_gate, w_exp_up, w_exp_down)
    return _combine(dest_flat, exp_out, x1, meta, tm).reshape(bsz, s, d)


def kernel(x, norm1_gain, w_in, sb_q_gain, sb_k_gain, rw_shift_mu, rw_w0, rw_w_up, rw_a0, rw_a_up, rw_g_up,
           rw_k_k, rw_k_a, rw_r_k, rw_ln_w, rw_ln_b, w_out, norm2_gain, w_router_group, w_router_expert,
           w_exp_gate, w_exp_up, w_exp_down):
    params = (norm1_gain, w_in, sb_q_gain, sb_k_gain, rw_shift_mu, rw_w0, rw_w_up, rw_a0, rw_a_up, rw_g_up,
              rw_k_k, rw_k_a, rw_r_k, rw_ln_w, rw_ln_b, w_out, norm2_gain, w_router_group, w_router_expert,
              w_exp_gate, w_exp_up, w_exp_down)
    for layer in range(norm1_gain.shape[0]):
        x = _layer(x, *(p[layer] for p in params))
    return x
```

```python
import functools

import jax
import jax.numpy as jnp
from jax import lax
from jax.experimental import pallas as pl
from jax.experimental.pallas import tpu as pltpu

F32 = jnp.float32
BF16 = jnp.bfloat16

HEAD_DIM = 64
NORM_EPS = 1e-6
LN_X_EPS = 64e-5
N_GROUPS = 4
EXPERTS_PER_GROUP = 8
N_EXPERTS = N_GROUPS * EXPERTS_PER_GROUP
TOP_K = 2

LANES = 128
SUBLANES = 8
MXU_DIM = 256
RW_CHUNK = 64
IN_TILE = 512
RW_TILE = 256
RW_WAVE = 16
GATHER_DEPTH = 3
SB_BLOCK = 256
EXPERT_ROWS = 512
UNDERFLOW_LOG = -110.0
VMEM_LIMIT = 56 * 1024 * 1024


def _nt(a, b):
    return lax.dot_general(a, b, (((1,), (1,)), ((), ())), preferred_element_type=F32)


def _tn(a, b):
    return lax.dot_general(a, b, (((0,), (0,)), ((), ())), preferred_element_type=F32)


def _mm(a, b):
    return jnp.dot(a, b, preferred_element_type=F32)


def _split2(x):
    hi = x.astype(BF16)
    lo = (x - hi.astype(F32)).astype(BF16)
    return hi, lo


def _split3(x):
    h1 = x.astype(BF16)
    r1 = x - h1.astype(F32)
    h2 = r1.astype(BF16)
    h3 = (r1 - h2.astype(F32)).astype(BF16)
    return h1, h2, h3


def _mm_exact_rhs(x, m):
    hi, lo = _split2(x)
    if 2 * x.shape[1] <= MXU_DIM:
        return _mm(jnp.concatenate([hi, lo], axis=1), jnp.concatenate([m, m], axis=0))
    return _mm(hi, m) + _mm(lo, m)


def _head_ones(width):
    r = lax.broadcasted_iota(jnp.int32, (width, width), 0) // HEAD_DIM
    c = lax.broadcasted_iota(jnp.int32, (width, width), 1) // HEAD_DIM
    return jnp.where(r == c, 1.0, 0.0).astype(BF16)


def _softplus(z):
    return jnp.maximum(z, 0.0) + jnp.log(1.0 + jnp.exp(-jnp.abs(z)))


def _sigmoid(z):
    return 1.0 / (1.0 + jnp.exp(-z))


def _store_row_tiles(ref, base, x):
    pieces = x.shape[1] // LANES
    for s in range(pieces):
        ref[pl.ds(base + s, x.shape[0], stride=pieces), :] = x[:, s * LANES:(s + 1) * LANES]


def _load_row_tiles(ref, base, rows, pieces):
    return jnp.concatenate([ref[pl.ds(base + s, rows, stride=pieces), :] for s in range(pieces)], axis=1)


def _tile_copy(src_ref, src_row, dst_ref, dst_row, sem):
    src = src_ref.at[pl.ds(pl.multiple_of(src_row * SUBLANES, SUBLANES), SUBLANES)]
    dst = dst_ref.at[pl.ds(pl.multiple_of(dst_row * SUBLANES, SUBLANES), SUBLANES)]
    return pltpu.make_async_copy(src, dst, sem)


def _head_rms_norm(x, gain):
    ones = _head_ones(MXU_DIM)
    groups = [x[:, c:c + MXU_DIM] for c in range(0, x.shape[1], MXU_DIM)]
    ms = jnp.concatenate([_mm_exact_rhs(g * g, ones) for g in groups], axis=1) * (1.0 / HEAD_DIM)
    return x * lax.rsqrt(ms + NORM_EPS) * gain


def _inproj_kernel(x_ref, g_ref, wsb_ref, wrw_ref, wlo_ref, gq_ref, gk_ref, sb_ref, rw_ref, lo_ref):
    x = x_ref[...]
    ms = jnp.mean(x * x, axis=-1, keepdims=True)
    h = (x * lax.rsqrt(ms + NORM_EPS) * g_ref[...]).astype(BF16)
    rw_ref[...] = _mm(h, wrw_ref[...])
    lo_ref[...] = _mm(h, wlo_ref[...])
    sb = _mm(h, wsb_ref[...])
    width = gq_ref.shape[1]
    sb_ref[:, :width] = (_head_rms_norm(sb[:, :width], gq_ref[...]) * (HEAD_DIM ** -0.5)).astype(BF16)
    sb_ref[:, width:2 * width] = _head_rms_norm(sb[:, width:2 * width], gk_ref[...]).astype(BF16)
    sb_ref[:, 2 * width:] = sb[:, 2 * width:].astype(BF16)


def _inproj(x2, gain, w_sb, w_rw, w_lo, gq, gk, tm):
    n, d = x2.shape
    full = lambda a: pl.BlockSpec(a.shape, lambda i: (0,) * a.ndim)
    rows = lambda c: pl.BlockSpec((tm, c), lambda i: (i, 0))
    return pl.pallas_call(
        _inproj_kernel,
        grid=(n // tm,),
        in_specs=[rows(d), full(gain), full(w_sb), full(w_rw), full(w_lo), full(gq), full(gk)],
        out_specs=[rows(w_sb.shape[1]), rows(w_rw.shape[1]), rows(w_lo.shape[1])],
        out_shape=[jax.ShapeDtypeStruct((n, w_sb.shape[1]), BF16), jax.ShapeDtypeStruct((n, w_rw.shape[1]), F32),
                   jax.ShapeDtypeStruct((n, w_lo.shape[1]), F32)],
        compiler_params=pltpu.CompilerParams(dimension_semantics=("arbitrary",),
                                             vmem_limit_bytes=VMEM_LIMIT),
        name="inproj",
    )(x2, gain, w_sb, w_rw, w_lo, gq, gk)


def _sbattn_kernel(q_ref, k_ref, v_ref, o_ref, *, blk, pairs):
    qb = pl.program_id(1)
    lane = lax.broadcasted_iota(jnp.int32, (1, pairs * LANES), 1)
    first = (lane % LANES) < HEAD_DIM
    qn = q_ref[...].astype(F32)
    q_first = jnp.where(first, qn, 0.0).astype(BF16)
    q_second = jnp.where(first, 0.0, qn).astype(BF16)
    lanes_of = lambda p: slice(p * LANES, (p + 1) * LANES)
    qq = [jnp.concatenate([q_first[:, lanes_of(p)], q_second[:, lanes_of(p)]], axis=0) for p in range(pairs)]

    jj = lax.broadcasted_iota(jnp.int32, (blk, blk), 0)
    ss = lax.broadcasted_iota(jnp.int32, (blk, blk), 1)
    uu = jnp.where(jj >= ss, 1.0, 0.0).astype(BF16)

    rows = 2 * blk
    tt = lax.broadcasted_iota(jnp.int32, (pairs * rows, blk), 0) % blk
    sk = lax.broadcasted_iota(jnp.int32, (pairs * rows, blk), 1)
    causal = sk < tt

    def step(j, c, acc, diagonal):
        start = pl.multiple_of(j * blk, blk)
        z = jnp.concatenate([_nt(qq[p], k_ref[pl.ds(start, blk), lanes_of(p)]) for p in range(pairs)], axis=0)
        lk = -_softplus(z)
        if diagonal:
            lk = jnp.where(causal, lk, 0.0)
        sums = _mm_exact_rhs(lk, uu)
        w = jnp.exp(z + sums + c)
        if diagonal:
            w = jnp.where(causal, w, 0.0)
        w = w.astype(BF16)
        pv = jnp.concatenate([_mm(w[p * rows:(p + 1) * rows], v_ref[pl.ds(start, blk), lanes_of(p)])
                              for p in range(pairs)], axis=0)
        c = c + sums[:, 0:1]
        return c, acc + pv, jnp.max(c) > UNDERFLOW_LOG

    carry = (qb - 1,) + step(qb, jnp.zeros((pairs * rows, 1), F32), jnp.zeros((pairs * rows, LANES), F32), True)
    _, _, acc, _ = lax.while_loop(lambda cr: (cr[0] >= 0) & cr[3],
                                  lambda cr: (cr[0] - 1,) + step(cr[0], cr[1], cr[2], False), carry)
    first_pair = first[:, :LANES]
    for p in range(pairs):
        o_ref[:, lanes_of(p)] = jnp.where(first_pair, acc[p * rows:p * rows + blk], acc[p * rows + blk:(p + 1) * rows])


def _sbattn(sb3, blk):
    b, s, w3 = sb3.shape
    width = w3 // 3
    return pl.pallas_call(
        functools.partial(_sbattn_kernel, blk=blk, pairs=width // LANES),
        grid=(b, s // blk),
        in_specs=[pl.BlockSpec((None, blk, width), lambda bi, qb: (bi, qb, 0)),
                  pl.BlockSpec((None, s, width), lambda bi, qb: (bi, 0, 1)),
                  pl.BlockSpec((None, s, width), lambda bi, qb: (bi, 0, 2))],
        out_specs=pl.BlockSpec((None, blk, width), lambda bi, qb: (bi, qb, 0)),
        out_shape=jax.ShapeDtypeStruct((b, s, width), F32),
        compiler_params=pltpu.CompilerParams(dimension_semantics=("arbitrary",) * 2,
                                             vmem_limit_bytes=VMEM_LIMIT),
        name="sbattn",
    )(sb3, sb3, sb3)


def _token_shift(p, prev_ref, mu, first_tile):
    rows = lax.broadcasted_iota(jnp.int32, (p.shape[0], 1), 0)
    last_prev = jnp.where(first_tile, 0.0, prev_ref[7:8, :])
    prev = jnp.where(rows == 0, last_prev, pltpu.roll(p, 1, axis=0))
    return p + (prev - p) * mu


def _rwprep_stage(rw_ref, rwp_ref, lo_ref, lop_ref, mur_ref, mul_ref, w0_ref, a0_ref, kk_ref, ka_ref,
                  wup_ref, aup_ref, gup_ref, *, width):
    first_tile = pl.program_id(1) == 0
    pf = _token_shift(rw_ref[...], rwp_ref, mur_ref[...], first_tile)
    lf = _token_shift(lo_ref[...], lop_ref, mul_ref[...], first_tile)
    r = pf[:, :width]
    k = pf[:, width:2 * width]
    v = pf[:, 2 * width:]
    w = -_softplus(-(w0_ref[...] + _mm(jnp.tanh(lf).astype(BF16), wup_ref[...]))) - 0.5
    lr = _sigmoid(a0_ref[...] + _mm(lf.astype(BF16), aup_ref[...]))
    kk = k * kk_ref[...]
    ss = _mm_exact_rhs(kk * kk, _head_ones(width))
    kk = kk / jnp.maximum(jnp.sqrt(ss), 1e-12)
    log_decay = -jnp.exp(w)
    gate = _mm(_sigmoid(lf).astype(BF16), gup_ref[...])
    return r, log_decay, k * (1.0 + (lr - 1.0) * ka_ref[...]), v, -kk, kk * lr, gate


def _rwscan_kernel(rw_ref, rwp_ref, lo_ref, lop_ref, mur_ref, mul_ref, w0_ref, a0_ref, kk_ref, ka_ref,
                   wup_ref, aup_ref, gup_ref, rk_ref, lnw_ref, lnb_ref,
                   o_ref, st_ref, y_ref, *, tb, width):
    r_all, lw_all, k_all, v_all, a_all, b_all, g_all = _rwprep_stage(
        rw_ref, rwp_ref, lo_ref, lop_ref, mur_ref, mul_ref, w0_ref, a0_ref, kk_ref, ka_ref,
        wup_ref, aup_ref, gup_ref, width=width)

    t = RW_CHUNK
    n = LANES
    m = MXU_DIM
    groups = width // m
    pairs = width // n

    @pl.when(pl.program_id(1) == 0)
    def _():
        st_ref[...] = jnp.zeros_like(st_ref)

    ri = lax.broadcasted_iota(jnp.int32, (2 * n, 2 * n), 0)
    ci = lax.broadcasted_iota(jnp.int32, (2 * n, 2 * n), 1)
    same = ((ri % n) // HEAD_DIM) == ((ci % n) // HEAD_DIM)
    gmask = same & ((ci % HEAD_DIM) < (ri % HEAD_DIM) + jnp.where(ri >= n, 1, 0))
    rn = lax.broadcasted_iota(jnp.int32, (n, n), 0)
    cn = lax.broadcasted_iota(jnp.int32, (n, n), 1)
    bdmask = jnp.where((rn // HEAD_DIM) == (cn // HEAD_DIM), 1.0, 0.0)
    eye = jnp.where(rn == cn, 1.0, 0.0)
    ti = lax.broadcasted_iota(jnp.int32, (t, 4 * t), 0)
    si = lax.broadcasted_iota(jnp.int32, (t, 4 * t), 1)
    tri3 = jnp.where((si % t <= ti) & (si < 3 * t), 1.0, 0.0).astype(BF16)
    right_half = jnp.where(lax.broadcasted_iota(jnp.int32, (n, 2 * n), 1) >= n, 1.0, 0.0)
    cat = jnp.concatenate

    def blockdiag32(x):
        return cat([x, x], axis=0) * bdmask

    def prepare(rows, cols):
        r = r_all[rows, cols]
        lw = lw_all[rows, cols]
        k = k_all[rows, cols]
        v = v_all[rows, cols]
        a = a_all[rows, cols]
        b = b_all[rows, cols]

        l1, l2, l3 = _split3(lw)
        cum = _mm(tri3, cat([l1, l2, l3, l1], axis=0))
        yield
        cum_end = cum[t - 1:t, :]
        p_inv = jnp.exp(-cum)
        p_rest = jnp.exp(cum_end - cum)
        r32 = blockdiag32(r * jnp.exp(cum))
        r_bd = r32.astype(BF16)
        a_bd = blockdiag32(a * jnp.exp(cum - lw)).astype(BF16)
        k_bd = blockdiag32(k * p_inv).astype(BF16)
        b_bd = blockdiag32(b * p_inv).astype(BF16)
        kd_bd = blockdiag32(k * p_rest).astype(BF16)
        bd_bd = blockdiag32(b * p_rest).astype(BF16)
        v32 = blockdiag32(v)
        v_bd = v32.astype(BF16)
        zero_v = (cat([v32, v32], axis=1) * right_half).astype(BF16)

        g = jnp.where(gmask, _nt(cat([a_bd, r_bd], axis=0), cat([b_bd, k_bd], axis=0)), 0.0)
        yield
        aab = g[:n, :n]
        aak = g[:n, n:].astype(BF16)
        brb_brk = g[n:, :].astype(BF16)

        x = eye + aab
        p = aab.astype(BF16)
        p = _mm(p, p).astype(BF16)
        av = _mm(aak, v_bd).astype(BF16)
        yield
        steps = (t - 1).bit_length() - 1
        for i in range(steps):
            if i + 1 < steps:
                px_pp = _mm(p, cat([x.astype(BF16), p], axis=1))
                x = x + px_pp[:, :n]
                p = px_pp[:, n:].astype(BF16)
            else:
                x = x + _mm(p, x.astype(BF16))
            yield
        tinv = x.astype(BF16)

        w_u0 = _mm(tinv, cat([a_bd, av], axis=1)).astype(BF16)
        yield
        stack = cat([w_u0, zero_v], axis=0)
        q_y0 = _mm(brb_brk, stack)
        m_c0 = _tn(cat([bd_bd, kd_bd], axis=0), stack)
        yield
        q_mat = (r32 + q_y0[:, :n]).astype(BF16)
        m_mat = (eye * jnp.exp(cum_end) + m_c0[:, :n]).astype(BF16)
        lhs = cat([cat([q_mat, q_mat], axis=1), cat([m_mat, m_mat], axis=1)], axis=0)
        return lhs, q_y0[:, n:], m_c0[:, n:]

    def lockstep(generators):
        results = [None] * len(generators)
        live = list(enumerate(generators))
        while live:
            still = []
            for idx, gen in live:
                try:
                    next(gen)
                    still.append((idx, gen))
                except StopIteration as done:
                    results[idx] = done.value
            live = still
        return results

    lane_cols = [slice(pr * n, (pr + 1) * n) for pr in range(pairs)]
    time_rows = [slice(ch * t, (ch + 1) * t) for ch in range(tb // t)]
    units = [(rows, cols) for rows in time_rows for cols in lane_cols]
    prepared = []
    for first in range(0, len(units), RW_WAVE):
        prepared += lockstep([prepare(rows, cols) for rows, cols in units[first:first + RW_WAVE]])
    states = [st_ref[pr] for pr in range(pairs)]
    for ch, rows in enumerate(time_rows):
        for pr, cols in enumerate(lane_cols):
            lhs, y0, c0 = prepared[ch * pairs + pr]
            y_s = _mm(lhs, cat(_split2(states[pr]), axis=0))
            y = y_s[:n] + y0
            y_ref[rows, cols] = y[:t] + y[t:]
            states[pr] = y_s[n:] + c0
    for pr in range(pairs):
        st_ref[pr] = states[pr]

    ones = _head_ones(m)
    inv = 1.0 / HEAD_DIM
    for grp in range(groups):
        cols = slice(grp * m, (grp + 1) * m)
        y = y_ref[:, cols]
        mean = _mm_exact_rhs(y, ones) * inv
        d = y - mean
        var = _mm_exact_rhs(d * d, ones) * inv
        yn = d * lax.rsqrt(var + LN_X_EPS) * lnw_ref[:, cols] + lnb_ref[:, cols]
        bonus = _mm_exact_rhs(r_all[:, cols] * k_all[:, cols] * rk_ref[:, cols], ones) * v_all[:, cols]
        o_ref[:, cols] = (yn + bonus) * g_all[:, cols]


def _rwscan(rw3, lo3, mu_rw, mu_lo, w0, a0, k_k, k_a, wup, aup, gup, r_k, ln_w, ln_b, tb):
    bsz, s, w3 = rw3.shape
    width = w3 // 3
    nlo = lo3.shape[2]
    cur = lambda c: pl.BlockSpec((None, tb, c), lambda bi, i: (bi, i, 0))
    prev = lambda c: pl.BlockSpec((None, SUBLANES, c),
                                  lambda bi, i: (bi, jnp.maximum(i * (tb // SUBLANES) - 1, 0), 0))
    full = lambda x: pl.BlockSpec(x.shape, lambda bi, i: (0,) * x.ndim)
    params = (mu_rw, mu_lo, w0, a0, k_k, k_a, wup, aup, gup, r_k, ln_w, ln_b)
    return pl.pallas_call(
        functools.partial(_rwscan_kernel, tb=tb, width=width),
        grid=(bsz, s // tb),
        in_specs=[cur(w3), prev(w3), cur(nlo), prev(nlo)] + [full(p) for p in params],
        out_specs=cur(width),
        out_shape=jax.ShapeDtypeStruct((bsz, s, width), F32),
        scratch_shapes=[pltpu.VMEM((width // LANES, LANES, LANES), F32)] + [pltpu.VMEM((tb, width), F32)],
        compiler_params=pltpu.CompilerParams(dimension_semantics=("arbitrary",) * 2,
                                             vmem_limit_bytes=VMEM_LIMIT),
        name="rwscan",
    )(rw3, rw3, lo3, lo3, *params)


def _outproj_kernel(sb_ref, rw_ref, x_ref, wsb_ref, wrw_ref, g_ref, wr_ref,
                    x1_ref, h2_ref, meta_ref, route_ref, cnt_ref, run_ref, *, tm):
    @pl.when(pl.program_id(0) == 0)
    def _():
        run_ref[...] = jnp.zeros_like(run_ref)

    x1 = x_ref[...] + _mm(sb_ref[...].astype(BF16), wsb_ref[...]) + _mm(rw_ref[...].astype(BF16), wrw_ref[...])
    x1_ref[...] = x1
    ms = jnp.mean(x1 * x1, axis=-1, keepdims=True)
    h2 = x1 * lax.rsqrt(ms + NORM_EPS) * g_ref[...]
    _store_row_tiles(h2_ref, 0, h2)

    h_hi, h_lo = _split2(h2)
    w_hi, w_lo = _split2(wr_ref[...])
    lg = _mm(h_hi, w_hi) + _mm(h_hi, w_lo) + _mm(h_lo, w_hi)

    lane = lax.broadcasted_iota(jnp.int32, (tm, LANES), 1).astype(F32)
    neg = -jnp.inf
    big = float(LANES)
    is_group = lane < N_GROUPS
    gl = jnp.where(is_group, lg, neg)
    gmax = jnp.max(gl, axis=1, keepdims=True)
    gidx = jnp.min(jnp.where(gl == gmax, lane, big), axis=1, keepdims=True)
    group_gate = 1.0 / jnp.sum(jnp.where(is_group, jnp.exp(lg - gmax), 0.0), axis=1, keepdims=True)
    lo_lane = N_GROUPS + EXPERTS_PER_GROUP * gidx
    el = jnp.where((lane >= lo_lane) & (lane < lo_lane + EXPERTS_PER_GROUP), lg, neg)
    m1 = jnp.max(el, axis=1, keepdims=True)
    i1 = jnp.min(jnp.where(el == m1, lane, big), axis=1, keepdims=True)
    el2 = jnp.where(lane == i1, neg, el)
    m2 = jnp.max(el2, axis=1, keepdims=True)
    i2 = jnp.min(jnp.where(el2 == m2, lane, big), axis=1, keepdims=True)
    p2 = jnp.exp(m2 - m1)
    gate1 = group_gate / (1.0 + p2)
    gate2 = group_gate * p2 / (1.0 + p2)

    hit1 = lane == i1
    hit2 = lane == i2
    onehot = jnp.where(hit1 | hit2, 1.0, 0.0)
    rr = lax.broadcasted_iota(jnp.int32, (tm, tm), 0)
    cc = lax.broadcasted_iota(jnp.int32, (tm, tm), 1)
    below = jnp.where(cc < rr, 1.0, 0.0).astype(BF16)
    before = run_ref[...] + _mm(below, onehot.astype(BF16))
    rank1 = jnp.sum(jnp.where(hit1, before, 0.0), axis=1, keepdims=True)
    rank2 = jnp.sum(jnp.where(hit2, before, 0.0), axis=1, keepdims=True)
    run = run_ref[...] + jnp.sum(onehot, axis=0, keepdims=True)
    run_ref[...] = run
    cnt_ref[...] = run

    vals = (i1 - N_GROUPS, i2 - N_GROUPS, gate1, gate2, rank1, rank2)
    meta = jnp.zeros((tm, LANES), F32)
    for pos, val in enumerate(vals):
        meta = jnp.where(lane == pos, val.astype(F32), meta)
    meta_ref[...] = meta
    route_ref[...] = meta.T[:SUBLANES]


def _outproj(sb_out, rw_out, x2, w_sb, w_rw, gain, w_router, tm):
    n, d = x2.shape
    rows = lambda c: pl.BlockSpec((tm, c), lambda i: (i, 0))
    full = lambda a: pl.BlockSpec(a.shape, lambda i: (0,) * a.ndim)
    return pl.pallas_call(
        functools.partial(_outproj_kernel, tm=tm),
        grid=(n // tm,),
        in_specs=[rows(sb_out.shape[1]), rows(rw_out.shape[1]), rows(d), full(w_sb), full(w_rw), full(gain),
                  full(w_router)],
        out_specs=[rows(d), pl.BlockSpec((tm * (d // LANES), LANES), lambda i: (i, 0)), rows(LANES),
                   pl.BlockSpec((SUBLANES, tm), lambda i: (0, i)), pl.BlockSpec((1, LANES), lambda i: (0, 0))],
        out_shape=[jax.ShapeDtypeStruct((n, d), F32), jax.ShapeDtypeStruct((n * (d // LANES), LANES), F32),
                   jax.ShapeDtypeStruct((n, LANES), F32), jax.ShapeDtypeStruct((SUBLANES, n), F32),
                   jax.ShapeDtypeStruct((1, LANES), F32)],
        scratch_shapes=[pltpu.VMEM((1, LANES), F32)],
        compiler_params=pltpu.CompilerParams(dimension_semantics=("arbitrary",),
                                             vmem_limit_bytes=VMEM_LIMIT),
        name="outproj",
    )(sb_out, rw_out, x2, w_sb, w_rw, gain, w_router)


def _start_tile_gather(src_ref, idx_ref, idx_base, dst_ref, dst_base, sem, n_rows):
    def issue(r, _):
        _tile_copy(src_ref, idx_ref[idx_base + r], dst_ref, dst_base + r, sem).start()
        return 0

    lax.fori_loop(0, n_rows, issue, 0, unroll=8)


def _wait_tile_gather(src_ref, dst_ref, sem, n_rows):
    def drain(r, _):
        _tile_copy(src_ref, 0, dst_ref, 0, sem).wait()
        return 0

    lax.fori_loop(0, n_rows, drain, 0, unroll=8)


def _srcmap_kernel(dest_ref, seg_ref, src_ref):
    def put(tok, _):
        for kk in range(TOP_K):
            src_ref[dest_ref[TOP_K * tok + kk]] = tok
        return 0

    lax.fori_loop(0, dest_ref.shape[0] // TOP_K, put, 0, unroll=8)

    def pad(e, _):
        first = seg_ref[0, e]

        def clear(r, _):
            src_ref[first + r] = 0
            return 0

        lax.fori_loop(seg_ref[1, e], seg_ref[2, e], clear, 0)
        return 0

    lax.fori_loop(0, seg_ref.shape[1], pad, 0)


def _srcmap(dest_flat, segments, n_rows):
    return pl.pallas_call(
        _srcmap_kernel,
        in_specs=[pl.BlockSpec(memory_space=pltpu.SMEM), pl.BlockSpec(memory_space=pltpu.SMEM)],
        out_specs=pl.BlockSpec(memory_space=pltpu.SMEM),
        out_shape=jax.ShapeDtypeStruct((n_rows,), jnp.int32),
        name="srcmap",
    )(dest_flat, segments)


def _experts_kernel(eid_ref, used_ref, src_ref, h_ref, wg_ref, wu_ref, wd_ref, o_ref,
                    x_ref, wgb_ref, wub_ref, wdb_ref, sems):
    i = pl.program_id(0)
    used = used_ref[0]
    slot = lax.rem(i, GATHER_DEPTH)
    rows = EXPERT_ROWS

    def start_block(block):
        dst = lax.rem(block, GATHER_DEPTH)
        _start_tile_gather(h_ref, src_ref, block * rows, x_ref, dst * rows, sems.at[dst], rows)

    for ahead in range(GATHER_DEPTH - 1):
        @pl.when((i == 0) & (ahead < used))
        def _():
            start_block(ahead)

    @pl.when(i + GATHER_DEPTH - 1 < used)
    def _():
        start_block(i + GATHER_DEPTH - 1)

    @pl.when((i == 0) | (eid_ref[i] != eid_ref[jnp.maximum(i - 1, 0)]))
    def _():
        wgb_ref[...] = wg_ref[...].astype(BF16)
        wub_ref[...] = wu_ref[...].astype(BF16)
        wdb_ref[...] = wd_ref[...].astype(BF16)

    @pl.when(i < used)
    def _():
        _wait_tile_gather(h_ref, x_ref, sems.at[slot], rows)
        xb = _load_row_tiles(x_ref, slot * (rows * SUBLANES), rows, SUBLANES).astype(BF16)
        hg = _mm(xb, wgb_ref[...])
        hu = _mm(xb, wub_ref[...])
        act = hg * _sigmoid(hg) * hu
        _store_row_tiles(o_ref, 0, _mm(act.astype(BF16), wdb_ref[...]))

    @pl.when(i >= used)
    def _():
        o_ref[...] = jnp.zeros_like(o_ref)


def _experts(blk_eid, n_used, src_rows, h2_tiles, w_g, w_u, w_d):
    n_blocks = src_rows.shape[0] // EXPERT_ROWS
    d, de = w_g.shape[1], w_g.shape[2]
    assert d == SUBLANES * LANES
    w_map = lambda i, eid, used, src: (eid[i], 0, 0)
    grid_spec = pltpu.PrefetchScalarGridSpec(
        num_scalar_prefetch=3,
        grid=(n_blocks,),
        in_specs=[pl.BlockSpec(memory_space=pl.ANY),
                  pl.BlockSpec((None, d, de), w_map), pl.BlockSpec((None, d, de), w_map),
                  pl.BlockSpec((None, de, d), w_map)],
        out_specs=pl.BlockSpec((EXPERT_ROWS * SUBLANES, LANES), lambda i, eid, used, src: (i, 0)),
        scratch_shapes=[pltpu.VMEM((GATHER_DEPTH * EXPERT_ROWS * SUBLANES, LANES), F32),
                        pltpu.VMEM((d, de), BF16), pltpu.VMEM((d, de), BF16), pltpu.VMEM((de, d), BF16),
                        pltpu.SemaphoreType.DMA((GATHER_DEPTH,))],
    )
    return pl.pallas_call(
        _experts_kernel,
        grid_spec=grid_spec,
        out_shape=jax.ShapeDtypeStruct((n_blocks * EXPERT_ROWS * SUBLANES, LANES), F32),
        compiler_params=pltpu.CompilerParams(dimension_semantics=("arbitrary",),
                                             vmem_limit_bytes=VMEM_LIMIT),
        name="experts",
    )(blk_eid, n_used, src_rows, h2_tiles, w_g, w_u, w_d)


def _combine_kernel(dest_ref, rows_ref, x1_ref, meta_ref, o_ref, got_ref, sems, *, tm):
    i = pl.program_id(0)
    slot = i % 2
    slot_rows = TOP_K * tm

    def start_tile(tile, dst_slot):
        def issue(t, _):
            for kk in range(TOP_K):
                _tile_copy(rows_ref, dest_ref[tile * slot_rows + TOP_K * t + kk], got_ref,
                           dst_slot * slot_rows + kk * tm + t, sems.at[dst_slot]).start(priority=kk)
            return 0

        lax.fori_loop(0, tm, issue, 0, unroll=4)

    @pl.when(i == 0)
    def _():
        start_tile(0, 0)

    @pl.when(i + 1 < pl.num_programs(0))
    def _():
        start_tile(i + 1, 1 - slot)

    _wait_tile_gather(rows_ref, got_ref, sems.at[slot], slot_rows)
    meta = meta_ref[...]
    y = x1_ref[...]
    for kk in range(TOP_K):
        y = y + _load_row_tiles(got_ref, (slot * slot_rows + kk * tm) * SUBLANES, tm, SUBLANES) * meta[:, 2 + kk:3 + kk]
    o_ref[...] = y


def _combine(dest_flat, exp_out_tiles, x1, meta, tm):
    n, d = x1.shape
    rows = lambda c: pl.BlockSpec((tm, c), lambda i, dest: (i, 0))
    grid_spec = pltpu.PrefetchScalarGridSpec(
        num_scalar_prefetch=1,
        grid=(n // tm,),
        in_specs=[pl.BlockSpec(memory_space=pl.ANY), rows(d), rows(LANES)],
        out_specs=rows(d),
        scratch_shapes=[pltpu.VMEM((2 * TOP_K * tm * SUBLANES, LANES), F32), pltpu.SemaphoreType.DMA((2,))],
    )
    return pl.pallas_call(
        functools.partial(_combine_kernel, tm=tm),
        grid_spec=grid_spec,
        out_shape=jax.ShapeDtypeStruct((n, d), F32),
        compiler_params=pltpu.CompilerParams(dimension_semantics=("arbitrary",),
                                             vmem_limit_bytes=VMEM_LIMIT),
        name="combine",
    )(dest_flat, exp_out_tiles, x1, meta)


def _layer(x, norm1_gain, w_in, sb_q_gain, sb_k_gain, rw_shift_mu, rw_w0, rw_w_up, rw_a0, rw_a_up, rw_g_up,
           rw_k_k, rw_k_a, rw_r_k, rw_ln_w, rw_ln_b, w_out, norm2_gain, w_router_group, w_router_expert,
           w_exp_gate, w_exp_up, w_exp_down):
    bsz, s, d = x.shape
    n = bsz * s
    sbw = d // 2
    rww = d - sbw
    n_decay, n_aaa = rw_w_up.shape[0], rw_a_up.shape[0]
    n_lora = n_decay + n_aaa + rw_g_up.shape[0]
    tm = min(256, n)
    row = lambda a: a.reshape(1, -1).astype(F32)

    x2 = x.reshape(n, d)
    w_bf = w_in.astype(BF16)
    heads = lambda gvec: jnp.tile(row(gvec), (1, sbw // HEAD_DIM))
    sb, rw, lo = _inproj(x2, row(norm1_gain), w_bf[:, :3 * sbw], w_bf[:, 3 * sbw:3 * sbw + 3 * rww],
                         w_bf[:, 3 * sbw + 3 * rww:], heads(sb_q_gain), heads(sb_k_gain), min(IN_TILE, n))
    sb_out = _sbattn(sb.reshape(bsz, s, 3 * sbw), min(SB_BLOCK, s))

    pad_rows = lambda w_up, start: jnp.zeros((n_lora, rww), BF16).at[start:start + w_up.shape[0]].set(
        w_up.astype(BF16))
    mu = row(rw_shift_mu)
    rw_out = _rwscan(rw.reshape(bsz, s, 3 * rww), lo.reshape(bsz, s, n_lora), mu[:, :3 * rww], mu[:, 3 * rww:],
                     row(rw_w0), row(rw_a0), row(rw_k_k), row(rw_k_a),
                     pad_rows(rw_w_up, 0), pad_rows(rw_a_up, n_decay), pad_rows(rw_g_up, n_decay + n_aaa),
                     row(rw_r_k), row(rw_ln_w), row(rw_ln_b), min(RW_TILE, s))

    w_out_bf = w_out.astype(BF16)
    w_router = jnp.zeros((d, LANES), F32).at[:, :N_GROUPS].set(w_router_group).at[
        :, N_GROUPS:N_GROUPS + N_EXPERTS].set(w_router_expert)
    x1, h2, meta, route, counts = _outproj(sb_out.reshape(n, sbw), rw_out.reshape(n, rww), x2, w_out_bf[:sbw],
                                    w_out_bf[sbw:], row(norm2_gain), w_router, tm)

    counts = counts[0, N_GROUPS:N_GROUPS + N_EXPERTS].astype(jnp.int32)
    padded = (counts + EXPERT_ROWS - 1) // EXPERT_ROWS * EXPERT_ROWS
    pad_end = jnp.cumsum(padded)
    pad_start = pad_end - padded
    eid = route[0:TOP_K].astype(jnp.int32)
    rank = route[4:4 + TOP_K].astype(jnp.int32)
    segment_start = jnp.sum(jnp.where(eid[None] == jnp.arange(N_EXPERTS, dtype=jnp.int32)[:, None, None],
                                      pad_start[:, None, None], 0), axis=0)
    dest = (segment_start + rank).T
    dest_flat = dest.reshape(-1)
    n_rows = n * TOP_K + N_EXPERTS * EXPERT_ROWS
    n_blocks = n_rows // EXPERT_ROWS
    block_start = jnp.arange(n_blocks, dtype=jnp.int32) * EXPERT_ROWS
    blk_eid = jnp.minimum(jnp.sum((pad_end[None, :] <= block_start[:, None]).astype(jnp.int32), axis=1),
                          N_EXPERTS - 1)
    n_used = (pad_end[-1:] // EXPERT_ROWS).astype(jnp.int32)

    tail = jnp.zeros((1,), jnp.int32)
    segments = jnp.stack([jnp.concatenate([pad_start, pad_end[-1:]]), jnp.concatenate([counts, tail]),
                          jnp.concatenate([padded, n_rows - pad_end[-1:]])]).astype(jnp.int32)
    src_rows = _srcmap(dest_flat, segments, n_rows)

    exp_out = _experts(blk_eid, n_used, src_rows, h2, w_exp_gate, w_exp_up, w_exp_down)
    return _combine(dest_flat, exp_out, x1, meta, tm).reshape(bsz, s, d)


def kernel(x, norm1_gain, w_in, sb_q_gain, sb_k_gain, rw_shift_mu, rw_w0, rw_w_up, rw_a0, rw_a_up, rw_g_up,
           rw_k_k, rw_k_a, rw_r_k, rw_ln_w, rw_ln_b, w_out, norm2_gain, w_router_group, w_router_expert,
           w_exp_gate, w_exp_up, w_exp_down):
    params = (norm1_gain, w_in, sb_q_gain, sb_k_gain, rw_shift_mu, rw_w0, rw_w_up, rw_a0, rw_a_up, rw_g_up,
              rw_k_k, rw_k_a, rw_r_k, rw_ln_w, rw_ln_b, w_out, norm2_gain, w_router_group, w_router_expert,
              w_exp_gate, w_exp_up, w_exp_down)
    for layer in range(norm1_gain.shape[0]):
        x = _layer(x, *(p[layer] for p in params))
    return x
```
